```python
import jax
import jax.numpy as jnp
from jax import lax
import numpy as np

D_MODEL = 1024
BATCH = 8
SEQ = 4096
DEPTH = 2

GRID_W = 64
CTX_LEN = 256
EPS = 1e-6

MLSTM_HEADS = 4
MLSTM_WIDTH = D_MODEL
MLSTM_HEAD_DIM = MLSTM_WIDTH // MLSTM_HEADS
MLSTM_CHUNK = 128
N_DIRS = 2
N_GATE_COLS = N_DIRS * 2 * MLSTM_HEADS

CONV_CHANNELS = D_MODEL // 2
DW_CONV_SIZE = 31

SG_WIDTH = D_MODEL // 2
SG_GROUPS = 4
SG_GROUP_DIM = SG_WIDTH // SG_GROUPS
SG_CHUNK = 128

N_BRANCHES = 3

N_EXPERTS = 16
EXPERT_FF = 2 * D_MODEL
CAPACITY_FACTOR = 2

OFF_Q = 0
OFF_K = MLSTM_WIDTH
OFF_V = 2 * MLSTM_WIDTH
OFF_GATES = 3 * MLSTM_WIDTH
N_STATE_COLS = OFF_GATES + N_GATE_COLS
OFF_O = N_STATE_COLS
OFF_CONV = OFF_O + MLSTM_WIDTH
OFF_SG = OFF_CONV + 2 * CONV_CHANNELS
OFF_MERGE = OFF_SG + 2 * SG_WIDTH
N_IN = OFF_MERGE + N_BRANCHES * D_MODEL

kernel_name = 'hybrid_mlstm_conformer_sgmlp_ecmoe_prefix_dit'


def rms_norm(x, w):
    xf = x.astype(jnp.float32)
    y = xf * lax.rsqrt(jnp.mean(xf * xf, axis=-1, keepdims=True) + EPS)
    return (y * w.astype(jnp.float32)).astype(x.dtype)


def layer_norm(x, w, b):
    xf = x.astype(jnp.float32)
    xc = xf - jnp.mean(xf, axis=-1, keepdims=True)
    y = xc * lax.rsqrt(jnp.mean(xc * xc, axis=-1, keepdims=True) + EPS)
    return (y * w.astype(jnp.float32) + b.astype(jnp.float32)).astype(x.dtype)


def modulate(h, shift, scale):
    return h * (1 + scale) + shift


def mlstm_scan(q, k, v, ig, lf, state):
    b_, h_, t_, dh = q.shape
    nc = t_ // MLSTM_CHUNK

    def to_chunks(a):
        a = a.reshape(a.shape[:2] + (nc, MLSTM_CHUNK) + a.shape[3:])
        return jnp.moveaxis(a, 2, 0)

    xs = (to_chunks(q), to_chunks(k), to_chunks(v), to_chunks(ig), to_chunks(lf))
    tril = jnp.tril(jnp.ones((MLSTM_CHUNK, MLSTM_CHUNK), dtype=bool))

    def step(carry, inp):
        c_prev, n_prev, m_prev = carry
        qc, kc, vc, ic, fc = inp
        bcum = jnp.cumsum(fc, axis=-1)
        log_d = bcum[..., :, None] - bcum[..., None, :] + ic[..., None, :]
        log_d = jnp.where(tril, log_d, -jnp.inf)
        m_inter = bcum + m_prev[..., None]
        m_t = jnp.maximum(m_inter, jnp.max(log_d, axis=-1))
        dmat = jnp.exp(log_d - m_t[..., None])
        inter = jnp.exp(m_inter - m_t)
        s = jnp.einsum('bhtd,bhsd->bhts', qc, kc) * dmat
        num = jnp.einsum('bhts,bhse->bhte', s, vc) + inter[..., None] * jnp.einsum('bhtd,bhde->bhte', qc, c_prev)
        den = jnp.sum(s, axis=-1) + inter * jnp.einsum('bhtd,bhd->bht', qc, n_prev)
        h = num / jnp.maximum(jnp.abs(den), jnp.exp(-m_t))[..., None]
        b_last = bcum[..., -1]
        g = b_last[..., None] - bcum + ic
        m_new = jnp.maximum(b_last + m_prev, jnp.max(g, axis=-1))
        decay = jnp.exp(b_last + m_prev - m_new)
        w = jnp.exp(g - m_new[..., None])
        c_new = decay[..., None, None] * c_prev + jnp.einsum('bhs,bhsd,bhse->bhde', w, kc, vc)
        n_new = decay[..., None] * n_prev + jnp.einsum('bhs,bhsd->bhd', w, kc)
        return (c_new, n_new, m_new), h

    state, hs = lax.scan(step, state, xs)
    h = jnp.moveaxis(hs, 0, 2).reshape(b_, h_, t_, dh)
    return h, state


def mlstm_inputs(z, gate_b):
    b_, t_, _ = z.shape
    f32 = jnp.float32

    def heads(a):
        return a.reshape(b_, t_, MLSTM_HEADS, MLSTM_HEAD_DIM).transpose(0, 2, 1, 3).astype(f32)

    q = heads(z[..., OFF_Q:OFF_K])
    k = heads(z[..., OFF_K:OFF_V]) * (MLSTM_HEAD_DIM ** -0.5)
    v = heads(z[..., OFF_V:OFF_GATES])
    gates = (z[..., OFF_GATES:N_STATE_COLS] + gate_b).astype(f32)
    gates = gates.reshape(b_, t_, N_DIRS, 2, MLSTM_HEADS).transpose(2, 3, 0, 4, 1)
    return q, k, v, gates[:, 0], jax.nn.log_sigmoid(gates[:, 1])


def mlstm_bidirectional(z_ctx, z_lat, gate_b):
    qc, kc, vc, ic, fc = mlstm_inputs(z_ctx, gate_b)
    ql, kl, vl, il, fl = mlstm_inputs(z_lat, gate_b)
    b_ = qc.shape[0]
    f32 = jnp.float32
    zero = (jnp.zeros((b_, MLSTM_HEADS, MLSTM_HEAD_DIM, MLSTM_HEAD_DIM), f32),
            jnp.zeros((b_, MLSTM_HEADS, MLSTM_HEAD_DIM), f32),
            jnp.zeros((b_, MLSTM_HEADS), f32))
    hc_f, st_f = mlstm_scan(qc, kc, vc, ic[0], fc[0], zero)
    hl_f, _ = mlstm_scan(ql, kl, vl, il[0], fl[0], st_f)
    rev = lambda a: jnp.flip(a, axis=2)
    hc_b, st_b = mlstm_scan(rev(qc), rev(kc), rev(vc), rev(ic[1]), rev(fc[1]), zero)
    hl_b, _ = mlstm_scan(rev(ql), rev(kl), rev(vl), rev(il[1]), rev(fl[1]), st_b)
    return hc_f + rev(hc_b), hl_f + rev(hl_b)


def head_rms_norm(h, w):
    y = h * lax.rsqrt(jnp.mean(h * h, axis=-1, keepdims=True) + EPS)
    b_, h_, t_, dh = h.shape
    return y.transpose(0, 2, 1, 3).reshape(b_, t_, h_ * dh) * w.astype(jnp.float32)


def conformer_conv(zc, dw_w, dw_b, ln_w, ln_b, w_out, on_grid):
    u = zc[..., :CONV_CHANNELS] * jax.nn.sigmoid(zc[..., CONV_CHANNELS:])
    b_, t_, ch = u.shape
    if on_grid:
        rows = t_ // GRID_W
        u = u.reshape(b_ * rows, GRID_W, ch)
    half = DW_CONV_SIZE // 2
    u = lax.conv_general_dilated(u, dw_w[:, None, :].astype(u.dtype), window_strides=(1,),
                                 padding=[(half, half)], dimension_numbers=('NWC', 'WIO', 'NWC'),
                                 feature_group_count=ch) + dw_b
    u = layer_norm(u.reshape(b_, t_, ch), ln_w, ln_b)
    return jax.nn.silu(u) @ w_out


def spatial_gating(zs, ln_w, ln_b, sg_w, sg_b, w_out):
    zs = jax.nn.gelu(zs)
    u, v = zs[..., :SG_WIDTH], zs[..., SG_WIDTH:]
    v = layer_norm(v, ln_w, ln_b)
    b_, t_, _ = v.shape
    nck = t_ // SG_CHUNK
    v = v.reshape(b_, nck, SG_CHUNK, SG_GROUPS, SG_GROUP_DIM)
    v = jnp.einsum('gts,bnsgc->bntgc', sg_w, v) + sg_b.T[:, :, None]
    return (u * v.reshape(b_, t_, SG_WIDTH)) @ w_out


def mixer_output(z, hm, on_grid, mlstm_norm_w, w_mlstm_out, conv_dw_w, conv_dw_b, conv_ln_w, conv_ln_b,
                 w_conv_out, sg_ln_w, sg_ln_b, sg_w, sg_b, w_sg_out, w_o):
    d = D_MODEL
    o = jax.nn.sigmoid(z[..., OFF_O:OFF_CONV])
    y_m = (head_rms_norm(hm, mlstm_norm_w).astype(z.dtype) * o) @ w_mlstm_out
    y_c = conformer_conv(z[..., OFF_CONV:OFF_SG], conv_dw_w, conv_dw_b, conv_ln_w, conv_ln_b, w_conv_out, on_grid)
    y_s = spatial_gating(z[..., OFF_SG:OFF_MERGE], sg_ln_w, sg_ln_b, sg_w, sg_b, w_sg_out)
    gm = jax.nn.sigmoid(z[..., OFF_MERGE:])
    merged = gm[..., :d] * y_m + gm[..., d:2 * d] * y_c + gm[..., 2 * d:] * y_s
    return merged @ w_o


def expert_choice_moe(h, router_w, router_b, w_gate, w_up, w_down):
    b_, t_, _ = h.shape
    cap = CAPACITY_FACTOR * t_ // N_EXPERTS
    logits = (h @ router_w + router_b).astype(jnp.float32)
    aff = jax.nn.softmax(logits, axis=-1)
    g, idx = lax.top_k(jnp.swapaxes(aff, 1, 2), cap)
    bidx = jnp.arange(b_)[:, None, None]
    xe = h[bidx, idx]
    a = jnp.einsum('becd,edf->becf', xe, w_gate)
    u = jnp.einsum('becd,edf->becf', xe, w_up)
    y = jnp.einsum('becf,efd->becd', jax.nn.silu(a) * u, w_down)
    y = y * g[..., None].astype(y.dtype)
    return jnp.zeros_like(h).at[bidx, idx].add(y)


def setup_inputs(seed: int = 0) -> dict:
    key = jax.random.key(seed)
    ks = jax.random.split(key, 32)
    f32 = jnp.float32
    L, D = DEPTH, D_MODEL

    def nrm(k, shape, scale):
        return jax.random.normal(k, shape, f32) * scale

    gate_offset = jnp.array([0.0, 3.0], f32)[None, None, :, None]
    return {
        'x': nrm(ks[0], (BATCH, SEQ, D), 1.0),
        'c': nrm(ks[1], (BATCH, D), 1.0),
        'ctx': nrm(ks[2], (BATCH, CTX_LEN, D), 1.0),
        'c_ctx': nrm(ks[3], (D,), 1.0),
        'ada_w': nrm(ks[4], (L, D, 6 * D), 0.5 * D ** -0.5),
        'ada_b': nrm(ks[5], (L, 6 * D), 0.02),
        'norm1_w': 1.0 + nrm(ks[6], (L, D), 0.05),
        'norm2_w': 1.0 + nrm(ks[7], (L, D), 0.05),
        'w_in': nrm(ks[8], (L, D, N_IN), D ** -0.5),
        'mlstm_gate_b': (nrm(ks[9], (L, N_DIRS, 2, MLSTM_HEADS), 0.1) + gate_offset).reshape(L, N_GATE_COLS),
        'mlstm_norm_w': 1.0 + nrm(ks[10], (L, MLSTM_WIDTH), 0.05),
        'w_mlstm_out': nrm(ks[11], (L, MLSTM_WIDTH, D), MLSTM_WIDTH ** -0.5),
        'conv_dw_w': nrm(ks[12], (L, DW_CONV_SIZE, CONV_CHANNELS), DW_CONV_SIZE ** -0.5),
        'conv_dw_b': nrm(ks[13], (L, CONV_CHANNELS), 0.02),
        'conv_ln_w': 1.0 + nrm(ks[14], (L, CONV_CHANNELS), 0.05),
        'conv_ln_b': nrm(ks[15], (L, CONV_CHANNELS), 0.02),
        'w_conv_out': nrm(ks[16], (L, CONV_CHANNELS, D), CONV_CHANNELS ** -0.5),
        'sg_ln_w': 1.0 + nrm(ks[17], (L, SG_WIDTH), 0.05),
        'sg_ln_b': nrm(ks[18], (L, SG_WIDTH), 0.02),
        'sg_w': nrm(ks[19], (L, SG_GROUPS, SG_CHUNK, SG_CHUNK), SG_CHUNK ** -0.5),
        'sg_b': 1.0 + nrm(ks[20], (L, SG_GROUPS, SG_CHUNK), 0.02),
        'w_sg_out': nrm(ks[21], (L, SG_WIDTH, D), SG_WIDTH ** -0.5),
        'w_o': nrm(ks[22], (L, D, D), D ** -0.5),
        'router_w': nrm(ks[23], (L, D, N_EXPERTS), D ** -0.5),
        'router_b': nrm(ks[24], (L, N_EXPERTS), 0.01),
        'expert_w_gate': nrm(ks[25], (L, N_EXPERTS, D, EXPERT_FF), D ** -0.5),
        'expert_w_up': nrm(ks[26], (L, N_EXPERTS, D, EXPERT_FF), D ** -0.5),
        'expert_w_down': nrm(ks[27], (L, N_EXPERTS, EXPERT_FF, D), EXPERT_FF ** -0.5),
        'final_norm_w': 1.0 + nrm(ks[28], (D,), 0.05),
    }


def reference(x, c, ctx, c_ctx, ada_w, ada_b, norm1_w, norm2_w, w_in, mlstm_gate_b, mlstm_norm_w,
              w_mlstm_out, conv_dw_w, conv_dw_b, conv_ln_w, conv_ln_b, w_conv_out, sg_ln_w, sg_ln_b,
              sg_w, sg_b, w_sg_out, w_o, router_w, router_b, expert_w_gate, expert_w_up, expert_w_down,
              final_norm_w):
    for layer in range(DEPTH):
        need_ctx = layer < DEPTH - 1
        mod_lat = jax.nn.silu(c) @ ada_w[layer] + ada_b[layer]
        mod_ctx = jax.nn.silu(c_ctx) @ ada_w[layer] + ada_b[layer]
        sh1, sc1, g1, sh2, sc2, g2 = jnp.split(mod_lat[:, None, :], 6, axis=-1)
        csh1, csc1, cg1, csh2, csc2, cg2 = jnp.split(mod_ctx, 6, axis=-1)

        h_lat = modulate(rms_norm(x, norm1_w[layer]), sh1, sc1)
        h_ctx = modulate(rms_norm(ctx, norm1_w[layer]), csh1, csc1)
        z_lat = h_lat @ w_in[layer]
        z_ctx = h_ctx @ (w_in[layer] if need_ctx else w_in[layer][:, :N_STATE_COLS])
        hm_ctx, hm_lat = mlstm_bidirectional(z_ctx[..., :N_STATE_COLS], z_lat[..., :N_STATE_COLS],
                                             mlstm_gate_b[layer])
        branch_w = (mlstm_norm_w[layer], w_mlstm_out[layer], conv_dw_w[layer], conv_dw_b[layer],
                    conv_ln_w[layer], conv_ln_b[layer], w_conv_out[layer], sg_ln_w[layer], sg_ln_b[layer],
                    sg_w[layer], sg_b[layer], w_sg_out[layer], w_o[layer])
        y_lat = mixer_output(z_lat, hm_lat, True, *branch_w)
        x = x + g1 * y_lat
        if need_ctx:
            y_ctx = mixer_output(z_ctx, hm_ctx, False, *branch_w)
            ctx = ctx + cg1 * y_ctx

        moe_w = (router_w[layer], router_b[layer], expert_w_gate[layer], expert_w_up[layer], expert_w_down[layer])
        x = x + g2 * expert_choice_moe(modulate(rms_norm(x, norm2_w[layer]), sh2, sc2), *moe_w)
        if need_ctx:
            ctx = ctx + cg2 * expert_choice_moe(modulate(rms_norm(ctx, norm2_w[layer]), csh2, csc2), *moe_w)

    return rms_norm(x, final_norm_w)
```

```python
import functools

import jax
import jax.numpy as jnp
from jax import lax
from jax.experimental import pallas as pl
from jax.experimental.pallas import tpu as pltpu

F32 = jnp.float32
BF16 = jnp.bfloat16
I32 = jnp.int32
U32 = jnp.uint32

EPS = 1e-6
LANES = 128
VMEM_LIMIT = 56 * 1024 * 1024

N_HEADS = 4
MLSTM_CHUNK = 128
N_DIRS = 2
N_GATE_COLS = N_DIRS * 2 * N_HEADS
DW_CONV_SIZE = 31
CONV_PAD = 16
SG_GROUPS = 4
SG_CHUNK = 128
N_EXPERTS = 16
CAPACITY_FACTOR = 2
GRID_W = 64


def _params(sem, vmem=VMEM_LIMIT):
    return pltpu.CompilerParams(dimension_semantics=sem, vmem_limit_bytes=vmem)


def _sigmoid(v):
    return jax.nn.sigmoid(v)


def _split_bf16(a):
    hi = a.astype(BF16)
    lo = (a - hi.astype(F32)).astype(BF16)
    return hi, lo


def _norm_mod(x, nw, shift, scale):
    ms = jnp.mean(x * x, axis=-1, keepdims=True)
    y = x * lax.rsqrt(ms + EPS) * nw
    return y * (1.0 + scale) + shift


def _modulation_kernel(a_ref, w_ref, b_ref, o_ref):
    a = a_ref[...]
    a = a * _sigmoid(a)
    a_hi, a_lo = _split_bf16(a)
    w_hi, w_lo = _split_bf16(w_ref[...])
    acc = jnp.dot(a_hi, w_hi, preferred_element_type=F32)
    acc += jnp.dot(a_hi, w_lo, preferred_element_type=F32)
    acc += jnp.dot(a_lo, w_hi, preferred_element_type=F32)
    o_ref[...] = acc + b_ref[...]


def _modulation(cond, w, b, tn=512):
    m, d = cond.shape
    n = w.shape[1]
    return pl.pallas_call(
        _modulation_kernel,
        grid=(n // tn,),
        in_specs=[pl.BlockSpec((m, d), lambda j: (0, 0)),
                  pl.BlockSpec((d, tn), lambda j: (0, j)),
                  pl.BlockSpec((1, tn), lambda j: (0, j))],
        out_specs=pl.BlockSpec((m, tn), lambda j: (0, j)),
        out_shape=jax.ShapeDtypeStruct((m, n), F32),
        compiler_params=_params(("arbitrary",)),
        name="modulation",
    )(cond, w, b.reshape(1, n))


def _in_proj_kernel(x_ref, nw_ref, sh_ref, sc_ref, w_ref, wg_ref, gb_ref, z_ref, g_ref, gt_ref, h_scr):
    j = pl.program_id(2)

    @pl.when(j == 0)
    def _():
        h = _norm_mod(x_ref[0], nw_ref[...], sh_ref[0], sc_ref[0])
        h_hi, h_lo = _split_bf16(h)
        h_scr[...] = h_hi
        wg = wg_ref[...]
        raw = jnp.dot(h_hi, wg, preferred_element_type=F32) + jnp.dot(h_lo, wg, preferred_element_type=F32)
        raw = raw[:, :N_GATE_COLS] + raw[:, N_GATE_COLS:2 * N_GATE_COLS] + gb_ref[...]
        col = lax.broadcasted_iota(I32, raw.shape, 1)
        is_forget = ((col >> 2) & 1) == 1
        logsig = jnp.minimum(raw, 0.0) - jnp.log(1.0 + jnp.exp(-jnp.abs(raw)))
        g = jnp.where(is_forget, logsig, raw)
        gpad = jnp.concatenate([g, jnp.zeros((g.shape[0], LANES - N_GATE_COLS), F32)], axis=1)
        g_ref[0] = gpad
        gt_ref[0] = gpad.T[:N_GATE_COLS, :]

    z_ref[0] = jnp.dot(h_scr[...], w_ref[...], preferred_element_type=F32).astype(BF16)


def _in_proj(x, nw, shift, scale, w_main, w_gate2, gate_b, tm, tn):
    b, t, d = x.shape
    n = w_main.shape[1]
    tm = min(tm, t)
    return pl.pallas_call(
        _in_proj_kernel,
        grid=(b, t // tm, n // tn),
        in_specs=[pl.BlockSpec((1, tm, d), lambda bi, i, j: (bi, i, 0)),
                  pl.BlockSpec((1, d), lambda bi, i, j: (0, 0)),
                  pl.BlockSpec((1, 1, d), lambda bi, i, j: (bi, 0, 0)),
                  pl.BlockSpec((1, 1, d), lambda bi, i, j: (bi, 0, 0)),
                  pl.BlockSpec((d, tn), lambda bi, i, j: (0, j)),
                  pl.BlockSpec((d, LANES), lambda bi, i, j: (0, 0)),
                  pl.BlockSpec((1, N_GATE_COLS), lambda bi, i, j: (0, 0))],
        out_specs=[pl.BlockSpec((1, tm, tn), lambda bi, i, j: (bi, i, j)),
                   pl.BlockSpec((1, tm, LANES), lambda bi, i, j: (bi, i, 0)),
                   pl.BlockSpec((1, N_GATE_COLS, tm), lambda bi, i, j: (bi, 0, i))],
        out_shape=[jax.ShapeDtypeStruct((b, t, n), BF16),
                   jax.ShapeDtypeStruct((b, t, LANES), F32),
                   jax.ShapeDtypeStruct((b, N_GATE_COLS, t), F32)],
        scratch_shapes=[pltpu.VMEM((tm, d), BF16)],
        compiler_params=_params(("arbitrary", "arbitrary", "arbitrary")),
        name="in_proj",
    )(x, nw.reshape(1, d), shift, scale, w_main, w_gate2, gate_b.reshape(1, N_GATE_COLS))


def _mlstm_kernel(*refs, dh, has_init, nc):
    if has_init:
        (zf_ref, zb_ref, gf_ref, gb_ref, gtf_ref, gtb_ref, c0_ref, n0_ref, m0_ref,
         hf_ref, hb_ref, cN_ref, nN_ref, mN_ref, c_scr, n_scr, m_scr) = refs
    else:
        (zf_ref, zb_ref, gf_ref, gb_ref, gtf_ref, gtb_ref,
         hf_ref, hb_ref, cN_ref, nN_ref, mN_ref, c_scr, n_scr, m_scr) = refs
    c = pl.program_id(1)
    L = MLSTM_CHUNK
    width = N_HEADS * dh

    @pl.when(c == 0)
    def _():
        if has_init:
            c_scr[...] = c0_ref[0]
            n_scr[...] = n0_ref[0]
            m_scr[...] = m0_ref[0]
        else:
            c_scr[...] = jnp.zeros_like(c_scr)
            n_scr[...] = jnp.zeros_like(n_scr)
            m_scr[...] = jnp.zeros_like(m_scr)

    row = lax.broadcasted_iota(I32, (L, L), 0)
    colm = lax.broadcasted_iota(I32, (L, L), 1)
    lower = (colm <= row)
    upper = (colm >= row)
    lower_f = lower.astype(F32)
    upper_f = upper.astype(F32)
    hp = lax.Precision.HIGHEST

    for d in range(N_DIRS):
        z_ref, g_ref, gt_ref, h_ref = ((zf_ref, gf_ref, gtf_ref, hf_ref), (zb_ref, gb_ref, gtb_ref, hb_ref))[d]
        keep = lower if d == 0 else upper
        g = g_ref[0]
        gt = gt_ref[0]
        cum_col = jnp.dot(lower_f if d == 0 else upper_f, g, precision=hp, preferred_element_type=F32)
        cum_row = jnp.dot(gt, upper_f if d == 0 else lower_f, precision=hp, preferred_element_type=F32)
        for hh in range(N_HEADS):
            si = d * N_HEADS + hh
            ci = d * 2 * N_HEADS + hh
            cf = ci + N_HEADS
            q = z_ref[0, :, hh * dh:(hh + 1) * dh]
            k = z_ref[0, :, width + hh * dh:width + (hh + 1) * dh] * jnp.asarray(dh ** -0.5, BF16)
            v = z_ref[0, :, 2 * width + hh * dh:2 * width + (hh + 1) * dh]
            i_col = g[:, ci:ci + 1]
            b_col = cum_col[:, cf:cf + 1]
            i_row = gt[ci:ci + 1, :]
            b_row = cum_row[cf:cf + 1, :]
            b_last = b_row[:, L - 1:L] if d == 0 else b_row[:, 0:1]
            m_prev = m_scr[si][:, 0:1]
            c_prev = c_scr[si]
            n_prev = n_scr[si]

            log_d = jnp.where(keep, b_col - b_row + i_row, -jnp.inf)
            m_inter = b_col + m_prev
            m_t = jnp.maximum(m_inter, jnp.max(log_d, axis=-1, keepdims=True))
            dmat = jnp.exp(log_d - m_t)
            inter = jnp.exp(m_inter - m_t)
            s = lax.dot_general(q, k, (((1,), (1,)), ((), ())), preferred_element_type=F32) * dmat
            num = jnp.dot(s.astype(BF16), v, preferred_element_type=F32)
            num += inter * jnp.dot(q, c_prev.astype(BF16), preferred_element_type=F32)
            qn = jnp.sum(q.astype(F32) * n_prev, axis=-1, keepdims=True)
            den = jnp.sum(s, axis=-1, keepdims=True) + inter * qn
            scale = 1.0 / jnp.maximum(jnp.abs(den), jnp.exp(-m_t))
            h_ref[0, :, hh * dh:(hh + 1) * dh] = (num * scale).astype(h_ref.dtype)

            g_row = b_last - b_row + i_row
            m_new = jnp.maximum(b_last + m_prev, jnp.max(g_row, axis=-1, keepdims=True))
            decay = jnp.exp(b_last + m_prev - m_new)
            w_col = jnp.exp(b_last - b_col + i_col - m_new)
            wv = (w_col * v.astype(F32)).astype(BF16)
            kv = lax.dot_general(k, wv, (((0,), (0,)), ((), ())), preferred_element_type=F32)
            c_scr[si] = decay * c_prev + kv
            n_scr[si] = decay * n_prev + jnp.sum(w_col * k.astype(F32), axis=0, keepdims=True)
            m_scr[si] = jnp.broadcast_to(m_new, (1, LANES))

    @pl.when(c == nc - 1)
    def _():
        cN_ref[0] = c_scr[...]
        nN_ref[0] = n_scr[...]
        mN_ref[0] = m_scr[...]


def _mlstm(z, zblk, gates, gates_t, init, dh):
    b, t, _ = z.shape
    L = MLSTM_CHUNK
    nc = t // L
    width = N_HEADS * dh
    ns = N_DIRS * N_HEADS
    fwd = lambda bi, c: (bi, c, 0)
    bwd = lambda bi, c: (bi, nc - 1 - c, 0)
    fwd_t = lambda bi, c: (bi, 0, c)
    bwd_t = lambda bi, c: (bi, 0, nc - 1 - c)
    st4 = lambda bi, c: (bi, 0, 0, 0)
    in_specs = [pl.BlockSpec((1, L, 3 * width), lambda bi, c: (bi, c, zblk)),
                pl.BlockSpec((1, L, 3 * width), lambda bi, c: (bi, nc - 1 - c, zblk)),
                pl.BlockSpec((1, L, LANES), fwd), pl.BlockSpec((1, L, LANES), bwd),
                pl.BlockSpec((1, N_GATE_COLS, L), fwd_t), pl.BlockSpec((1, N_GATE_COLS, L), bwd_t)]
    args = [z, z, gates, gates, gates_t, gates_t]
    state_specs = [pl.BlockSpec((1, ns, dh, dh), st4), pl.BlockSpec((1, ns, 1, dh), st4),
                   pl.BlockSpec((1, ns, 1, LANES), st4)]
    state_shapes = [jax.ShapeDtypeStruct((b, ns, dh, dh), F32), jax.ShapeDtypeStruct((b, ns, 1, dh), F32),
                    jax.ShapeDtypeStruct((b, ns, 1, LANES), F32)]
    if init is not None:
        in_specs += state_specs
        args += list(init)
    outs = pl.pallas_call(
        functools.partial(_mlstm_kernel, dh=dh, has_init=init is not None, nc=nc),
        grid=(b, nc),
        in_specs=in_specs,
        out_specs=[pl.BlockSpec((1, L, width), fwd), pl.BlockSpec((1, L, width), bwd)] + state_specs,
        out_shape=[jax.ShapeDtypeStruct((b, t, width), BF16), jax.ShapeDtypeStruct((b, t, width), BF16)] + state_shapes,
        scratch_shapes=[pltpu.VMEM((ns, dh, dh), F32), pltpu.VMEM((ns, 1, dh), F32), pltpu.VMEM((ns, 1, LANES), F32)],
        compiler_params=_params(("arbitrary", "arbitrary")),
        name="mlstm",
    )(*args)
    return outs[0], outs[1], tuple(outs[2:])


def _layer_norm(v, w, b):
    mu = jnp.mean(v, axis=-1, keepdims=True)
    vc = v - mu
    var = jnp.mean(vc * vc, axis=-1, keepdims=True)
    return vc * lax.rsqrt(var + EPS) * w + b


def _gelu_tanh(v):
    return 0.5 * v * (1.0 + jnp.tanh(0.7978845608028654 * (v + 0.044715 * (v * v * v))))


def _mixer_kernel(z_ref, hf_ref, hb_ref, x_ref, g1_ref, mnw_ref, wm_ref, dww_ref, dwb_ref, clw_ref, clb_ref,
                  wc_ref, slw_ref, slb_ref, sgw_ref, sgb_ref, ws_ref, wo_ref, o_ref, pad_scr, conv_scr,
                  *, d, row_len):
    tm = z_ref.shape[1]
    dh = d // N_HEADS
    cc = d // 2
    off_conv, off_sg, off_merge = d, 2 * d, 3 * d

    y_m = jnp.zeros((tm, d), F32)
    for hh in range(N_HEADS):
        sl = slice(hh * dh, (hh + 1) * dh)
        hm = hf_ref[0, :, sl].astype(F32) + hb_ref[0, :, sl].astype(F32)
        yn = hm * lax.rsqrt(jnp.mean(hm * hm, axis=-1, keepdims=True) + EPS) * mnw_ref[:, sl]
        og = _sigmoid(z_ref[0, :, sl].astype(F32))
        y_m += jnp.dot((yn * og).astype(BF16), wm_ref[sl, :], preferred_element_type=F32)

    u = z_ref[0, :, off_conv:off_conv + cc].astype(F32) * _sigmoid(z_ref[0, :, off_conv + cc:off_conv + 2 * cc].astype(F32))
    n_rows = tm // row_len
    zpad = jnp.zeros((CONV_PAD, cc), F32)
    for r in range(n_rows):
        pad_scr[r, 0:CONV_PAD, :] = zpad
        pad_scr[r, CONV_PAD:CONV_PAD + row_len, :] = u[r * row_len:(r + 1) * row_len, :]
        pad_scr[r, CONV_PAD + row_len:2 * CONV_PAD + row_len, :] = zpad
    base = CONV_PAD - DW_CONV_SIZE // 2
    for cb in range(cc // LANES):
        ls = slice(cb * LANES, (cb + 1) * LANES)
        taps = [dww_ref[kk:kk + 1, ls] for kk in range(DW_CONV_SIZE)]
        for r in range(n_rows):
            acc = jnp.zeros((row_len, LANES), F32)
            for kk in range(DW_CONV_SIZE):
                acc += pad_scr[r, base + kk:base + kk + row_len, ls] * taps[kk]
            conv_scr[r * row_len:(r + 1) * row_len, ls] = acc
    cv = _layer_norm(conv_scr[...] + dwb_ref[...], clw_ref[...], clb_ref[...])
    cv = cv * _sigmoid(cv)
    y_c = jnp.dot(cv.astype(BF16), wc_ref[...], preferred_element_type=F32)

    su = _gelu_tanh(z_ref[0, :, off_sg:off_sg + cc].astype(F32))
    sv = _gelu_tanh(z_ref[0, :, off_sg + cc:off_sg + 2 * cc].astype(F32))
    sv = _layer_norm(sv, slw_ref[...], slb_ref[...]).astype(BF16)
    gd = cc // SG_GROUPS
    for ch in range(tm // SG_CHUNK):
        rs = slice(ch * SG_CHUNK, (ch + 1) * SG_CHUNK)
        parts = []
        for gi in range(SG_GROUPS):
            mixed = jnp.dot(sgw_ref[gi], sv[rs, gi * gd:(gi + 1) * gd], preferred_element_type=F32)
            parts.append(mixed + sgb_ref[:, gi:gi + 1])
        gated = su[rs, :] * jnp.concatenate(parts, axis=1)
        conv_scr[rs, :] = gated
    y_s = jnp.dot(conv_scr[...].astype(BF16), ws_ref[...], preferred_element_type=F32)

    gm0 = _sigmoid(z_ref[0, :, off_merge:off_merge + d].astype(F32))
    gm1 = _sigmoid(z_ref[0, :, off_merge + d:off_merge + 2 * d].astype(F32))
    gm2 = _sigmoid(z_ref[0, :, off_merge + 2 * d:off_merge + 3 * d].astype(F32))
    merged = gm0 * y_m + gm1 * y_c + gm2 * y_s
    y = jnp.dot(merged.astype(BF16), wo_ref[...], preferred_element_type=F32)
    o_ref[0] = x_ref[0] + g1_ref[0] * y


def _mixer_out(z, zblk, hf, hb, x, g1, p, row_len, tm):
    b, t, d = x.shape
    cc = d // 2
    tm = min(tm, t)
    nz = 6 * d
    full = lambda a: pl.BlockSpec(a.shape, lambda bi, i: (0,) * a.ndim)
    row = lambda a: a.reshape(1, -1)
    consts = [row(p["mlstm_norm_w"]), p["w_mlstm_out"], p["conv_dw_w"], row(p["conv_dw_b"]), row(p["conv_ln_w"]),
              row(p["conv_ln_b"]), p["w_conv_out"], row(p["sg_ln_w"]), row(p["sg_ln_b"]), p["sg_w"], p["sg_b"].T,
              p["w_sg_out"], p["w_o"]]
    tok = lambda w: pl.BlockSpec((1, tm, w), lambda bi, i: (bi, i, 0))
    return pl.pallas_call(
        functools.partial(_mixer_kernel, d=d, row_len=row_len),
        grid=(b, t // tm),
        in_specs=[pl.BlockSpec((1, tm, nz), lambda bi, i: (bi, i, zblk)), tok(d), tok(d), tok(d),
                  pl.BlockSpec((1, 1, d), lambda bi, i: (bi, 0, 0))] + [full(a) for a in consts],
        out_specs=tok(d),
        out_shape=jax.ShapeDtypeStruct((b, t, d), F32),
        scratch_shapes=[pltpu.VMEM((tm // row_len, row_len + 2 * CONV_PAD, cc), F32), pltpu.VMEM((tm, cc), F32)],
        compiler_params=_params(("arbitrary", "arbitrary")),
        name="mixer_out",
    )(z, hf, hb, x, g1, *consts)


def _router_kernel(x_ref, nw_ref, sh_ref, sc_ref, wr_ref, rb_ref, hp_ref, at_ref):
    h = _norm_mod(x_ref[0], nw_ref[...], sh_ref[0], sc_ref[0])
    h_hi, h_lo = _split_bf16(h)
    wr = wr_ref[...]
    raw = jnp.dot(h_hi, wr, preferred_element_type=F32) + jnp.dot(h_lo, wr, preferred_element_type=F32)
    logits = raw[:, :N_EXPERTS] + raw[:, N_EXPERTS:2 * N_EXPERTS] + rb_ref[...]
    mx = jnp.max(logits, axis=-1, keepdims=True)
    ex = jnp.exp(logits - mx)
    aff = ex / jnp.sum(ex, axis=-1, keepdims=True)
    apad = jnp.concatenate([aff, jnp.zeros((aff.shape[0], LANES - N_EXPERTS), F32)], axis=1)
    at_ref[0] = apad.T[:N_EXPERTS, :]
    half = h.shape[1] // 2
    bits = lax.bitcast_convert_type(h_hi.astype(F32), U32)
    hp_ref[0] = (bits[:, :half] >> 16) | (bits[:, half:] & jnp.uint32(0xFFFF0000))


def _router(x, nw, shift, scale, wr2, rb, tm):
    b, t, d = x.shape
    tm = min(tm, t)
    return pl.pallas_call(
        _router_kernel,
        grid=(b, t // tm),
        in_specs=[pl.BlockSpec((1, tm, d), lambda bi, i: (bi, i, 0)),
                  pl.BlockSpec((1, d), lambda bi, i: (0, 0)),
                  pl.BlockSpec((1, 1, d), lambda bi, i: (bi, 0, 0)),
                  pl.BlockSpec((1, 1, d), lambda bi, i: (bi, 0, 0)),
                  pl.BlockSpec((d, LANES), lambda bi, i: (0, 0)),
                  pl.BlockSpec((1, N_EXPERTS), lambda bi, i: (0, 0))],
        out_specs=[pl.BlockSpec((1, tm, d // 2), lambda bi, i: (bi, i, 0)),
                   pl.BlockSpec((1, N_EXPERTS, tm), lambda bi, i: (bi, 0, i))],
        out_shape=[jax.ShapeDtypeStruct((b, t, d // 2), U32), jax.ShapeDtypeStruct((b, N_EXPERTS, t), F32)],
        compiler_params=_params(("arbitrary", "arbitrary")),
        name="router",
    )(x, nw.reshape(1, d), shift, scale, wr2, rb.reshape(1, N_EXPERTS))


def _prefix_count(mask, tri):
    e, t = mask.shape
    nb = t // LANES
    stacked = jnp.concatenate([mask[:, c * LANES:(c + 1) * LANES] for c in range(nb)], axis=0).astype(BF16)
    local = jnp.dot(stacked, tri, preferred_element_type=F32)
    out = []
    off = jnp.zeros((e, 1), F32)
    for c in range(nb):
        blk = local[c * e:(c + 1) * e, :]
        out.append(blk + off)
        off = off + blk[:, LANES - 1:LANES]
    return jnp.concatenate(out, axis=1)


def _select_kernel(at_ref, idx_ref, gcm_ref, cnt_scr, asel_scr, icol_scr, *, cap, jt):
    a = at_ref[0]
    e, t = a.shape
    bits = lax.bitcast_convert_type(a, I32)

    def search(i, v):
        cand = v | jnp.left_shift(jnp.int32(1), 30 - i)
        cnt = jnp.sum((bits >= cand).astype(I32), axis=1, keepdims=True)
        return jnp.where(cnt >= cap, cand, v)

    thr = lax.fori_loop(0, 31, search, jnp.zeros((e, 1), I32))
    r = lax.broadcasted_iota(I32, (LANES, LANES), 0)
    s = lax.broadcasted_iota(I32, (LANES, LANES), 1)
    tri = (r <= s).astype(BF16)
    gt = bits > thr
    eq = bits == thr
    need = (cap - jnp.sum(gt.astype(I32), axis=1, keepdims=True)).astype(F32)
    sel = gt | (eq & (_prefix_count(eq.astype(F32), tri) <= need))
    cnt = _prefix_count(sel.astype(F32), tri)
    asel = jnp.where(sel, a, 0.0)
    for ei in range(e):
        cnt_scr[ei] = cnt[ei:ei + 1, :]
        asel_scr[ei] = asel[ei:ei + 1, :]
    icol_scr[...] = jnp.zeros_like(icol_scr)
    gcm_ref[0] = jnp.zeros_like(gcm_ref[0])
    lane = lax.broadcasted_iota(I32, (jt, LANES), 1)

    def per_expert(ei, carry):
        for jb in range(cap // jt):
            jcol = (lax.broadcasted_iota(I32, (jt, 1), 0) + jb * jt).astype(F32)
            acc_n = jnp.zeros((jt, LANES), F32)
            acc_g = jnp.zeros((jt, LANES), F32)
            for tb in range(t // LANES):
                cn = cnt_scr[ei, :, tb * LANES:(tb + 1) * LANES]
                av = asel_scr[ei, :, tb * LANES:(tb + 1) * LANES]
                acc_n += jnp.where(cn <= jcol, 1.0, 0.0)
                acc_g = jnp.maximum(acc_g, jnp.where(cn == jcol + 1.0, av, 0.0))
            tok = jnp.sum(acc_n, axis=1, keepdims=True)
            gv = jnp.max(acc_g, axis=1, keepdims=True)
            rows = slice(jb * jt, (jb + 1) * jt)
            icol_scr[rows, :] = jnp.where(lane == ei, tok, icol_scr[rows, :])
            gcm_ref[0, rows, :] = jnp.where(lane == ei, gv, gcm_ref[0, rows, :])
        return carry

    lax.fori_loop(0, e, per_expert, 0)
    idx_ref[0, :, 0, :] = icol_scr[...].T[:e, :cap].astype(I32)


def _route_select(aff_t, cap):
    b, e, t = aff_t.shape
    jt = min(cap, 64)
    return pl.pallas_call(
        functools.partial(_select_kernel, cap=cap, jt=jt),
        grid=(b,),
        in_specs=[pl.BlockSpec((1, e, t), lambda bi: (bi, 0, 0))],
        out_specs=[pl.BlockSpec((1, e, 1, cap), lambda bi: (bi, 0, 0, 0)),
                   pl.BlockSpec((1, cap, LANES), lambda bi: (bi, 0, 0))],
        out_shape=[jax.ShapeDtypeStruct((b, e, 1, cap), I32), jax.ShapeDtypeStruct((b, cap, LANES), F32)],
        scratch_shapes=[pltpu.VMEM((e, 1, t), F32), pltpu.VMEM((e, 1, t), F32), pltpu.VMEM((max(cap, LANES), LANES), F32)],
        compiler_params=_params(("arbitrary",)),
        name="route_select",
    )(aff_t)


def _moe_kernel(idx_ref, gcm_ref, hp_ref, g2_ref, wg_ref, wu_ref, wd_ref, x_hbm, o_hbm,
                xe32_scr, xe_scr, y_scr, acc_scr, sem_in, sem_out, *, gsz, cap, n_groups, n_f):
    grp = pl.program_id(0)
    e = pl.program_id(1)
    f = pl.program_id(2)
    half = hp_ref.shape[2]

    def in_copy(gi):
        return pltpu.make_async_copy(x_hbm.at[pl.ds(gi * gsz, gsz)], acc_scr, sem_in)

    def out_copy(gi):
        return pltpu.make_async_copy(acc_scr, o_hbm.at[pl.ds(gi * gsz, gsz)], sem_out)

    @pl.when((e == 0) & (f == 0))
    def _():
        @pl.when(grp > 0)
        def _():
            out_copy(grp - 1).wait()
        in_copy(grp).start()

    @pl.when(f == 0)
    def _():
        for si in range(gsz):
            def gather(j, carry):
                tkn = idx_ref[si, 0, 0, j]
                xe32_scr[pl.ds(si * cap + j, 1), :] = hp_ref[si, pl.ds(tkn, 1), :]
                return carry
            lax.fori_loop(0, cap, gather, 0, unroll=8)
        p = xe32_scr[...]
        lo = lax.bitcast_convert_type(p << 16, F32)
        hi = lax.bitcast_convert_type(p & jnp.uint32(0xFFFF0000), F32)
        xe_scr[:, :half] = lo.astype(BF16)
        xe_scr[:, half:] = hi.astype(BF16)

    xe = xe_scr[...]
    a = jnp.dot(xe, wg_ref[0], preferred_element_type=F32)
    u = jnp.dot(xe, wu_ref[0], preferred_element_type=F32)
    hid = (a * _sigmoid(a) * u).astype(BF16)
    part = jnp.dot(hid, wd_ref[0], preferred_element_type=F32)

    @pl.when(f == 0)
    def _():
        y_scr[...] = part

    @pl.when(f > 0)
    def _():
        y_scr[...] += part

    @pl.when(f == n_f - 1)
    def _():
        lane = lax.broadcasted_iota(I32, (cap, LANES), 1)
        for si in range(gsz):
            rows = slice(si * cap, (si + 1) * cap)
            gate = jnp.sum(jnp.where(lane == e, gcm_ref[si], 0.0), axis=1, keepdims=True)
            y_scr[rows, :] = y_scr[rows, :] * gate * g2_ref[si]

        @pl.when(e == 0)
        def _():
            in_copy(grp).wait()

        for si in range(gsz):
            def scatter(j, carry):
                tkn = idx_ref[si, 0, 0, j]
                acc_scr[si, pl.ds(tkn, 1), :] += y_scr[pl.ds(si * cap + j, 1), :]
                return carry
            lax.fori_loop(0, cap, scatter, 0, unroll=8)

    @pl.when((e == pl.num_programs(1) - 1) & (f == n_f - 1))
    def _():
        out_copy(grp).start()

        @pl.when(grp == n_groups - 1)
        def _():
            out_copy(grp).wait()


def _moe_ffn(idx, gcm, hp, g2, x, wg, wu, wd, gsz, tf):
    b, t, d = x.shape
    e, cap = idx.shape[1], idx.shape[3]
    ff = wg.shape[2]
    n_groups, n_f = b // gsz, ff // tf
    return pl.pallas_call(
        functools.partial(_moe_kernel, gsz=gsz, cap=cap, n_groups=n_groups, n_f=n_f),
        grid=(n_groups, e, n_f),
        in_specs=[pl.BlockSpec((gsz, 1, 1, cap), lambda gi, ei, fi: (gi, ei, 0, 0), memory_space=pltpu.SMEM),
                  pl.BlockSpec((gsz, cap, LANES), lambda gi, ei, fi: (gi, 0, 0)),
                  pl.BlockSpec((gsz, t, d // 2), lambda gi, ei, fi: (gi, 0, 0)),
                  pl.BlockSpec((gsz, 1, d), lambda gi, ei, fi: (gi, 0, 0)),
                  pl.BlockSpec((1, d, tf), lambda gi, ei, fi: (ei, 0, fi)),
                  pl.BlockSpec((1, d, tf), lambda gi, ei, fi: (ei, 0, fi)),
                  pl.BlockSpec((1, tf, d), lambda gi, ei, fi: (ei, fi, 0)),
                  pl.BlockSpec(memory_space=pl.ANY)],
        out_specs=pl.BlockSpec(memory_space=pl.ANY),
        out_shape=jax.ShapeDtypeStruct((b, t, d), F32),
        scratch_shapes=[pltpu.VMEM((gsz * cap, d // 2), U32), pltpu.VMEM((gsz * cap, d), BF16),
                        pltpu.VMEM((gsz * cap, d), F32), pltpu.VMEM((gsz, t, d), F32),
                        pltpu.SemaphoreType.DMA(()), pltpu.SemaphoreType.DMA(())],
        compiler_params=_params(("arbitrary", "arbitrary", "arbitrary")),
        name="moe_ffn",
    )(idx, gcm, hp, g2, wg, wu, wd, x)


def _final_norm_kernel(x_ref, w_ref, o_ref):
    x = x_ref[0]
    o_ref[0] = x * lax.rsqrt(jnp.mean(x * x, axis=-1, keepdims=True) + EPS) * w_ref[...]


def _final_norm(x, w, tm=1024):
    b, t, d = x.shape
    tm = min(tm, t)
    return pl.pallas_call(
        _final_norm_kernel,
        grid=(b, t // tm),
        in_specs=[pl.BlockSpec((1, tm, d), lambda bi, i: (bi, i, 0)), pl.BlockSpec((1, d), lambda bi, i: (0, 0))],
        out_specs=pl.BlockSpec((1, tm, d), lambda bi, i: (bi, i, 0)),
        out_shape=jax.ShapeDtypeStruct((b, t, d), F32),
        compiler_params=_params(("arbitrary", "arbitrary")),
        name="final_norm",
    )(x, w.reshape(1, d))


def _stack_hi_lo(w):
    hi, lo = _split_bf16(w)
    pad = jnp.zeros((w.shape[0], LANES - 2 * w.shape[1]), BF16)
    return jnp.concatenate([hi, lo, pad], axis=1)


def _moe(x, nw, shift, scale, g2, wr2, rb, wg, wu, wd, gsz, tm):
    t = x.shape[1]
    cap = CAPACITY_FACTOR * t // N_EXPERTS
    hp, aff_t = _router(x, nw, shift, scale, wr2, rb, tm)
    idx, gcm = _route_select(aff_t, cap)
    return _moe_ffn(idx, gcm, hp, g2, x, wg, wu, wd, gsz, tf=512)


def kernel(x, c, ctx, c_ctx, ada_w, ada_b, norm1_w, norm2_w, w_in, mlstm_gate_b, mlstm_norm_w, w_mlstm_out,
           conv_dw_w, conv_dw_b, conv_ln_w, conv_ln_b, w_conv_out, sg_ln_w, sg_ln_b, sg_w, sg_b, w_sg_out, w_o,
           router_w, router_b, expert_w_gate, expert_w_up, expert_w_down, final_norm_w):
    depth = ada_w.shape[0]
    b, t, d = x.shape
    dh = d // N_HEADS
    n_state = 3 * d + N_GATE_COLS
    cond = jnp.concatenate([c, c_ctx[None, :], jnp.zeros((16 - b - 1, d), F32)], axis=0)

    for layer in range(depth):
        need_ctx = layer < depth - 1
        mod = _modulation(cond, ada_w[layer], ada_b[layer])
        lat = [m[:, None, :] for m in jnp.split(mod[:b], 6, axis=-1)]
        cx = [jnp.broadcast_to(m[None], (b, 1, d)) for m in jnp.split(mod[b:b + 1], 6, axis=-1)]

        wl = w_in[layer]
        w_qkv = wl[:, :3 * d].astype(BF16)
        w_main = jnp.concatenate([wl[:, n_state:].astype(BF16), w_qkv], axis=1)
        w_gate2 = _stack_hi_lo(wl[:, 3 * d:n_state])
        z_lat, g_lat, gt_lat = _in_proj(x, norm1_w[layer], lat[0], lat[1], w_main, w_gate2, mlstm_gate_b[layer],
                                        tm=1024, tn=1024)
        z_ctx, g_ctx, gt_ctx = _in_proj(ctx, norm1_w[layer], cx[0], cx[1], w_main if need_ctx else w_qkv, w_gate2,
                                        mlstm_gate_b[layer], tm=256, tn=1024)

        hcf, hcb, state = _mlstm(z_ctx, 2 if need_ctx else 0, g_ctx, gt_ctx, None, dh)
        hlf, hlb, _ = _mlstm(z_lat, 2, g_lat, gt_lat, state, dh)

        bf = lambda a: a[layer].astype(BF16)
        p = dict(mlstm_norm_w=mlstm_norm_w[layer], w_mlstm_out=bf(w_mlstm_out), conv_dw_w=conv_dw_w[layer],
                 conv_dw_b=conv_dw_b[layer], conv_ln_w=conv_ln_w[layer], conv_ln_b=conv_ln_b[layer],
                 w_conv_out=bf(w_conv_out), sg_ln_w=sg_ln_w[layer], sg_ln_b=sg_ln_b[layer], sg_w=bf(sg_w),
                 sg_b=sg_b[layer], w_sg_out=bf(w_sg_out), w_o=bf(w_o))
        x = _mixer_out(z_lat, 0, hlf, hlb, x, lat[2], p, row_len=GRID_W, tm=256)
        if need_ctx:
            ctx = _mixer_out(z_ctx, 0, hcf, hcb, ctx, cx[2], p, row_len=ctx.shape[1], tm=256)

        wr2 = _stack_hi_lo(router_w[layer])
        wg, wu, wd = bf(expert_w_gate), bf(expert_w_up), bf(expert_w_down)
        x = _moe(x, norm2_w[layer], lat[3], lat[4], lat[5], wr2, router_b[layer], wg, wu, wd, gsz=1, tm=1024)
        if need_ctx:
            ctx = _moe(ctx, norm2_w[layer], cx[3], cx[4], cx[5], wr2, router_b[layer], wg, wu, wd, gsz=b, tm=256)

    return _final_norm(x, final_norm_w)
```

```python
import functools

import jax
import jax.numpy as jnp
from jax import lax
from jax.experimental import pallas as pl
from jax.experimental.pallas import tpu as pltpu

F32 = jnp.float32
BF16 = jnp.bfloat16
I32 = jnp.int32
U32 = jnp.uint32

EPS = 1e-6
LANES = 128
VMEM_LIMIT = 56 * 1024 * 1024

N_HEADS = 4
MLSTM_CHUNK = 128
N_DIRS = 2
N_GATE_COLS = N_DIRS * 2 * N_HEADS
DW_CONV_SIZE = 31
CONV_PAD = 16
SG_GROUPS = 4
SG_CHUNK = 128
N_EXPERTS = 16
CAPACITY_FACTOR = 2
GRID_W = 64


def _params(sem, vmem=VMEM_LIMIT):
    return pltpu.CompilerParams(dimension_semantics=sem, vmem_limit_bytes=vmem)


def _sigmoid(v):
    return jax.nn.sigmoid(v)


def _split_bf16(a):
    hi = a.astype(BF16)
    lo = (a - hi.astype(F32)).astype(BF16)
    return hi, lo


def _split3_bf16(a):
    hi = a.astype(BF16)
    r1 = a - hi.astype(F32)
    mid = r1.astype(BF16)
    lo = (r1 - mid.astype(F32)).astype(BF16)
    return hi, mid, lo


def _norm_mod(x, nw, shift, scale):
    ms = jnp.mean(x * x, axis=-1, keepdims=True)
    y = x * lax.rsqrt(ms + EPS) * nw
    return y * (1.0 + scale) + shift


def _modulation_kernel(a_ref, w_ref, b_ref, o_ref):
    a = a_ref[...]
    a = a * _sigmoid(a)
    a_hi, a_lo = _split_bf16(a)
    w_hi, w_lo = _split_bf16(w_ref[...])
    acc = jnp.dot(a_hi, w_hi, preferred_element_type=F32)
    acc += jnp.dot(a_hi, w_lo, preferred_element_type=F32)
    acc += jnp.dot(a_lo, w_hi, preferred_element_type=F32)
    o_ref[...] = acc + b_ref[...]


def _modulation(cond, w, b, layer, tn=512):
    m, d = cond.shape
    n = w.shape[2]
    return pl.pallas_call(
        _modulation_kernel,
        grid=(n // tn,),
        in_specs=[pl.BlockSpec((m, d), lambda j: (0, 0)),
                  pl.BlockSpec((None, d, tn), lambda j: (layer, 0, j)),
                  pl.BlockSpec((None, 1, tn), lambda j: (layer, 0, j))],
        out_specs=pl.BlockSpec((m, tn), lambda j: (0, j)),
        out_shape=jax.ShapeDtypeStruct((m, n), F32),
        compiler_params=_params(("arbitrary",)),
        name="modulation",
    )(cond, w, b.reshape(b.shape[0], 1, n))


def _in_proj_kernel(x_ref, nw_ref, sh_ref, sc_ref, w_ref, wg_ref, gb_ref, z_ref, gt_ref, h_scr):
    j = pl.program_id(2)

    @pl.when(j == 0)
    def _():
        h = _norm_mod(x_ref[0], nw_ref[...], sh_ref[0], sc_ref[0])
        h_hi, h_lo = _split_bf16(h)
        h_scr[...] = h_hi
        wg = wg_ref[...]
        raw = jnp.dot(h_hi, wg, preferred_element_type=F32) + jnp.dot(h_lo, wg, preferred_element_type=F32)
        raw = raw[:, :N_GATE_COLS] + raw[:, N_GATE_COLS:2 * N_GATE_COLS] + gb_ref[...]
        col = lax.broadcasted_iota(I32, raw.shape, 1)
        is_forget = ((col >> 2) & 1) == 1
        logsig = jnp.minimum(raw, 0.0) - jnp.log(1.0 + jnp.exp(-jnp.abs(raw)))
        g = jnp.where(is_forget, logsig, raw)
        gpad = jnp.concatenate([g, jnp.zeros((g.shape[0], LANES - N_GATE_COLS), F32)], axis=1)
        gt_ref[0] = gpad.T[:N_GATE_COLS, :]

    z_ref[0] = jnp.dot(h_scr[...], w_ref[...], preferred_element_type=F32).astype(BF16)


def _in_proj(x, nw, shift, scale, w_main, w_gate2, gate_b, tm, tn):
    b, t, d = x.shape
    n = w_main.shape[1]
    tm = min(tm, t)
    return pl.pallas_call(
        _in_proj_kernel,
        grid=(b, t // tm, n // tn),
        in_specs=[pl.BlockSpec((1, tm, d), lambda bi, i, j: (bi, i, 0)),
                  pl.BlockSpec((1, d), lambda bi, i, j: (0, 0)),
                  pl.BlockSpec((1, 1, d), lambda bi, i, j: (bi, 0, 0)),
                  pl.BlockSpec((1, 1, d), lambda bi, i, j: (bi, 0, 0)),
                  pl.BlockSpec((d, tn), lambda bi, i, j: (0, j)),
                  pl.BlockSpec((d, LANES), lambda bi, i, j: (0, 0)),
                  pl.BlockSpec((1, N_GATE_COLS), lambda bi, i, j: (0, 0))],
        out_specs=[pl.BlockSpec((1, tm, tn), lambda bi, i, j: (bi, i, j)),
                   pl.BlockSpec((1, N_GATE_COLS, tm), lambda bi, i, j: (bi, 0, i))],
        out_shape=[jax.ShapeDtypeStruct((b, t, n), BF16),
                   jax.ShapeDtypeStruct((b, N_GATE_COLS, t), F32)],
        scratch_shapes=[pltpu.VMEM((tm, d), BF16)],
        compiler_params=_params(("arbitrary", "arbitrary", "arbitrary")),
        name="in_proj",
    )(x, nw.reshape(1, d), shift, scale, w_main, w_gate2, gate_b.reshape(1, N_GATE_COLS))


def _mlstm_kernel(*refs, dh, has_init, nc):
    if has_init:
        (zf_ref, zb_ref, gtf_ref, gtb_ref, c0_ref, n0_ref, m0_ref,
         hf_ref, hb_ref, cN_ref, nN_ref, mN_ref, c_scr, n_scr, m_scr) = refs
    else:
        zf_ref, zb_ref, gtf_ref, gtb_ref, hf_ref, hb_ref, cN_ref, nN_ref, mN_ref, c_scr, n_scr, m_scr = refs
    c = pl.program_id(1)
    L = MLSTM_CHUNK
    width = N_HEADS * dh
    scans = [(d, hh) for d in range(N_DIRS) for hh in range(N_HEADS)]
    n_s = len(scans)

    @pl.when(c == 0)
    def _():
        if has_init:
            c_scr[...] = c0_ref[0]
            n_scr[...] = n0_ref[0]
            m_scr[...] = m0_ref[0]
        else:
            c_scr[...] = jnp.zeros_like(c_scr)
            n_scr[...] = jnp.zeros_like(n_scr)
            m_scr[...] = jnp.zeros_like(m_scr)

    row = lax.broadcasted_iota(I32, (L, L), 0)
    colm = lax.broadcasted_iota(I32, (L, L), 1)
    eye = row == colm
    eye_bf = eye.astype(BF16)
    eye3 = jnp.concatenate([eye_bf, eye_bf, eye_bf], axis=1)
    keep_t = (row <= colm, row >= colm)
    tri = (keep_t[0].astype(BF16), keep_t[1].astype(BF16))

    gts = [(gtf_ref, gtb_ref)[d][0] for d in range(N_DIRS)]
    cums = [sum(jnp.dot(part, tri[d], preferred_element_type=F32) for part in _split3_bf16(gts[d]))
            for d in range(N_DIRS)]

    q, k, v, m_prev, i_row, b_row, b_last = [], [], [], [], [], [], []
    for si, (d, hh) in enumerate(scans):
        z_ref = (zf_ref, zb_ref)[d]
        q.append(z_ref[0, :, hh * dh:(hh + 1) * dh])
        k.append(z_ref[0, :, width + hh * dh:width + (hh + 1) * dh] * jnp.asarray(dh ** -0.5, BF16))
        v.append(z_ref[0, :, 2 * width + hh * dh:2 * width + (hh + 1) * dh])
        m_prev.append(m_scr[si][:, 0:1])
        ci = d * 2 * N_HEADS + hh
        cf = ci + N_HEADS
        i_row.append(gts[d][ci:ci + 1, :])
        b_row.append(cums[d][cf:cf + 1, :])
        b_last.append(b_row[si][:, L - 1:L] if d == 0 else b_row[si][:, 0:1])

    nt = (((1,), (1,)), ((), ()))
    s_t = [lax.dot_general(k[i], q[i], nt, preferred_element_type=F32) for i in range(n_s)]
    qc = [jnp.dot(q[i], c_scr[i].astype(BF16), preferred_element_type=F32).astype(BF16) for i in range(n_s)]
    qn = [lax.dot_general(n_scr[i].astype(BF16), q[i], nt, preferred_element_type=F32)[0:1, :] for i in range(n_s)]
    cb = []
    for i in range(n_s):
        parts = [jnp.broadcast_to(p, (LANES, L)) for p in _split3_bf16(b_row[i] - i_row[i])]
        cb.append(lax.dot_general(eye3, jnp.concatenate(parts, axis=1), nt, preferred_element_type=F32))

    for i in range(n_s):
        g_row = b_last[i] - b_row[i] + i_row[i]
        mn = jnp.maximum(b_last[i] + m_prev[i], jnp.max(g_row, axis=-1, keepdims=True))
        decay = jnp.exp(b_last[i] + m_prev[i] - mn)
        w_row = jnp.exp(g_row - mn)
        ktw = (k[i].T.astype(F32) * w_row).astype(BF16)
        kv = jnp.dot(ktw, v[i], preferred_element_type=F32)
        wk = jnp.dot(jnp.broadcast_to(w_row, (8, L)).astype(BF16), k[i], preferred_element_type=F32)
        c_new = decay * c_scr[i] + kv
        n_new = decay * n_scr[i] + wk
        c_scr[i] = c_new
        n_scr[i] = n_new
        m_scr[i] = jnp.broadcast_to(mn, (1, LANES))

    for i, (d, hh) in enumerate(scans):
        h_ref = (hf_ref, hb_ref)[d]
        log_d = jnp.where(keep_t[d], b_row[i] - cb[i], -jnp.inf)
        m_inter = b_row[i] + m_prev[i]
        m_t = jnp.maximum(m_inter, jnp.max(log_d, axis=0, keepdims=True))
        p_t = s_t[i] * jnp.exp(log_d - m_t)
        inter = jnp.exp(m_inter - m_t)
        den = jnp.sum(p_t, axis=0, keepdims=True) + inter * qn[i]
        scale = 1.0 / jnp.maximum(jnp.abs(den), jnp.exp(-m_t))
        lhs_t = jnp.concatenate([(p_t * scale).astype(BF16), jnp.where(eye, scale * inter, 0.0).astype(BF16)], axis=0)
        rhs = jnp.concatenate([v[i], qc[i]], axis=0)
        h = lax.dot_general(lhs_t, rhs, (((0,), (0,)), ((), ())), preferred_element_type=F32)
        h_ref[0, :, hh * dh:(hh + 1) * dh] = h.astype(h_ref.dtype)

    @pl.when(c == nc - 1)
    def _():
        cN_ref[0] = c_scr[...]
        nN_ref[0] = n_scr[...]
        mN_ref[0] = m_scr[...]


def _mlstm(z, zblk, gates_t, init, dh):
    b, t, _ = z.shape
    L = MLSTM_CHUNK
    nc = t // L
    width = N_HEADS * dh
    ns = N_DIRS * N_HEADS
    fwd = lambda bi, c: (bi, c, 0)
    bwd = lambda bi, c: (bi, nc - 1 - c, 0)
    st4 = lambda bi, c: (bi, 0, 0, 0)
    in_specs = [pl.BlockSpec((1, L, 3 * width), lambda bi, c: (bi, c, zblk)),
                pl.BlockSpec((1, L, 3 * width), lambda bi, c: (bi, nc - 1 - c, zblk)),
                pl.BlockSpec((1, N_GATE_COLS, L), lambda bi, c: (bi, 0, c)),
                pl.BlockSpec((1, N_GATE_COLS, L), lambda bi, c: (bi, 0, nc - 1 - c))]
    args = [z, z, gates_t, gates_t]
    state_specs = [pl.BlockSpec((1, ns, dh, dh), st4), pl.BlockSpec((1, ns, 8, dh), st4),
                   pl.BlockSpec((1, ns, 1, LANES), st4)]
    state_shapes = [jax.ShapeDtypeStruct((b, ns, dh, dh), F32), jax.ShapeDtypeStruct((b, ns, 8, dh), F32),
                    jax.ShapeDtypeStruct((b, ns, 1, LANES), F32)]
    if init is not None:
        in_specs += state_specs
        args += list(init)
    outs = pl.pallas_call(
        functools.partial(_mlstm_kernel, dh=dh, has_init=init is not None, nc=nc),
        grid=(b, nc),
        in_specs=in_specs,
        out_specs=[pl.BlockSpec((1, L, width), fwd), pl.BlockSpec((1, L, width), bwd)] + state_specs,
        out_shape=[jax.ShapeDtypeStruct((b, t, width), BF16), jax.ShapeDtypeStruct((b, t, width), BF16)] + state_shapes,
        scratch_shapes=[pltpu.VMEM((ns, dh, dh), F32), pltpu.VMEM((ns, 8, dh), F32), pltpu.VMEM((ns, 1, LANES), F32)],
        compiler_params=_params(("arbitrary", "arbitrary")),
        name="mlstm",
    )(*args)
    return outs[0], outs[1], tuple(outs[2:])


def _layer_norm(v, w, b):
    mu = jnp.mean(v, axis=-1, keepdims=True)
    vc = v - mu
    var = jnp.mean(vc * vc, axis=-1, keepdims=True)
    return vc * lax.rsqrt(var + EPS) * w + b


def _gelu_tanh(v):
    return 0.5 * v * (1.0 + jnp.tanh(0.7978845608028654 * (v + 0.044715 * (v * v * v))))


def _mixer_kernel(z_ref, hf_ref, hb_ref, x_ref, g1_ref, mnw_ref, wm_ref, dww_ref, dwb_ref, clw_ref, clb_ref,
                  wc_ref, slw_ref, slb_ref, sgw_ref, sgb_ref, ws_ref, wo_ref, o_ref, pad_scr, conv_scr,
                  *, d, row_len):
    tm = z_ref.shape[1]
    dh = d // N_HEADS
    cc = d // 2
    off_conv, off_sg, off_merge = d, 2 * d, 3 * d

    y_m = jnp.zeros((tm, d), F32)
    for hh in range(N_HEADS):
        sl = slice(hh * dh, (hh + 1) * dh)
        hm = hf_ref[0, :, sl].astype(F32) + hb_ref[0, :, sl].astype(F32)
        yn = hm * lax.rsqrt(jnp.mean(hm * hm, axis=-1, keepdims=True) + EPS) * mnw_ref[:, sl]
        og = _sigmoid(z_ref[0, :, sl].astype(F32))
        y_m += jnp.dot((yn * og).astype(BF16), wm_ref[sl, :], preferred_element_type=F32)

    u = z_ref[0, :, off_conv:off_conv + cc].astype(F32) * _sigmoid(z_ref[0, :, off_conv + cc:off_conv + 2 * cc].astype(F32))
    n_rows = tm // row_len
    zpad = jnp.zeros((CONV_PAD, cc), F32)
    for r in range(n_rows):
        pad_scr[r, 0:CONV_PAD, :] = zpad
        pad_scr[r, CONV_PAD:CONV_PAD + row_len, :] = u[r * row_len:(r + 1) * row_len, :]
        pad_scr[r, CONV_PAD + row_len:2 * CONV_PAD + row_len, :] = zpad
    base = CONV_PAD - DW_CONV_SIZE // 2
    for cb in range(cc // LANES):
        ls = slice(cb * LANES, (cb + 1) * LANES)
        taps = [dww_ref[kk:kk + 1, ls] for kk in range(DW_CONV_SIZE)]
        for r in range(n_rows):
            acc = jnp.zeros((row_len, LANES), F32)
            for kk in range(DW_CONV_SIZE):
                acc += pad_scr[r, base + kk:base + kk + row_len, ls] * taps[kk]
            conv_scr[r * row_len:(r + 1) * row_len, ls] = acc
    cv = _layer_norm(conv_scr[...] + dwb_ref[...], clw_ref[...], clb_ref[...])
    cv = cv * _sigmoid(cv)
    y_c = jnp.dot(cv.astype(BF16), wc_ref[...], preferred_element_type=F32)

    su = _gelu_tanh(z_ref[0, :, off_sg:off_sg + cc].astype(F32))
    sv = _gelu_tanh(z_ref[0, :, off_sg + cc:off_sg + 2 * cc].astype(F32))
    sv = _layer_norm(sv, slw_ref[...], slb_ref[...]).astype(BF16)
    gd = cc // SG_GROUPS
    for ch in range(tm // SG_CHUNK):
        rs = slice(ch * SG_CHUNK, (ch + 1) * SG_CHUNK)
        parts = []
        for gi in range(SG_GROUPS):
            mixed = jnp.dot(sgw_ref[gi], sv[rs, gi * gd:(gi + 1) * gd], preferred_element_type=F32)
            parts.append(mixed + sgb_ref[:, gi:gi + 1])
        gated = su[rs, :] * jnp.concatenate(parts, axis=1)
        conv_scr[rs, :] = gated
    y_s = jnp.dot(conv_scr[...].astype(BF16), ws_ref[...], preferred_element_type=F32)

    gm0 = _sigmoid(z_ref[0, :, off_merge:off_merge + d].astype(F32))
    gm1 = _sigmoid(z_ref[0, :, off_merge + d:off_merge + 2 * d].astype(F32))
    gm2 = _sigmoid(z_ref[0, :, off_merge + 2 * d:off_merge + 3 * d].astype(F32))
    merged = gm0 * y_m + gm1 * y_c + gm2 * y_s
    y = jnp.dot(merged.astype(BF16), wo_ref[...], preferred_element_type=F32)
    o_ref[0] = x_ref[0] + g1_ref[0] * y


def _mixer_out(z, zblk, hf, hb, x, g1, p, row_len, tm):
    b, t, d = x.shape
    cc = d // 2
    tm = min(tm, t)
    nz = 6 * d
    full = lambda a: pl.BlockSpec(a.shape, lambda bi, i: (0,) * a.ndim)
    row = lambda a: a.reshape(1, -1)
    consts = [row(p["mlstm_norm_w"]), p["w_mlstm_out"], p["conv_dw_w"], row(p["conv_dw_b"]), row(p["conv_ln_w"]),
              row(p["conv_ln_b"]), p["w_conv_out"], row(p["sg_ln_w"]), row(p["sg_ln_b"]), p["sg_w"], p["sg_b"].T,
              p["w_sg_out"], p["w_o"]]
    tok = lambda w: pl.BlockSpec((1, tm, w), lambda bi, i: (bi, i, 0))
    return pl.pallas_call(
        functools.partial(_mixer_kernel, d=d, row_len=row_len),
        grid=(b, t // tm),
        in_specs=[pl.BlockSpec((1, tm, nz), lambda bi, i: (bi, i, zblk)), tok(d), tok(d), tok(d),
                  pl.BlockSpec((1, 1, d), lambda bi, i: (bi, 0, 0))] + [full(a) for a in consts],
        out_specs=tok(d),
        out_shape=jax.ShapeDtypeStruct((b, t, d), F32),
        scratch_shapes=[pltpu.VMEM((tm // row_len, row_len + 2 * CONV_PAD, cc), F32), pltpu.VMEM((tm, cc), F32)],
        compiler_params=_params(("arbitrary", "arbitrary")),
        name="mixer_out",
    )(z, hf, hb, x, g1, *consts)


def _router_kernel(x_ref, nw_ref, sh_ref, sc_ref, wr_ref, rb_ref, hp_ref, at_ref):
    h = _norm_mod(x_ref[0], nw_ref[...], sh_ref[0], sc_ref[0])
    h_hi, h_lo = _split_bf16(h)
    wr = wr_ref[...]
    raw = jnp.dot(h_hi, wr, preferred_element_type=F32) + jnp.dot(h_lo, wr, preferred_element_type=F32)
    logits = raw[:, :N_EXPERTS] + raw[:, N_EXPERTS:2 * N_EXPERTS] + rb_ref[...]
    mx = jnp.max(logits, axis=-1, keepdims=True)
    ex = jnp.exp(logits - mx)
    aff = ex / jnp.sum(ex, axis=-1, keepdims=True)
    apad = jnp.concatenate([aff, jnp.zeros((aff.shape[0], LANES - N_EXPERTS), F32)], axis=1)
    at_ref[0] = apad.T[:N_EXPERTS, :]
    half = h.shape[1] // 2
    bits = lax.bitcast_convert_type(h_hi.astype(F32), U32)
    hp_ref[0] = (bits[:, :half] >> 16) | (bits[:, half:] & jnp.uint32(0xFFFF0000))


def _router(x, nw, shift, scale, wr2, rb, tm):
    b, t, d = x.shape
    tm = min(tm, t)
    return pl.pallas_call(
        _router_kernel,
        grid=(b, t // tm),
        in_specs=[pl.BlockSpec((1, tm, d), lambda bi, i: (bi, i, 0)),
                  pl.BlockSpec((1, d), lambda bi, i: (0, 0)),
                  pl.BlockSpec((1, 1, d), lambda bi, i: (bi, 0, 0)),
                  pl.BlockSpec((1, 1, d), lambda bi, i: (bi, 0, 0)),
                  pl.BlockSpec((d, LANES), lambda bi, i: (0, 0)),
                  pl.BlockSpec((1, N_EXPERTS), lambda bi, i: (0, 0))],
        out_specs=[pl.BlockSpec((1, tm, d // 2), lambda bi, i: (bi, i, 0)),
                   pl.BlockSpec((1, N_EXPERTS, tm), lambda bi, i: (bi, 0, i))],
        out_shape=[jax.ShapeDtypeStruct((b, t, d // 2), U32), jax.ShapeDtypeStruct((b, N_EXPERTS, t), F32)],
        compiler_params=_params(("arbitrary", "arbitrary")),
        name="router",
    )(x, nw.reshape(1, d), shift, scale, wr2, rb.reshape(1, N_EXPERTS))


def _prefix_count(mask, tri):
    e, t = mask.shape
    nb = t // LANES
    stacked = jnp.concatenate([mask[:, c * LANES:(c + 1) * LANES] for c in range(nb)], axis=0).astype(BF16)
    local = jnp.dot(stacked, tri, preferred_element_type=F32)
    out = []
    off = jnp.zeros((e, 1), F32)
    for c in range(nb):
        blk = local[c * e:(c + 1) * e, :]
        out.append(blk + off)
        off = off + blk[:, LANES - 1:LANES]
    return jnp.concatenate(out, axis=1)


def _select_kernel(at_ref, idx_ref, gcm_ref, cnt_scr, asel_scr, icol_scr, *, cap, jt):
    a = at_ref[0]
    e, t = a.shape
    bits = lax.bitcast_convert_type(a, I32)

    def search(i, v):
        cand = v | jnp.left_shift(jnp.int32(1), 30 - i)
        cnt = jnp.sum((bits >= cand).astype(I32), axis=1, keepdims=True)
        return jnp.where(cnt >= cap, cand, v)

    thr = lax.fori_loop(0, 31, search, jnp.zeros((e, 1), I32))
    r = lax.broadcasted_iota(I32, (LANES, LANES), 0)
    s = lax.broadcasted_iota(I32, (LANES, LANES), 1)
    tri = (r <= s).astype(BF16)
    gt = bits > thr
    eq = bits == thr
    need = (cap - jnp.sum(gt.astype(I32), axis=1, keepdims=True)).astype(F32)
    sel = gt | (eq & (_prefix_count(eq.astype(F32), tri) <= need))
    cnt = _prefix_count(sel.astype(F32), tri)
    asel = jnp.where(sel, a, 0.0)
    for ei in range(e):
        cnt_scr[ei] = cnt[ei:ei + 1, :]
        asel_scr[ei] = asel[ei:ei + 1, :]
    icol_scr[...] = jnp.zeros_like(icol_scr)
    gcm_ref[0] = jnp.zeros_like(gcm_ref[0])
    lane = lax.broadcasted_iota(I32, (jt, LANES), 1)

    def per_expert(ei, carry):
        for jb in range(cap // jt):
            jcol = (lax.broadcasted_iota(I32, (jt, 1), 0) + jb * jt).astype(F32)
            acc_n = jnp.zeros((jt, LANES), F32)
            acc_g = jnp.zeros((jt, LANES), F32)
            for tb in range(t // LANES):
                cn = cnt_scr[ei, :, tb * LANES:(tb + 1) * LANES]
                av = asel_scr[ei, :, tb * LANES:(tb + 1) * LANES]
                acc_n += jnp.where(cn <= jcol, 1.0, 0.0)
                acc_g = jnp.maximum(acc_g, jnp.where(cn == jcol + 1.0, av, 0.0))
            tok = jnp.sum(acc_n, axis=1, keepdims=True)
            gv = jnp.max(acc_g, axis=1, keepdims=True)
            rows = slice(jb * jt, (jb + 1) * jt)
            icol_scr[rows, :] = jnp.where(lane == ei, tok, icol_scr[rows, :])
            gcm_ref[0, rows, :] = jnp.where(lane == ei, gv, gcm_ref[0, rows, :])
        return carry

    lax.fori_loop(0, e, per_expert, 0)
    idx_ref[0, :, 0, :] = icol_scr[...].T[:e, :cap].astype(I32)


def _route_select(aff_t, cap):
    b, e, t = aff_t.shape
    jt = min(cap, 64)
    return pl.pallas_call(
        functools.partial(_select_kernel, cap=cap, jt=jt),
        grid=(b,),
        in_specs=[pl.BlockSpec((1, e, t), lambda bi: (bi, 0, 0))],
        out_specs=[pl.BlockSpec((1, e, 1, cap), lambda bi: (bi, 0, 0, 0)),
                   pl.BlockSpec((1, cap, LANES), lambda bi: (bi, 0, 0))],
        out_shape=[jax.ShapeDtypeStruct((b, e, 1, cap), I32), jax.ShapeDtypeStruct((b, cap, LANES), F32)],
        scratch_shapes=[pltpu.VMEM((e, 1, t), F32), pltpu.VMEM((e, 1, t), F32), pltpu.VMEM((max(cap, LANES), LANES), F32)],
        compiler_params=_params(("arbitrary",)),
        name="route_select",
    )(aff_t)


def _moe_kernel(idxp_ref, idxn_ref, gcm_ref, hp_ref, g2_ref, wg_ref, wu_ref, wd_ref, x_hbm, o_hbm,
                xe32_scr, xe_scr, y_scr, yg_scr, acc_scr, sem_in, sem_out, *, gsz, cap, t, n_groups, n_f):
    grp = pl.program_id(0)
    e = pl.program_id(1)
    f = pl.program_id(2)
    n_e = pl.num_programs(1)
    half = hp_ref.shape[2]
    q_rows = cap // n_f

    def in_copy(gi):
        return pltpu.make_async_copy(x_hbm.at[pl.ds(gi * gsz, gsz)], acc_scr.at[:, pl.ds(0, t)], sem_in)

    def out_copy(gi):
        return pltpu.make_async_copy(acc_scr.at[:, pl.ds(0, t)], o_hbm.at[pl.ds(gi * gsz, gsz)], sem_out)

    def unpack():
        p = xe32_scr[...]
        xe_scr[:, :half] = lax.bitcast_convert_type(p << 16, F32).astype(BF16)
        xe_scr[:, half:] = lax.bitcast_convert_type(p & jnp.uint32(0xFFFF0000), F32).astype(BF16)

    def gated(expert):
        lane = lax.broadcasted_iota(I32, (cap, LANES), 1)
        for si in range(gsz):
            rs = slice(si * cap, (si + 1) * cap)
            gate = jnp.sum(jnp.where(lane == expert, gcm_ref[si], 0.0), axis=1, keepdims=True)
            yg_scr[rs, :] = y_scr[rs, :] * gate * g2_ref[si]

    @pl.when((e == 0) & (f == 0))
    def _():
        @pl.when(grp > 0)
        def _():
            out_copy(grp - 1).wait()
        in_copy(grp).start()
        for si in range(gsz):
            def gather(j, carry):
                tkn = idxp_ref[si, 0, 0, j]
                xe32_scr[pl.ds(si * cap + j, 1), :] = hp_ref[si, pl.ds(tkn, 1), :]
                return carry
            lax.fori_loop(0, cap, gather, 0, unroll=8)
        unpack()
        yg_scr[...] = jnp.zeros_like(yg_scr)
        acc_scr[:, t:t + 8, :] = jnp.zeros((gsz, 8, acc_scr.shape[2]), F32)

    @pl.when((e > 0) & (f == 0))
    def _():
        unpack()
        gated(e - 1)

    @pl.when((e == 1) & (f == 0))
    def _():
        in_copy(grp).wait()

    xe = xe_scr[...]
    a = jnp.dot(xe, wg_ref[0], preferred_element_type=F32)
    u = jnp.dot(xe, wu_ref[0], preferred_element_type=F32)
    hid = (a * _sigmoid(a) * u).astype(BF16)
    part = jnp.dot(hid, wd_ref[0], preferred_element_type=F32)
    spare = jnp.where(e == 0, 1, 0)
    for si in range(gsz):
        for jj in range(q_rows):
            j = f * q_rows + jj
            tkn = idxn_ref[si, 0, 0, j]
            xe32_scr[pl.ds(si * cap + j, 1), :] = hp_ref[si, pl.ds(tkn, 1), :]
        for jj in range(q_rows):
            j = f * q_rows + jj
            tkn = jnp.where(spare == 1, t, idxp_ref[si, 0, 0, j])
            acc_scr[si, pl.ds(tkn, 1), :] += yg_scr[pl.ds(si * cap + j, 1), :]

    @pl.when(f == 0)
    def _():
        y_scr[...] = part

    @pl.when(f > 0)
    def _():
        y_scr[...] += part

    @pl.when((e == n_e - 1) & (f == n_f - 1))
    def _():
        gated(e)
        for si in range(gsz):
            def scatter(j, carry):
                tkn = idxn_ref[si, 0, 0, j]
                acc_scr[si, pl.ds(tkn, 1), :] += yg_scr[pl.ds(si * cap + j, 1), :]
                return carry
            lax.fori_loop(0, cap, scatter, 0, unroll=8)
        out_copy(grp).start()

        @pl.when(grp == n_groups - 1)
        def _():
            out_copy(grp).wait()


def _moe_ffn(idx, gcm, hp, g2, x, wg, wu, wd, layer, gsz, tf):
    b, t, d = x.shape
    e, cap = idx.shape[1], idx.shape[3]
    ff = wg.shape[3]
    n_groups, n_f = b // gsz, ff // tf
    smem_idx = lambda fn: pl.BlockSpec((gsz, 1, 1, cap), fn, memory_space=pltpu.SMEM)
    return pl.pallas_call(
        functools.partial(_moe_kernel, gsz=gsz, cap=cap, t=t, n_groups=n_groups, n_f=n_f),
        grid=(n_groups, e, n_f),
        in_specs=[smem_idx(lambda gi, ei, fi: (gi, jnp.maximum(ei - 1, 0), 0, 0)),
                  smem_idx(lambda gi, ei, fi: (gi, jnp.minimum(ei + 1, e - 1), 0, 0)),
                  pl.BlockSpec((gsz, cap, LANES), lambda gi, ei, fi: (gi, 0, 0)),
                  pl.BlockSpec((gsz, t, d // 2), lambda gi, ei, fi: (gi, 0, 0)),
                  pl.BlockSpec((gsz, 1, d), lambda gi, ei, fi: (gi, 0, 0)),
                  pl.BlockSpec((None, 1, d, tf), lambda gi, ei, fi: (layer, ei, 0, fi)),
                  pl.BlockSpec((None, 1, d, tf), lambda gi, ei, fi: (layer, ei, 0, fi)),
                  pl.BlockSpec((None, 1, tf, d), lambda gi, ei, fi: (layer, ei, fi, 0)),
                  pl.BlockSpec(memory_space=pl.ANY)],
        out_specs=pl.BlockSpec(memory_space=pl.ANY),
        out_shape=jax.ShapeDtypeStruct((b, t, d), F32),
        scratch_shapes=[pltpu.VMEM((gsz * cap, d // 2), U32), pltpu.VMEM((gsz * cap, d), BF16),
                        pltpu.VMEM((gsz * cap, d), F32), pltpu.VMEM((gsz * cap, d), F32),
                        pltpu.VMEM((gsz, t + 8, d), F32),
                        pltpu.SemaphoreType.DMA(()), pltpu.SemaphoreType.DMA(())],
        compiler_params=_params(("arbitrary", "arbitrary", "arbitrary")),
        name="moe_ffn",
    )(idx, idx, gcm, hp, g2, wg, wu, wd, x)


def _final_norm_kernel(x_ref, w_ref, o_ref):
    x = x_ref[0]
    o_ref[0] = x * lax.rsqrt(jnp.mean(x * x, axis=-1, keepdims=True) + EPS) * w_ref[...]


def _final_norm(x, w, tm=1024):
    b, t, d = x.shape
    tm = min(tm, t)
    return pl.pallas_call(
        _final_norm_kernel,
        grid=(b, t // tm),
        in_specs=[pl.BlockSpec((1, tm, d), lambda bi, i: (bi, i, 0)), pl.BlockSpec((1, d), lambda bi, i: (0, 0))],
        out_specs=pl.BlockSpec((1, tm, d), lambda bi, i: (bi, i, 0)),
        out_shape=jax.ShapeDtypeStruct((b, t, d), F32),
        compiler_params=_params(("arbitrary", "arbitrary")),
        name="final_norm",
    )(x, w.reshape(1, d))


def _stack_hi_lo(w):
    hi, lo = _split_bf16(w)
    pad = jnp.zeros((w.shape[0], LANES - 2 * w.shape[1]), BF16)
    return jnp.concatenate([hi, lo, pad], axis=1)


def _moe(x, nw, shift, scale, g2, wr2, rb, wg, wu, wd, layer, gsz, tm):
    t = x.shape[1]
    cap = CAPACITY_FACTOR * t // N_EXPERTS
    hp, aff_t = _router(x, nw, shift, scale, wr2, rb, tm)
    idx, gcm = _route_select(aff_t, cap)
    return _moe_ffn(idx, gcm, hp, g2, x, wg, wu, wd, layer, gsz, tf=512)


def kernel(x, c, ctx, c_ctx, ada_w, ada_b, norm1_w, norm2_w, w_in, mlstm_gate_b, mlstm_norm_w, w_mlstm_out,
           conv_dw_w, conv_dw_b, conv_ln_w, conv_ln_b, w_conv_out, sg_ln_w, sg_ln_b, sg_w, sg_b, w_sg_out, w_o,
           router_w, router_b, expert_w_gate, expert_w_up, expert_w_down, final_norm_w):
    depth = ada_w.shape[0]
    b, t, d = x.shape
    dh = d // N_HEADS
    n_state = 3 * d + N_GATE_COLS
    cond = jnp.concatenate([c, c_ctx[None, :], jnp.zeros((16 - b - 1, d), F32)], axis=0)
    wg, wu, wd = expert_w_gate.astype(BF16), expert_w_up.astype(BF16), expert_w_down.astype(BF16)

    for layer in range(depth):
        need_ctx = layer < depth - 1
        mod = _modulation(cond, ada_w, ada_b, layer)
        lat = [m[:, None, :] for m in jnp.split(mod[:b], 6, axis=-1)]
        cx = [jnp.broadcast_to(m[None], (b, 1, d)) for m in jnp.split(mod[b:b + 1], 6, axis=-1)]

        wl = w_in[layer]
        w_qkv = wl[:, :3 * d].astype(BF16)
        w_main = jnp.concatenate([wl[:, n_state:].astype(BF16), w_qkv], axis=1)
        w_gate2 = _stack_hi_lo(wl[:, 3 * d:n_state])
        z_lat, gt_lat = _in_proj(x, norm1_w[layer], lat[0], lat[1], w_main, w_gate2, mlstm_gate_b[layer],
                                 tm=1024, tn=1024)
        z_ctx, gt_ctx = _in_proj(ctx, norm1_w[layer], cx[0], cx[1], w_main if need_ctx else w_qkv, w_gate2,
                                 mlstm_gate_b[layer], tm=256, tn=1024)

        hcf, hcb, state = _mlstm(z_ctx, 2 if need_ctx else 0, gt_ctx, None, dh)
        hlf, hlb, _ = _mlstm(z_lat, 2, gt_lat, state, dh)

        bf = lambda a: a[layer].astype(BF16)
        p = dict(mlstm_norm_w=mlstm_norm_w[layer], w_mlstm_out=bf(w_mlstm_out), conv_dw_w=conv_dw_w[layer],
                 conv_dw_b=conv_dw_b[layer], conv_ln_w=conv_ln_w[layer], conv_ln_b=conv_ln_b[layer],
                 w_conv_out=bf(w_conv_out), sg_ln_w=sg_ln_w[layer], sg_ln_b=sg_ln_b[layer], sg_w=bf(sg_w),
                 sg_b=sg_b[layer], w_sg_out=bf(w_sg_out), w_o=bf(w_o))
        x = _mixer_out(z_lat, 0, hlf, hlb, x, lat[2], p, row_len=GRID_W, tm=256)
        if need_ctx:
            ctx = _mixer_out(z_ctx, 0, hcf, hcb, ctx, cx[2], p, row_len=ctx.shape[1], tm=256)

        wr2 = _stack_hi_lo(router_w[layer])
        x = _moe(x, norm2_w[layer], lat[3], lat[4], lat[5], wr2, router_b[layer], wg, wu, wd, layer, gsz=1, tm=1024)
        if need_ctx:
            ctx = _moe(ctx, norm2_w[layer], cx[3], cx[4], cx[5], wr2, router_b[layer], wg, wu, wd, layer, gsz=b, tm=256)

    return _final_norm(x, final_norm_w)
```

```python
import functools

import jax
import jax.numpy as jnp
from jax import lax
from jax.experimental import pallas as pl
from jax.experimental.pallas import tpu as pltpu

F32 = jnp.float32
BF16 = jnp.bfloat16
I32 = jnp.int32
U32 = jnp.uint32

EPS = 1e-6
LANES = 128
SUBLANES = 8
VMEM_LIMIT = 56 * 1024 * 1024

N_HEADS = 4
MLSTM_CHUNK = 128
N_DIRS = 2
N_GATE_COLS = N_DIRS * 2 * N_HEADS
DW_CONV_SIZE = 31
CONV_PAD = 16
SG_GROUPS = 4
SG_CHUNK = 128
N_EXPERTS = 16
CAPACITY_FACTOR = 2
GRID_W = 64


def _params(sem, vmem=VMEM_LIMIT):
    return pltpu.CompilerParams(dimension_semantics=sem, vmem_limit_bytes=vmem)


def _sigmoid(v):
    return 0.5 * jnp.tanh(0.5 * v) + 0.5


def _split_bf16(a):
    hi = a.astype(BF16)
    lo = (a - hi.astype(F32)).astype(BF16)
    return hi, lo


def _split3_bf16(a):
    hi = a.astype(BF16)
    r1 = a - hi.astype(F32)
    mid = r1.astype(BF16)
    lo = (r1 - mid.astype(F32)).astype(BF16)
    return hi, mid, lo


def _norm_mod(x, nw, shift, scale):
    ms = jnp.mean(x * x, axis=-1, keepdims=True)
    y = x * lax.rsqrt(ms + EPS) * nw
    return y * (1.0 + scale) + shift


def _modulation_kernel(a_ref, w_ref, b_ref, o_ref):
    a = a_ref[...]
    a = a * _sigmoid(a)
    a_hi, a_lo = _split_bf16(a)
    w_hi, w_lo = _split_bf16(w_ref[...])
    acc = jnp.dot(a_hi, w_hi, preferred_element_type=F32)
    acc += jnp.dot(a_hi, w_lo, preferred_element_type=F32)
    acc += jnp.dot(a_lo, w_hi, preferred_element_type=F32)
    o_ref[...] = acc + b_ref[...]


def _modulation(cond, w, b, layer, tn=512):
    m, d = cond.shape
    n = w.shape[2]
    return pl.pallas_call(
        _modulation_kernel,
        grid=(n // tn,),
        in_specs=[pl.BlockSpec((m, d), lambda j: (0, 0)),
                  pl.BlockSpec((None, d, tn), lambda j: (layer, 0, j)),
                  pl.BlockSpec((None, 1, tn), lambda j: (layer, 0, j))],
        out_specs=pl.BlockSpec((m, tn), lambda j: (0, j)),
        out_shape=jax.ShapeDtypeStruct((m, n), F32),
        compiler_params=_params(("arbitrary",)),
        name="modulation",
    )(cond, w, b.reshape(b.shape[0], 1, n))


def _in_proj_kernel(x_ref, nw_ref, sh_ref, sc_ref, w_ref, wg_ref, gb_ref, z_ref, gt_ref, h_scr):
    j = pl.program_id(2)

    @pl.when(j == 0)
    def _():
        h = _norm_mod(x_ref[0], nw_ref[...], sh_ref[0], sc_ref[0])
        h_hi, h_lo = _split_bf16(h)
        h_scr[...] = h_hi
        wg = wg_ref[...]
        raw = jnp.dot(h_hi, wg, preferred_element_type=F32) + jnp.dot(h_lo, wg, preferred_element_type=F32)
        raw = raw[:, :N_GATE_COLS] + raw[:, N_GATE_COLS:2 * N_GATE_COLS] + gb_ref[...]
        col = lax.broadcasted_iota(I32, raw.shape, 1)
        is_forget = ((col >> 2) & 1) == 1
        logsig = jnp.minimum(raw, 0.0) - jnp.log(1.0 + jnp.exp(-jnp.abs(raw)))
        g = jnp.where(is_forget, logsig, raw)
        gpad = jnp.concatenate([g, jnp.zeros((g.shape[0], LANES - N_GATE_COLS), F32)], axis=1)
        gt_ref[0] = gpad.T[:N_GATE_COLS, :]

    z_ref[0] = jnp.dot(h_scr[...], w_ref[...], preferred_element_type=F32).astype(BF16)


def _in_proj(x, nw, shift, scale, w_main, w_gate2, gate_b, tm, tn):
    b, t, d = x.shape
    n = w_main.shape[1]
    tm = min(tm, t)
    return pl.pallas_call(
        _in_proj_kernel,
        grid=(b, t // tm, n // tn),
        in_specs=[pl.BlockSpec((1, tm, d), lambda bi, i, j: (bi, i, 0)),
                  pl.BlockSpec((1, d), lambda bi, i, j: (0, 0)),
                  pl.BlockSpec((1, 1, d), lambda bi, i, j: (bi, 0, 0)),
                  pl.BlockSpec((1, 1, d), lambda bi, i, j: (bi, 0, 0)),
                  pl.BlockSpec((d, tn), lambda bi, i, j: (0, j)),
                  pl.BlockSpec((d, LANES), lambda bi, i, j: (0, 0)),
                  pl.BlockSpec((1, N_GATE_COLS), lambda bi, i, j: (0, 0))],
        out_specs=[pl.BlockSpec((1, tm, tn), lambda bi, i, j: (bi, i, j)),
                   pl.BlockSpec((1, N_GATE_COLS, tm), lambda bi, i, j: (bi, 0, i))],
        out_shape=[jax.ShapeDtypeStruct((b, t, n), BF16),
                   jax.ShapeDtypeStruct((b, N_GATE_COLS, t), F32)],
        scratch_shapes=[pltpu.VMEM((tm, d), BF16)],
        compiler_params=_params(("arbitrary", "arbitrary", "arbitrary")),
        name="in_proj",
    )(x, nw.reshape(1, d), shift, scale, w_main, w_gate2, gate_b.reshape(1, N_GATE_COLS))


def _mlstm_kernel(*refs, dh, has_init, nc):
    if has_init:
        (zf_ref, zb_ref, gtf_ref, gtb_ref, c0_ref, n0_ref, m0_ref,
         hf_ref, hb_ref, cN_ref, nN_ref, mN_ref, c_scr, n_scr, m_scr) = refs
    else:
        zf_ref, zb_ref, gtf_ref, gtb_ref, hf_ref, hb_ref, cN_ref, nN_ref, mN_ref, c_scr, n_scr, m_scr = refs
    c = pl.program_id(1)
    L = MLSTM_CHUNK
    width = N_HEADS * dh
    scans = [(d, hh) for d in range(N_DIRS) for hh in range(N_HEADS)]
    n_s = len(scans)

    @pl.when(c == 0)
    def _():
        if has_init:
            c_scr[...] = c0_ref[0]
            n_scr[...] = n0_ref[0]
            m_scr[...] = m0_ref[0]
        else:
            c_scr[...] = jnp.zeros_like(c_scr)
            n_scr[...] = jnp.zeros_like(n_scr)
            m_scr[...] = jnp.zeros_like(m_scr)

    row = lax.broadcasted_iota(I32, (L, L), 0)
    colm = lax.broadcasted_iota(I32, (L, L), 1)
    eye = row == colm
    eye_bf = eye.astype(BF16)
    eye3 = jnp.concatenate([eye_bf, eye_bf, eye_bf], axis=1)
    keep_t = (row <= colm, row >= colm)
    tri = (keep_t[0].astype(BF16), keep_t[1].astype(BF16))

    gts = [(gtf_ref, gtb_ref)[d][0] for d in range(N_DIRS)]
    cums = [sum(jnp.dot(part, tri[d], preferred_element_type=F32) for part in _split3_bf16(gts[d]))
            for d in range(N_DIRS)]

    q, k, v, m_prev, i_row, b_row, b_last = [], [], [], [], [], [], []
    for si, (d, hh) in enumerate(scans):
        z_ref = (zf_ref, zb_ref)[d]
        q.append(z_ref[0, :, hh * dh:(hh + 1) * dh])
        k.append(z_ref[0, :, width + hh * dh:width + (hh + 1) * dh] * jnp.asarray(dh ** -0.5, BF16))
        v.append(z_ref[0, :, 2 * width + hh * dh:2 * width + (hh + 1) * dh])
        m_prev.append(m_scr[si][:, 0:1])
        ci = d * 2 * N_HEADS + hh
        cf = ci + N_HEADS
        i_row.append(gts[d][ci:ci + 1, :])
        b_row.append(cums[d][cf:cf + 1, :])
        b_last.append(b_row[si][:, L - 1:L] if d == 0 else b_row[si][:, 0:1])

    nt = (((1,), (1,)), ((), ()))
    s_t = [lax.dot_general(k[i], q[i], nt, preferred_element_type=F32) for i in range(n_s)]
    qc = [jnp.dot(q[i], c_scr[i].astype(BF16), preferred_element_type=F32).astype(BF16) for i in range(n_s)]
    qn = [lax.dot_general(n_scr[i].astype(BF16), q[i], nt, preferred_element_type=F32)[0:1, :] for i in range(n_s)]
    cb = []
    for i in range(n_s):
        parts = [jnp.broadcast_to(p, (LANES, L)) for p in _split3_bf16(b_row[i] - i_row[i])]
        cb.append(lax.dot_general(eye3, jnp.concatenate(parts, axis=1), nt, preferred_element_type=F32))

    for i in range(n_s):
        g_row = b_last[i] - b_row[i] + i_row[i]
        mn = jnp.maximum(b_last[i] + m_prev[i], jnp.max(g_row, axis=-1, keepdims=True))
        decay = jnp.exp(b_last[i] + m_prev[i] - mn)
        w_row = jnp.exp(g_row - mn)
        ktw = (k[i].T.astype(F32) * w_row).astype(BF16)
        kv = jnp.dot(ktw, v[i], preferred_element_type=F32)
        wk = jnp.dot(jnp.broadcast_to(w_row, (8, L)).astype(BF16), k[i], preferred_element_type=F32)
        c_new = decay * c_scr[i] + kv
        n_new = decay * n_scr[i] + wk
        c_scr[i] = c_new
        n_scr[i] = n_new
        m_scr[i] = jnp.broadcast_to(mn, (1, LANES))

    for i, (d, hh) in enumerate(scans):
        h_ref = (hf_ref, hb_ref)[d]
        log_d = jnp.where(keep_t[d], b_row[i] - cb[i], -jnp.inf)
        m_inter = b_row[i] + m_prev[i]
        m_t = jnp.maximum(m_inter, jnp.max(log_d, axis=0, keepdims=True))
        p_t = s_t[i] * jnp.exp(log_d - m_t)
        inter = jnp.exp(m_inter - m_t)
        den = jnp.sum(p_t, axis=0, keepdims=True) + inter * qn[i]
        scale = 1.0 / jnp.maximum(jnp.abs(den), jnp.exp(-m_t))
        lhs_t = jnp.concatenate([(p_t * scale).astype(BF16), jnp.where(eye, scale * inter, 0.0).astype(BF16)], axis=0)
        rhs = jnp.concatenate([v[i], qc[i]], axis=0)
        h = lax.dot_general(lhs_t, rhs, (((0,), (0,)), ((), ())), preferred_element_type=F32)
        h_ref[0, :, hh * dh:(hh + 1) * dh] = h.astype(h_ref.dtype)

    @pl.when(c == nc - 1)
    def _():
        cN_ref[0] = c_scr[...]
        nN_ref[0] = n_scr[...]
        mN_ref[0] = m_scr[...]


def _mlstm(z, zblk, gates_t, init, dh):
    b, t, _ = z.shape
    L = MLSTM_CHUNK
    nc = t // L
    width = N_HEADS * dh
    ns = N_DIRS * N_HEADS
    fwd = lambda bi, c: (bi, c, 0)
    bwd = lambda bi, c: (bi, nc - 1 - c, 0)
    st4 = lambda bi, c: (bi, 0, 0, 0)
    in_specs = [pl.BlockSpec((1, L, 3 * width), lambda bi, c: (bi, c, zblk)),
                pl.BlockSpec((1, L, 3 * width), lambda bi, c: (bi, nc - 1 - c, zblk)),
                pl.BlockSpec((1, N_GATE_COLS, L), lambda bi, c: (bi, 0, c)),
                pl.BlockSpec((1, N_GATE_COLS, L), lambda bi, c: (bi, 0, nc - 1 - c))]
    args = [z, z, gates_t, gates_t]
    state_specs = [pl.BlockSpec((1, ns, dh, dh), st4), pl.BlockSpec((1, ns, 8, dh), st4),
                   pl.BlockSpec((1, ns, 1, LANES), st4)]
    state_shapes = [jax.ShapeDtypeStruct((b, ns, dh, dh), F32), jax.ShapeDtypeStruct((b, ns, 8, dh), F32),
                    jax.ShapeDtypeStruct((b, ns, 1, LANES), F32)]
    if init is not None:
        in_specs += state_specs
        args += list(init)
    outs = pl.pallas_call(
        functools.partial(_mlstm_kernel, dh=dh, has_init=init is not None, nc=nc),
        grid=(b, nc),
        in_specs=in_specs,
        out_specs=[pl.BlockSpec((1, L, width), fwd), pl.BlockSpec((1, L, width), bwd)] + state_specs,
        out_shape=[jax.ShapeDtypeStruct((b, t, width), BF16), jax.ShapeDtypeStruct((b, t, width), BF16)] + state_shapes,
        scratch_shapes=[pltpu.VMEM((ns, dh, dh), F32), pltpu.VMEM((ns, 8, dh), F32), pltpu.VMEM((ns, 1, LANES), F32)],
        compiler_params=_params(("arbitrary", "arbitrary")),
        name="mlstm",
    )(*args)
    return outs[0], outs[1], tuple(outs[2:])


def _layer_norm(v, w, b):
    mu = jnp.mean(v, axis=-1, keepdims=True)
    vc = v - mu
    var = jnp.mean(vc * vc, axis=-1, keepdims=True)
    return vc * lax.rsqrt(var + EPS) * w + b


def _gelu_tanh(v):
    return 0.5 * v * (1.0 + jnp.tanh(0.7978845608028654 * (v + 0.044715 * (v * v * v))))


def _mixer_kernel(z_ref, hf_ref, hb_ref, x_ref, g1_ref, mnw_ref, wm_ref, dww_ref, dwb_ref, clw_ref, clb_ref,
                  wc_ref, slw_ref, slb_ref, sgw_ref, sgb_ref, ws_ref, wo_ref, o_ref, pad_scr, conv_scr,
                  *, d, row_len):
    tm = z_ref.shape[1]
    dh = d // N_HEADS
    cc = d // 2
    off_conv, off_sg, off_merge = d, 2 * d, 3 * d

    y_m = jnp.zeros((tm, d), F32)
    for hh in range(N_HEADS):
        sl = slice(hh * dh, (hh + 1) * dh)
        hm = hf_ref[0, :, sl].astype(F32) + hb_ref[0, :, sl].astype(F32)
        yn = hm * lax.rsqrt(jnp.mean(hm * hm, axis=-1, keepdims=True) + EPS) * mnw_ref[:, sl]
        og = _sigmoid(z_ref[0, :, sl]).astype(F32)
        y_m += jnp.dot((yn * og).astype(BF16), wm_ref[sl, :], preferred_element_type=F32)

    u = z_ref[0, :, off_conv:off_conv + cc].astype(F32) * _sigmoid(z_ref[0, :, off_conv + cc:off_conv + 2 * cc]).astype(F32)
    n_rows = tm // row_len
    zpad = jnp.zeros((CONV_PAD, cc), F32)
    for r in range(n_rows):
        pad_scr[0, r, 0:CONV_PAD, :] = zpad
        pad_scr[0, r, CONV_PAD:CONV_PAD + row_len, :] = u[r * row_len:(r + 1) * row_len, :]
        pad_scr[0, r, CONV_PAD + row_len:2 * CONV_PAD + row_len, :] = zpad
    span = row_len + 2 * CONV_PAD - SUBLANES
    for j in range(1, SUBLANES):
        pad_scr[j, :, 0:span, :] = pad_scr[0, :, j:j + span, :]
    base = CONV_PAD - DW_CONV_SIZE // 2
    for cb in range(cc // LANES):
        ls = slice(cb * LANES, (cb + 1) * LANES)
        taps = [dww_ref[kk:kk + 1, ls] for kk in range(DW_CONV_SIZE)]
        for r in range(n_rows):
            acc = jnp.zeros((row_len, LANES), F32)
            for kk in range(DW_CONV_SIZE):
                hi, lo = divmod(base + kk, SUBLANES)
                acc += pad_scr[lo, r, hi * SUBLANES:hi * SUBLANES + row_len, ls] * taps[kk]
            conv_scr[r * row_len:(r + 1) * row_len, ls] = acc
    cv = _layer_norm(conv_scr[...] + dwb_ref[...], clw_ref[...], clb_ref[...])
    cv = cv * _sigmoid(cv)
    y_c = jnp.dot(cv.astype(BF16), wc_ref[...], preferred_element_type=F32)

    su = _gelu_tanh(z_ref[0, :, off_sg:off_sg + cc].astype(F32))
    sv = _gelu_tanh(z_ref[0, :, off_sg + cc:off_sg + 2 * cc].astype(F32))
    sv = _layer_norm(sv, slw_ref[...], slb_ref[...]).astype(BF16)
    gd = cc // SG_GROUPS
    for ch in range(tm // SG_CHUNK):
        rs = slice(ch * SG_CHUNK, (ch + 1) * SG_CHUNK)
        parts = []
        for gi in range(SG_GROUPS):
            mixed = jnp.dot(sgw_ref[gi], sv[rs, gi * gd:(gi + 1) * gd], preferred_element_type=F32)
            parts.append(mixed + sgb_ref[:, gi:gi + 1])
        gated = su[rs, :] * jnp.concatenate(parts, axis=1)
        conv_scr[rs, :] = gated
    y_s = jnp.dot(conv_scr[...].astype(BF16), ws_ref[...], preferred_element_type=F32)

    gm0 = _sigmoid(z_ref[0, :, off_merge:off_merge + d]).astype(F32)
    gm1 = _sigmoid(z_ref[0, :, off_merge + d:off_merge + 2 * d]).astype(F32)
    gm2 = _sigmoid(z_ref[0, :, off_merge + 2 * d:off_merge + 3 * d]).astype(F32)
    merged = gm0 * y_m + gm1 * y_c + gm2 * y_s
    y = jnp.dot(merged.astype(BF16), wo_ref[...], preferred_element_type=F32)
    o_ref[0] = x_ref[0] + g1_ref[0] * y


def _mixer_out(z, zblk, hf, hb, x, g1, p, row_len, tm):
    b, t, d = x.shape
    cc = d // 2
    tm = min(tm, t)
    nz = 6 * d
    full = lambda a: pl.BlockSpec(a.shape, lambda bi, i: (0,) * a.ndim)
    row = lambda a: a.reshape(1, -1)
    consts = [row(p["mlstm_norm_w"]), p["w_mlstm_out"], p["conv_dw_w"], row(p["conv_dw_b"]), row(p["conv_ln_w"]),
              row(p["conv_ln_b"]), p["w_conv_out"], row(p["sg_ln_w"]), row(p["sg_ln_b"]), p["sg_w"], p["sg_b"].T,
              p["w_sg_out"], p["w_o"]]
    tok = lambda w: pl.BlockSpec((1, tm, w), lambda bi, i: (bi, i, 0))
    return pl.pallas_call(
        functools.partial(_mixer_kernel, d=d, row_len=row_len),
        grid=(b, t // tm),
        in_specs=[pl.BlockSpec((1, tm, nz), lambda bi, i: (bi, i, zblk)), tok(d), tok(d), tok(d),
                  pl.BlockSpec((1, 1, d), lambda bi, i: (bi, 0, 0))] + [full(a) for a in consts],
        out_specs=tok(d),
        out_shape=jax.ShapeDtypeStruct((b, t, d), F32),
        scratch_shapes=[pltpu.VMEM((SUBLANES, tm // row_len, row_len + 2 * CONV_PAD, cc), F32),
                        pltpu.VMEM((tm, cc), F32)],
        compiler_params=_params(("arbitrary", "arbitrary")),
        name="mixer_out",
    )(z, hf, hb, x, g1, *consts)


def _router_kernel(x_ref, nw_ref, sh_ref, sc_ref, wr_ref, rb_ref, hp_ref, at_ref):
    h = _norm_mod(x_ref[0], nw_ref[...], sh_ref[0], sc_ref[0])
    h_hi, h_lo = _split_bf16(h)
    wr = wr_ref[...]
    raw = jnp.dot(h_hi, wr, preferred_element_type=F32) + jnp.dot(h_lo, wr, preferred_element_type=F32)
    logits = raw[:, :N_EXPERTS] + raw[:, N_EXPERTS:2 * N_EXPERTS] + rb_ref[...]
    mx = jnp.max(logits, axis=-1, keepdims=True)
    ex = jnp.exp(logits - mx)
    aff = ex / jnp.sum(ex, axis=-1, keepdims=True)
    apad = jnp.concatenate([aff, jnp.zeros((aff.shape[0], LANES - N_EXPERTS), F32)], axis=1)
    at_ref[0] = apad.T[:N_EXPERTS, :]
    half = h.shape[1] // 2
    bits = lax.bitcast_convert_type(h_hi.astype(F32), U32)
    hp_ref[0, :, :half] = (bits[:, :half] >> 16) | (bits[:, half:] & jnp.uint32(0xFFFF0000))
    hp_ref[0, :, half:] = lax.bitcast_convert_type(apad, U32)


def _router(x, nw, shift, scale, wr2, rb, tm):
    b, t, d = x.shape
    tm = min(tm, t)
    return pl.pallas_call(
        _router_kernel,
        grid=(b, t // tm),
        in_specs=[pl.BlockSpec((1, tm, d), lambda bi, i: (bi, i, 0)),
                  pl.BlockSpec((1, d), lambda bi, i: (0, 0)),
                  pl.BlockSpec((1, 1, d), lambda bi, i: (bi, 0, 0)),
                  pl.BlockSpec((1, 1, d), lambda bi, i: (bi, 0, 0)),
                  pl.BlockSpec((d, LANES), lambda bi, i: (0, 0)),
                  pl.BlockSpec((1, N_EXPERTS), lambda bi, i: (0, 0))],
        out_specs=[pl.BlockSpec((1, tm, d // 2 + LANES), lambda bi, i: (bi, i, 0)),
                   pl.BlockSpec((1, N_EXPERTS, tm), lambda bi, i: (bi, 0, i))],
        out_shape=[jax.ShapeDtypeStruct((b, t, d // 2 + LANES), U32), jax.ShapeDtypeStruct((b, N_EXPERTS, t), F32)],
        compiler_params=_params(("arbitrary", "arbitrary")),
        name="router",
    )(x, nw.reshape(1, d), shift, scale, wr2, rb.reshape(1, N_EXPERTS))


def _prefix_count(mask, tri):
    e, t = mask.shape
    nb = t // LANES
    stacked = jnp.concatenate([mask[:, c * LANES:(c + 1) * LANES] for c in range(nb)], axis=0).astype(BF16)
    local = jnp.dot(stacked, tri, preferred_element_type=F32)
    out = []
    off = jnp.zeros((e, 1), F32)
    for c in range(nb):
        blk = local[c * e:(c + 1) * e, :]
        out.append(blk + off)
        off = off + blk[:, LANES - 1:LANES]
    return jnp.concatenate(out, axis=1)


def _select_kernel(at_ref, idx_ref, cnt_scr, icol_scr, *, cap, jt):
    a = at_ref[0]
    e, t = a.shape
    bits = lax.bitcast_convert_type(a, I32)

    def search(i, v):
        cand = v | jnp.left_shift(jnp.int32(1), 30 - i)
        cnt = jnp.sum((bits >= cand).astype(I32), axis=1, keepdims=True)
        return jnp.where(cnt >= cap, cand, v)

    thr = lax.fori_loop(0, 31, search, jnp.zeros((e, 1), I32))
    r = lax.broadcasted_iota(I32, (LANES, LANES), 0)
    s = lax.broadcasted_iota(I32, (LANES, LANES), 1)
    tri = (r <= s).astype(BF16)
    gt = bits > thr
    eq = bits == thr
    need = (cap - jnp.sum(gt.astype(I32), axis=1, keepdims=True)).astype(F32)
    sel = gt | (eq & (_prefix_count(eq.astype(F32), tri) <= need))
    cnt = _prefix_count(sel.astype(F32), tri)
    for ei in range(e):
        cnt_scr[ei] = cnt[ei:ei + 1, :]
    icol_scr[...] = jnp.zeros_like(icol_scr)
    lane = lax.broadcasted_iota(I32, (jt, LANES), 1)

    def per_expert(ei, carry):
        for jb in range(cap // jt):
            jcol = (lax.broadcasted_iota(I32, (jt, 1), 0) + jb * jt).astype(F32)
            acc_n = jnp.zeros((jt, LANES), F32)
            for tb in range(t // LANES):
                cn = cnt_scr[ei, :, tb * LANES:(tb + 1) * LANES]
                acc_n += jnp.where(cn <= jcol, 1.0, 0.0)
            tok = jnp.sum(acc_n, axis=1, keepdims=True)
            rows = slice(jb * jt, (jb + 1) * jt)
            icol_scr[rows, :] = jnp.where(lane == ei, tok, icol_scr[rows, :])
        return carry

    lax.fori_loop(0, e, per_expert, 0)
    idx_ref[0, :, 0, :] = icol_scr[...].T[:e, :cap].astype(I32)


def _route_select(aff_t, cap):
    b, e, t = aff_t.shape
    jt = min(cap, 64)
    return pl.pallas_call(
        functools.partial(_select_kernel, cap=cap, jt=jt),
        grid=(b,),
        in_specs=[pl.BlockSpec((1, e, t), lambda bi: (bi, 0, 0))],
        out_specs=pl.BlockSpec((1, e, 1, cap), lambda bi: (bi, 0, 0, 0)),
        out_shape=jax.ShapeDtypeStruct((b, e, 1, cap), I32),
        scratch_shapes=[pltpu.VMEM((e, 1, t), F32), pltpu.VMEM((max(cap, LANES), LANES), F32)],
        compiler_params=_params(("arbitrary",)),
        name="route_select",
    )(aff_t)


def _moe_kernel(idxp_ref, idxn_ref, hp_ref, g2_ref, wg_ref, wu_ref, wd_ref, x_hbm, o_hbm,
                xe32_scr, xe_scr, gate_scr, y_scr, yg_scr, acc_scr, sem_in, sem_out, *, gsz, cap, t, n_groups, n_f):
    grp = pl.program_id(0)
    e = pl.program_id(1)
    f = pl.program_id(2)
    n_e = pl.num_programs(1)
    half = hp_ref.shape[2] - LANES
    q_rows = cap // n_f

    def in_copy(gi):
        return pltpu.make_async_copy(x_hbm.at[pl.ds(gi * gsz, gsz)], acc_scr.at[:, pl.ds(0, t)], sem_in)

    def out_copy(gi):
        return pltpu.make_async_copy(acc_scr.at[:, pl.ds(0, t)], o_hbm.at[pl.ds(gi * gsz, gsz)], sem_out)

    def unpack(expert):
        p = xe32_scr[:, :half]
        xe_scr[:, :half] = lax.bitcast_convert_type(p << 16, F32).astype(BF16)
        xe_scr[:, half:] = lax.bitcast_convert_type(p & jnp.uint32(0xFFFF0000), F32).astype(BF16)
        aff = lax.bitcast_convert_type(xe32_scr[:, half:], F32)
        lane = lax.broadcasted_iota(I32, aff.shape, 1)
        gate = jnp.sum(jnp.where(lane == expert, aff, 0.0), axis=1, keepdims=True)
        gate_scr[...] = jnp.broadcast_to(gate, gate_scr.shape)

    def gated():
        for si in range(gsz):
            rs = slice(si * cap, (si + 1) * cap)
            g2 = g2_ref[si]
            for cbk in range(y_scr.shape[1] // LANES):
                ls = slice(cbk * LANES, (cbk + 1) * LANES)
                yg_scr[rs, ls] = y_scr[rs, ls] * gate_scr[rs, :] * g2[:, ls]

    @pl.when((e == 0) & (f == 0))
    def _():
        @pl.when(grp > 0)
        def _():
            out_copy(grp - 1).wait()
        in_copy(grp).start()
        for si in range(gsz):
            def gather(j, carry):
                tkn = idxp_ref[si, 0, 0, j]
                xe32_scr[pl.ds(si * cap + j, 1), :] = hp_ref[si, pl.ds(tkn, 1), :]
                return carry
            lax.fori_loop(0, cap, gather, 0, unroll=8)
        unpack(e)
        y_scr[...] = jnp.zeros_like(y_scr)
        yg_scr[...] = jnp.zeros_like(yg_scr)
        acc_scr[:, t:t + 8, :] = jnp.zeros((gsz, 8, acc_scr.shape[2]), F32)

    @pl.when((e > 0) & (f == 0))
    def _():
        gated()
        unpack(e)

    @pl.when((e == 1) & (f == 0))
    def _():
        in_copy(grp).wait()

    xe = xe_scr[...]
    a = jnp.dot(xe, wg_ref[0], preferred_element_type=F32)
    u = jnp.dot(xe, wu_ref[0], preferred_element_type=F32)
    hid = (a * _sigmoid(a) * u).astype(BF16)
    part = jnp.dot(hid, wd_ref[0], preferred_element_type=F32)
    spare = jnp.where(e == 0, 1, 0)
    for si in range(gsz):
        for jj in range(q_rows):
            j = f * q_rows + jj
            tkn = idxn_ref[si, 0, 0, j]
            xe32_scr[pl.ds(si * cap + j, 1), :] = hp_ref[si, pl.ds(tkn, 1), :]
        for jj in range(q_rows):
            j = f * q_rows + jj
            tkn = jnp.where(spare == 1, t, idxp_ref[si, 0, 0, j])
            acc_scr[si, pl.ds(tkn, 1), :] += yg_scr[pl.ds(si * cap + j, 1), :]

    y_scr[...] = jnp.where(f > 0, y_scr[...], 0.0) + part

    @pl.when((e == n_e - 1) & (f == n_f - 1))
    def _():
        gated()
        for si in range(gsz):
            def scatter(j, carry):
                tkn = idxn_ref[si, 0, 0, j]
                acc_scr[si, pl.ds(tkn, 1), :] += yg_scr[pl.ds(si * cap + j, 1), :]
                return carry
            lax.fori_loop(0, cap, scatter, 0, unroll=8)
        out_copy(grp).start()

        @pl.when(grp == n_groups - 1)
        def _():
            out_copy(grp).wait()


def _moe_ffn(idx, hp, g2, x, wg, wu, wd, layer, gsz, tf):
    b, t, d = x.shape
    e, cap = idx.shape[1], idx.shape[3]
    ff = wg.shape[3]
    n_groups, n_f = b // gsz, ff // tf
    hw = hp.shape[2]
    smem_idx = lambda fn: pl.BlockSpec((gsz, 1, 1, cap), fn, memory_space=pltpu.SMEM)
    return pl.pallas_call(
        functools.partial(_moe_kernel, gsz=gsz, cap=cap, t=t, n_groups=n_groups, n_f=n_f),
        grid=(n_groups, e, n_f),
        in_specs=[smem_idx(lambda gi, ei, fi: (gi, jnp.maximum(ei - 1, 0), 0, 0)),
                  smem_idx(lambda gi, ei, fi: (gi, jnp.minimum(ei + 1, e - 1), 0, 0)),
                  pl.BlockSpec((gsz, t, hw), lambda gi, ei, fi: (gi, 0, 0), pipeline_mode=pl.Buffered(1)),
                  pl.BlockSpec((gsz, 1, d), lambda gi, ei, fi: (gi, 0, 0)),
                  pl.BlockSpec((None, 1, d, tf), lambda gi, ei, fi: (layer, ei, 0, fi)),
                  pl.BlockSpec((None, 1, d, tf), lambda gi, ei, fi: (layer, ei, 0, fi)),
                  pl.BlockSpec((None, 1, tf, d), lambda gi, ei, fi: (layer, ei, fi, 0)),
                  pl.BlockSpec(memory_space=pl.ANY)],
        out_specs=pl.BlockSpec(memory_space=pl.ANY),
        out_shape=jax.ShapeDtypeStruct((b, t, d), F32),
        scratch_shapes=[pltpu.VMEM((gsz * cap, hw), U32), pltpu.VMEM((gsz * cap, d), BF16),
                        pltpu.VMEM((gsz * cap, LANES), F32),
                        pltpu.VMEM((gsz * cap, d), F32), pltpu.VMEM((gsz * cap, d), F32),
                        pltpu.VMEM((gsz, t + 8, d), F32),
                        pltpu.SemaphoreType.DMA(()), pltpu.SemaphoreType.DMA(())],
        compiler_params=_params(("arbitrary", "arbitrary", "arbitrary")),
        name="moe_ffn",
    )(idx, idx, hp, g2, wg, wu, wd, x)


def _final_norm_kernel(x_ref, w_ref, o_ref):
    x = x_ref[0]
    o_ref[0] = x * lax.rsqrt(jnp.mean(x * x, axis=-1, keepdims=True) + EPS) * w_ref[...]


def _final_norm(x, w, tm=1024):
    b, t, d = x.shape
    tm = min(tm, t)
    return pl.pallas_call(
        _final_norm_kernel,
        grid=(b, t // tm),
        in_specs=[pl.BlockSpec((1, tm, d), lambda bi, i: (bi, i, 0)), pl.BlockSpec((1, d), lambda bi, i: (0, 0))],
        out_specs=pl.BlockSpec((1, tm, d), lambda bi, i: (bi, i, 0)),
        out_shape=jax.ShapeDtypeStruct((b, t, d), F32),
        compiler_params=_params(("arbitrary", "arbitrary")),
        name="final_norm",
    )(x, w.reshape(1, d))


def _stack_hi_lo(w):
    hi, lo = _split_bf16(w)
    pad = jnp.zeros((w.shape[0], LANES - 2 * w.shape[1]), BF16)
    return jnp.concatenate([hi, lo, pad], axis=1)


def _moe(x, nw, shift, scale, g2, wr2, rb, wg, wu, wd, layer, gsz, tm):
    t = x.shape[1]
    cap = CAPACITY_FACTOR * t // N_EXPERTS
    hp, aff_t = _router(x, nw, shift, scale, wr2, rb, tm)
    idx = _route_select(aff_t, cap)
    return _moe_ffn(idx, hp, g2, x, wg, wu, wd, layer, gsz, tf=1024)


def kernel(x, c, ctx, c_ctx, ada_w, ada_b, norm1_w, norm2_w, w_in, mlstm_gate_b, mlstm_norm_w, w_mlstm_out,
           conv_dw_w, conv_dw_b, conv_ln_w, conv_ln_b, w_conv_out, sg_ln_w, sg_ln_b, sg_w, sg_b, w_sg_out, w_o,
           router_w, router_b, expert_w_gate, expert_w_up, expert_w_down, final_norm_w):
    depth = ada_w.shape[0]
    b, t, d = x.shape
    dh = d // N_HEADS
    n_state = 3 * d + N_GATE_COLS
    cond = jnp.concatenate([c, c_ctx[None, :], jnp.zeros((16 - b - 1, d), F32)], axis=0)
    wg, wu, wd = expert_w_gate.astype(BF16), expert_w_up.astype(BF16), expert_w_down.astype(BF16)

    for layer in range(depth):
        need_ctx = layer < depth - 1
        mod = _modulation(cond, ada_w, ada_b, layer)
        lat = [m[:, None, :] for m in jnp.split(mod[:b], 6, axis=-1)]
        cx = [jnp.broadcast_to(m[None], (b, 1, d)) for m in jnp.split(mod[b:b + 1], 6, axis=-1)]

        wl = w_in[layer]
        w_qkv = wl[:, :3 * d].astype(BF16)
        w_main = jnp.concatenate([wl[:, n_state:].astype(BF16), w_qkv], axis=1)
        w_gate2 = _stack_hi_lo(wl[:, 3 * d:n_state])
        z_lat, gt_lat = _in_proj(x, norm1_w[layer], lat[0], lat[1], w_main, w_gate2, mlstm_gate_b[layer],
                                 tm=1024, tn=1024)
        z_ctx, gt_ctx = _in_proj(ctx, norm1_w[layer], cx[0], cx[1], w_main if need_ctx else w_qkv, w_gate2,
                                 mlstm_gate_b[layer], tm=256, tn=1024)

        hcf, hcb, state = _mlstm(z_ctx, 2 if need_ctx else 0, gt_ctx, None, dh)
        hlf, hlb, _ = _mlstm(z_lat, 2, gt_lat, state, dh)

        bf = lambda a: a[layer].astype(BF16)
        p = dict(mlstm_norm_w=mlstm_norm_w[layer], w_mlstm_out=bf(w_mlstm_out), conv_dw_w=conv_dw_w[layer],
                 conv_dw_b=conv_dw_b[layer], conv_ln_w=conv_ln_w[layer], conv_ln_b=conv_ln_b[layer],
                 w_conv_out=bf(w_conv_out), sg_ln_w=sg_ln_w[layer], sg_ln_b=sg_ln_b[layer], sg_w=bf(sg_w),
                 sg_b=sg_b[layer], w_sg_out=bf(w_sg_out), w_o=bf(w_o))
        x = _mixer_out(z_lat, 0, hlf, hlb, x, lat[2], p, row_len=GRID_W, tm=256)
        if need_ctx:
            ctx = _mixer_out(z_ctx, 0, hcf, hcb, ctx, cx[2], p, row_len=ctx.shape[1], tm=256)

        wr2 = _stack_hi_lo(router_w[layer])
        x = _moe(x, norm2_w[layer], lat[3], lat[4], lat[5], wr2, router_b[layer], wg, wu, wd, layer, gsz=1, tm=1024)
        if need_ctx:
            ctx = _moe(ctx, norm2_w[layer], cx[3], cx[4], cx[5], wr2, router_b[layer], wg, wu, wd, layer, gsz=b, tm=256)

    return _final_norm(x, final_norm_w)
```

```python
import functools

import jax
import jax.numpy as jnp
from jax import lax
from jax.experimental import pallas as pl
from jax.experimental.pallas import tpu as pltpu

F32 = jnp.float32
BF16 = jnp.bfloat16
I32 = jnp.int32
U32 = jnp.uint32

EPS = 1e-6
LANES = 128
SUBLANES = 8
VMEM_LIMIT = 56 * 1024 * 1024

N_HEADS = 4
MLSTM_CHUNK = 128
N_DIRS = 2
N_GATE_COLS = N_DIRS * 2 * N_HEADS
DW_CONV_SIZE = 31
CONV_PAD = 16
SG_GROUPS = 4
SG_CHUNK = 128
N_EXPERTS = 16
CAPACITY_FACTOR = 2
GRID_W = 64
NORM_ROWS = 128


def _params(sem, vmem=VMEM_LIMIT):
    return pltpu.CompilerParams(dimension_semantics=sem, vmem_limit_bytes=vmem)


def _sigmoid(v):
    return 0.5 * jnp.tanh(0.5 * v) + 0.5


def _split_bf16(a):
    hi = a.astype(BF16)
    lo = (a - hi.astype(F32)).astype(BF16)
    return hi, lo


def _split3_bf16(a):
    hi = a.astype(BF16)
    r1 = a - hi.astype(F32)
    mid = r1.astype(BF16)
    lo = (r1 - mid.astype(F32)).astype(BF16)
    return hi, mid, lo


def _norm_mod(x, nw, shift, scale):
    ms = jnp.mean(x * x, axis=-1, keepdims=True)
    y = x * lax.rsqrt(ms + EPS) * nw
    return y * (1.0 + scale) + shift


def _modulation_kernel(a_ref, w_ref, b_ref, o_ref):
    a = a_ref[...]
    a = a * _sigmoid(a)
    a_hi, a_lo = _split_bf16(a)
    w_hi, w_lo = _split_bf16(w_ref[...])
    acc = jnp.dot(a_hi, w_hi, preferred_element_type=F32)
    acc += jnp.dot(a_hi, w_lo, preferred_element_type=F32)
    acc += jnp.dot(a_lo, w_hi, preferred_element_type=F32)
    o_ref[...] = acc + b_ref[...]


def _modulation(cond, w, b, layer, tn=512):
    m, d = cond.shape
    n = w.shape[2]
    return pl.pallas_call(
        _modulation_kernel,
        grid=(n // tn,),
        in_specs=[pl.BlockSpec((m, d), lambda j: (0, 0)),
                  pl.BlockSpec((None, d, tn), lambda j: (layer, 0, j)),
                  pl.BlockSpec((None, 1, tn), lambda j: (layer, 0, j))],
        out_specs=pl.BlockSpec((m, tn), lambda j: (0, j)),
        out_shape=jax.ShapeDtypeStruct((m, n), F32),
        compiler_params=_params(("arbitrary",)),
        name="modulation",
    )(cond, w, b.reshape(b.shape[0], 1, n))


def _in_proj_kernel(x_ref, nw_ref, sh_ref, sc_ref, w_ref, wg_ref, gb_ref, z_ref, gt_ref, h_scr):
    j = pl.program_id(2)

    @pl.when(j == 0)
    def _():
        h = _norm_mod(x_ref[0], nw_ref[...], sh_ref[0], sc_ref[0])
        h_hi, h_lo = _split_bf16(h)
        h_scr[...] = h_hi
        wg = wg_ref[...]
        raw = jnp.dot(h_hi, wg, preferred_element_type=F32) + jnp.dot(h_lo, wg, preferred_element_type=F32)
        raw = raw[:, :N_GATE_COLS] + raw[:, N_GATE_COLS:2 * N_GATE_COLS] + gb_ref[...]
        col = lax.broadcasted_iota(I32, raw.shape, 1)
        is_forget = ((col >> 2) & 1) == 1
        logsig = jnp.minimum(raw, 0.0) - jnp.log(1.0 + jnp.exp(-jnp.abs(raw)))
        g = jnp.where(is_forget, logsig, raw)
        gpad = jnp.concatenate([g, jnp.zeros((g.shape[0], LANES - N_GATE_COLS), F32)], axis=1)
        gt_ref[0] = gpad.T[:N_GATE_COLS, :]

    z_ref[0] = jnp.dot(h_scr[...], w_ref[...], preferred_element_type=F32).astype(BF16)


def _in_proj(x, nw, shift, scale, w_main, w_gate2, gate_b, tm, tn):
    b, t, d = x.shape
    n = w_main.shape[1]
    tm = min(tm, t)
    return pl.pallas_call(
        _in_proj_kernel,
        grid=(b, t // tm, n // tn),
        in_specs=[pl.BlockSpec((1, tm, d), lambda bi, i, j: (bi, i, 0)),
                  pl.BlockSpec((1, d), lambda bi, i, j: (0, 0)),
                  pl.BlockSpec((1, 1, d), lambda bi, i, j: (bi, 0, 0)),
                  pl.BlockSpec((1, 1, d), lambda bi, i, j: (bi, 0, 0)),
                  pl.BlockSpec((d, tn), lambda bi, i, j: (0, j)),
                  pl.BlockSpec((d, LANES), lambda bi, i, j: (0, 0)),
                  pl.BlockSpec((1, N_GATE_COLS), lambda bi, i, j: (0, 0))],
        out_specs=[pl.BlockSpec((1, tm, tn), lambda bi, i, j: (bi, i, j)),
                   pl.BlockSpec((1, N_GATE_COLS, tm), lambda bi, i, j: (bi, 0, i))],
        out_shape=[jax.ShapeDtypeStruct((b, t, n), BF16),
                   jax.ShapeDtypeStruct((b, N_GATE_COLS, t), F32)],
        scratch_shapes=[pltpu.VMEM((tm, d), BF16)],
        compiler_params=_params(("arbitrary", "arbitrary", "arbitrary")),
        name="in_proj",
    )(x, nw.reshape(1, d), shift, scale, w_main, w_gate2, gate_b.reshape(1, N_GATE_COLS))


def _mlstm_kernel(*refs, dh, has_init, nc):
    if has_init:
        (zf_ref, zb_ref, gtf_ref, gtb_ref, c0_ref, n0_ref, m0_ref,
         hf_ref, hb_ref, cN_ref, nN_ref, mN_ref, c_scr, n_scr, m_scr) = refs
    else:
        zf_ref, zb_ref, gtf_ref, gtb_ref, hf_ref, hb_ref, cN_ref, nN_ref, mN_ref, c_scr, n_scr, m_scr = refs
    c = pl.program_id(1)
    L = MLSTM_CHUNK
    width = N_HEADS * dh
    scans = [(d, hh) for d in range(N_DIRS) for hh in range(N_HEADS)]
    n_s = len(scans)

    @pl.when(c == 0)
    def _():
        if has_init:
            c_scr[...] = c0_ref[0]
            n_scr[...] = n0_ref[0]
            m_scr[...] = m0_ref[0]
        else:
            c_scr[...] = jnp.zeros_like(c_scr)
            n_scr[...] = jnp.zeros_like(n_scr)
            m_scr[...] = jnp.zeros_like(m_scr)

    row = lax.broadcasted_iota(I32, (L, L), 0)
    colm = lax.broadcasted_iota(I32, (L, L), 1)
    eye = row == colm
    eye_bf = eye.astype(BF16)
    eye3 = jnp.concatenate([eye_bf, eye_bf, eye_bf], axis=1)
    keep_t = (row <= colm, row >= colm)
    tri = (keep_t[0].astype(BF16), keep_t[1].astype(BF16))

    gts = [(gtf_ref, gtb_ref)[d][0] for d in range(N_DIRS)]
    cums = [sum(jnp.dot(part, tri[d], preferred_element_type=F32) for part in _split3_bf16(gts[d]))
            for d in range(N_DIRS)]

    q, k, v, m_prev, i_row, b_row, b_last = [], [], [], [], [], [], []
    for si, (d, hh) in enumerate(scans):
        z_ref = (zf_ref, zb_ref)[d]
        q.append(z_ref[0, :, hh * dh:(hh + 1) * dh])
        k.append(z_ref[0, :, width + hh * dh:width + (hh + 1) * dh] * jnp.asarray(dh ** -0.5, BF16))
        v.append(z_ref[0, :, 2 * width + hh * dh:2 * width + (hh + 1) * dh])
        m_prev.append(m_scr[si][:, 0:1])
        ci = d * 2 * N_HEADS + hh
        cf = ci + N_HEADS
        i_row.append(gts[d][ci:ci + 1, :])
        b_row.append(cums[d][cf:cf + 1, :])
        b_last.append(b_row[si][:, L - 1:L] if d == 0 else b_row[si][:, 0:1])

    nt = (((1,), (1,)), ((), ()))
    s_t = [lax.dot_general(k[i], q[i], nt, preferred_element_type=F32) for i in range(n_s)]
    qc = [jnp.dot(q[i], c_scr[i].astype(BF16), preferred_element_type=F32).astype(BF16) for i in range(n_s)]
    qn = [lax.dot_general(n_scr[i].astype(BF16), q[i], nt, preferred_element_type=F32)[0:1, :] for i in range(n_s)]
    cb = []
    for i in range(n_s):
        parts = [jnp.broadcast_to(p, (LANES, L)) for p in _split3_bf16(b_row[i] - i_row[i])]
        cb.append(lax.dot_general(eye3, jnp.concatenate(parts, axis=1), nt, preferred_element_type=F32))

    for i in range(n_s):
        g_row = b_last[i] - b_row[i] + i_row[i]
        mn = jnp.maximum(b_last[i] + m_prev[i], jnp.max(g_row, axis=-1, keepdims=True))
        decay = jnp.exp(b_last[i] + m_prev[i] - mn)
        w_row = jnp.exp(g_row - mn)
        ktw = (k[i].T.astype(F32) * w_row).astype(BF16)
        kv = jnp.dot(ktw, v[i], preferred_element_type=F32)
        wk = jnp.dot(jnp.broadcast_to(w_row, (8, L)).astype(BF16), k[i], preferred_element_type=F32)
        c_new = decay * c_scr[i] + kv
        n_new = decay * n_scr[i] + wk
        c_scr[i] = c_new
        n_scr[i] = n_new
        m_scr[i] = jnp.broadcast_to(mn, (1, LANES))

    for i, (d, hh) in enumerate(scans):
        h_ref = (hf_ref, hb_ref)[d]
        log_d = jnp.where(keep_t[d], b_row[i] - cb[i], -jnp.inf)
        m_inter = b_row[i] + m_prev[i]
        m_t = jnp.maximum(m_inter, jnp.max(log_d, axis=0, keepdims=True))
        p_t = s_t[i] * jnp.exp(log_d - m_t)
        inter = jnp.exp(m_inter - m_t)
        den = jnp.sum(p_t, axis=0, keepdims=True) + inter * qn[i]
        scale = 1.0 / jnp.maximum(jnp.abs(den), jnp.exp(-m_t))
        lhs_t = jnp.concatenate([(p_t * scale).astype(BF16), jnp.where(eye, scale * inter, 0.0).astype(BF16)], axis=0)
        rhs = jnp.concatenate([v[i], qc[i]], axis=0)
        h = lax.dot_general(lhs_t, rhs, (((0,), (0,)), ((), ())), preferred_element_type=F32)
        h_ref[0, :, hh * dh:(hh + 1) * dh] = h.astype(h_ref.dtype)

    @pl.when(c == nc - 1)
    def _():
        cN_ref[0] = c_scr[...]
        nN_ref[0] = n_scr[...]
        mN_ref[0] = m_scr[...]


def _mlstm(z, zblk, gates_t, init, dh):
    b, t, _ = z.shape
    L = MLSTM_CHUNK
    nc = t // L
    width = N_HEADS * dh
    ns = N_DIRS * N_HEADS
    fwd = lambda bi, c: (bi, c, 0)
    bwd = lambda bi, c: (bi, nc - 1 - c, 0)
    st4 = lambda bi, c: (bi, 0, 0, 0)
    in_specs = [pl.BlockSpec((1, L, 3 * width), lambda bi, c: (bi, c, zblk)),
                pl.BlockSpec((1, L, 3 * width), lambda bi, c: (bi, nc - 1 - c, zblk)),
                pl.BlockSpec((1, N_GATE_COLS, L), lambda bi, c: (bi, 0, c)),
                pl.BlockSpec((1, N_GATE_COLS, L), lambda bi, c: (bi, 0, nc - 1 - c))]
    args = [z, z, gates_t, gates_t]
    state_specs = [pl.BlockSpec((1, ns, dh, dh), st4), pl.BlockSpec((1, ns, 8, dh), st4),
                   pl.BlockSpec((1, ns, 1, LANES), st4)]
    state_shapes = [jax.ShapeDtypeStruct((b, ns, dh, dh), F32), jax.ShapeDtypeStruct((b, ns, 8, dh), F32),
                    jax.ShapeDtypeStruct((b, ns, 1, LANES), F32)]
    if init is not None:
        in_specs += state_specs
        args += list(init)
    outs = pl.pallas_call(
        functools.partial(_mlstm_kernel, dh=dh, has_init=init is not None, nc=nc),
        grid=(b, nc),
        in_specs=in_specs,
        out_specs=[pl.BlockSpec((1, L, width), fwd), pl.BlockSpec((1, L, width), bwd)] + state_specs,
        out_shape=[jax.ShapeDtypeStruct((b, t, width), BF16), jax.ShapeDtypeStruct((b, t, width), BF16)] + state_shapes,
        scratch_shapes=[pltpu.VMEM((ns, dh, dh), F32), pltpu.VMEM((ns, 8, dh), F32), pltpu.VMEM((ns, 1, LANES), F32)],
        compiler_params=_params(("arbitrary", "arbitrary")),
        name="mlstm",
    )(*args)
    return outs[0], outs[1], tuple(outs[2:])


def _layer_norm(v, w, b):
    mu = jnp.mean(v, axis=-1, keepdims=True)
    vc = v - mu
    var = jnp.mean(vc * vc, axis=-1, keepdims=True)
    return vc * lax.rsqrt(var + EPS) * w + b


def _gelu_tanh(v):
    return 0.5 * v * (1.0 + jnp.tanh(0.7978845608028654 * (v + 0.044715 * (v * v * v))))


def _mixer_kernel(z_ref, hf_ref, hb_ref, x_ref, g1_ref, mnw_ref, wm_ref, dww_ref, dwb_ref, clw_ref, clb_ref,
                  wc_ref, slw_ref, slb_ref, sgw_ref, sgb_ref, ws_ref, wo_ref, o_ref, pad_scr, conv_scr,
                  *, d, row_len):
    tm = z_ref.shape[1]
    dh = d // N_HEADS
    cc = d // 2
    off_conv, off_sg, off_merge = d, 2 * d, 3 * d

    heads = []
    for hh in range(N_HEADS):
        sl = slice(hh * dh, (hh + 1) * dh)
        hm = hf_ref[0, :, sl].astype(F32) + hb_ref[0, :, sl].astype(F32)
        yn = hm * lax.rsqrt(jnp.mean(hm * hm, axis=-1, keepdims=True) + EPS) * mnw_ref[:, sl]
        og = _sigmoid(z_ref[0, :, sl]).astype(F32)
        heads.append((yn * og).astype(BF16))
    y_m = jnp.dot(jnp.concatenate(heads, axis=1), wm_ref[...], preferred_element_type=F32)

    u = z_ref[0, :, off_conv:off_conv + cc].astype(F32) * _sigmoid(z_ref[0, :, off_conv + cc:off_conv + 2 * cc]).astype(F32)
    n_rows = tm // row_len
    zpad = jnp.zeros((SUBLANES, n_rows, CONV_PAD, cc), F32)
    pad_scr[:, :, 0:CONV_PAD, :] = zpad
    pad_scr[:, :, row_len:row_len + CONV_PAD, :] = zpad
    pad_scr[:, :, row_len + CONV_PAD:row_len + 2 * CONV_PAD, :] = zpad
    for j in range(SUBLANES):
        for r in range(n_rows):
            pad_scr[j, r, CONV_PAD - j:CONV_PAD - j + row_len, :] = u[r * row_len:(r + 1) * row_len, :]
    base = CONV_PAD - DW_CONV_SIZE // 2
    for cb in range(cc // LANES):
        ls = slice(cb * LANES, (cb + 1) * LANES)
        taps = [dww_ref[kk:kk + 1, ls] for kk in range(DW_CONV_SIZE)]
        for r in range(n_rows):
            acc = jnp.zeros((row_len, LANES), F32)
            for kk in range(DW_CONV_SIZE):
                hi, lo = divmod(base + kk, SUBLANES)
                acc += pad_scr[lo, r, hi * SUBLANES:hi * SUBLANES + row_len, ls] * taps[kk]
            conv_scr[r * row_len:(r + 1) * row_len, ls] = acc
    cv = _layer_norm(conv_scr[...] + dwb_ref[...], clw_ref[...], clb_ref[...])
    cv = cv * _sigmoid(cv)
    y_c = jnp.dot(cv.astype(BF16), wc_ref[...], preferred_element_type=F32)

    su = _gelu_tanh(z_ref[0, :, off_sg:off_sg + cc].astype(F32))
    sv = _gelu_tanh(z_ref[0, :, off_sg + cc:off_sg + 2 * cc].astype(F32))
    sv = _layer_norm(sv, slw_ref[...], slb_ref[...]).astype(BF16)
    gd = cc // SG_GROUPS
    for ch in range(tm // SG_CHUNK):
        rs = slice(ch * SG_CHUNK, (ch + 1) * SG_CHUNK)
        parts = []
        for gi in range(SG_GROUPS):
            mixed = jnp.dot(sgw_ref[gi], sv[rs, gi * gd:(gi + 1) * gd], preferred_element_type=F32)
            parts.append(mixed + sgb_ref[:, gi:gi + 1])
        gated = su[rs, :] * jnp.concatenate(parts, axis=1)
        conv_scr[rs, :] = gated
    y_s = jnp.dot(conv_scr[...].astype(BF16), ws_ref[...], preferred_element_type=F32)

    gm0 = _sigmoid(z_ref[0, :, off_merge:off_merge + d]).astype(F32)
    gm1 = _sigmoid(z_ref[0, :, off_merge + d:off_merge + 2 * d]).astype(F32)
    gm2 = _sigmoid(z_ref[0, :, off_merge + 2 * d:off_merge + 3 * d]).astype(F32)
    merged = gm0 * y_m + gm1 * y_c + gm2 * y_s
    y = jnp.dot(merged.astype(BF16), wo_ref[...], preferred_element_type=F32)
    o_ref[0] = x_ref[0] + g1_ref[0] * y


def _mixer_out(z, zblk, hf, hb, x, g1, p, row_len, tm):
    b, t, d = x.shape
    cc = d // 2
    tm = min(tm, t)
    nz = 6 * d
    full = lambda a: pl.BlockSpec(a.shape, lambda bi, i: (0,) * a.ndim)
    row = lambda a: a.reshape(1, -1)
    consts = [row(p["mlstm_norm_w"]), p["w_mlstm_out"], p["conv_dw_w"], row(p["conv_dw_b"]), row(p["conv_ln_w"]),
              row(p["conv_ln_b"]), p["w_conv_out"], row(p["sg_ln_w"]), row(p["sg_ln_b"]), p["sg_w"], p["sg_b"].T,
              p["w_sg_out"], p["w_o"]]
    tok = lambda w: pl.BlockSpec((1, tm, w), lambda bi, i: (bi, i, 0))
    return pl.pallas_call(
        functools.partial(_mixer_kernel, d=d, row_len=row_len),
        grid=(b, t // tm),
        in_specs=[pl.BlockSpec((1, tm, nz), lambda bi, i: (bi, i, zblk)), tok(d), tok(d), tok(d),
                  pl.BlockSpec((1, 1, d), lambda bi, i: (bi, 0, 0))] + [full(a) for a in consts],
        out_specs=tok(d),
        out_shape=jax.ShapeDtypeStruct((b, t, d), F32),
        scratch_shapes=[pltpu.VMEM((SUBLANES, tm // row_len, row_len + 2 * CONV_PAD, cc), F32),
                        pltpu.VMEM((tm, cc), F32)],
        compiler_params=_params(("arbitrary", "arbitrary")),
        name="mixer_out",
    )(z, hf, hb, x, g1, *consts)


def _router_kernel(x_ref, nw_ref, sh_ref, sc_ref, wr_ref, rb_ref, hp_ref, at_ref):
    h = _norm_mod(x_ref[0], nw_ref[...], sh_ref[0], sc_ref[0])
    h_hi, h_lo = _split_bf16(h)
    wr = wr_ref[...]
    raw = jnp.dot(h_hi, wr, preferred_element_type=F32) + jnp.dot(h_lo, wr, preferred_element_type=F32)
    logits = raw[:, :N_EXPERTS] + raw[:, N_EXPERTS:2 * N_EXPERTS] + rb_ref[...]
    mx = jnp.max(logits, axis=-1, keepdims=True)
    ex = jnp.exp(logits - mx)
    aff = ex / jnp.sum(ex, axis=-1, keepdims=True)
    apad = jnp.concatenate([aff, jnp.zeros((aff.shape[0], LANES - N_EXPERTS), F32)], axis=1)
    at_ref[0] = apad.T[:N_EXPERTS, :]
    half = h.shape[1] // 2
    bits = lax.bitcast_convert_type(h_hi.astype(F32), U32)
    hp_ref[0, :, :half] = (bits[:, :half] >> 16) | (bits[:, half:] & jnp.uint32(0xFFFF0000))
    hp_ref[0, :, half:] = lax.bitcast_convert_type(apad, U32)


def _router(x, nw, shift, scale, wr2, rb, tm):
    b, t, d = x.shape
    tm = min(tm, t)
    return pl.pallas_call(
        _router_kernel,
        grid=(b, t // tm),
        in_specs=[pl.BlockSpec((1, tm, d), lambda bi, i: (bi, i, 0)),
                  pl.BlockSpec((1, d), lambda bi, i: (0, 0)),
                  pl.BlockSpec((1, 1, d), lambda bi, i: (bi, 0, 0)),
                  pl.BlockSpec((1, 1, d), lambda bi, i: (bi, 0, 0)),
                  pl.BlockSpec((d, LANES), lambda bi, i: (0, 0)),
                  pl.BlockSpec((1, N_EXPERTS), lambda bi, i: (0, 0))],
        out_specs=[pl.BlockSpec((1, tm, d // 2 + LANES), lambda bi, i: (bi, i, 0)),
                   pl.BlockSpec((1, N_EXPERTS, tm), lambda bi, i: (bi, 0, i))],
        out_shape=[jax.ShapeDtypeStruct((b, t, d // 2 + LANES), U32), jax.ShapeDtypeStruct((b, N_EXPERTS, t), F32)],
        compiler_params=_params(("arbitrary", "arbitrary")),
        name="router",
    )(x, nw.reshape(1, d), shift, scale, wr2, rb.reshape(1, N_EXPERTS))


def _prefix_count(mask, tri):
    e, t = mask.shape
    nb = t // LANES
    stacked = jnp.concatenate([mask[:, c * LANES:(c + 1) * LANES] for c in range(nb)], axis=0).astype(BF16)
    local = jnp.dot(stacked, tri, preferred_element_type=F32)
    out = []
    off = jnp.zeros((e, 1), F32)
    for c in range(nb):
        blk = local[c * e:(c + 1) * e, :]
        out.append(blk + off)
        off = off + blk[:, LANES - 1:LANES]
    return jnp.concatenate(out, axis=1)


def _select_kernel(at_ref, idx_ref, cnt_scr, icol_scr, *, cap, jt):
    a = at_ref[0]
    e, t = a.shape
    bits = lax.bitcast_convert_type(a, I32)

    def search(i, v):
        cand = v | jnp.left_shift(jnp.int32(1), 30 - i)
        cnt = jnp.sum((bits >= cand).astype(I32), axis=1, keepdims=True)
        return jnp.where(cnt >= cap, cand, v)

    thr = lax.fori_loop(0, 31, search, jnp.zeros((e, 1), I32))
    r = lax.broadcasted_iota(I32, (LANES, LANES), 0)
    s = lax.broadcasted_iota(I32, (LANES, LANES), 1)
    tri = (r <= s).astype(BF16)
    gt = bits > thr
    eq = bits == thr
    need = (cap - jnp.sum(gt.astype(I32), axis=1, keepdims=True)).astype(F32)
    sel = gt | (eq & (_prefix_count(eq.astype(F32), tri) <= need))
    cnt = _prefix_count(sel.astype(F32), tri)
    for ei in range(e):
        cnt_scr[ei] = cnt[ei:ei + 1, :]
    icol_scr[...] = jnp.zeros_like(icol_scr)
    lane = lax.broadcasted_iota(I32, (jt, LANES), 1)

    def per_expert(ei, carry):
        for jb in range(cap // jt):
            jcol = (lax.broadcasted_iota(I32, (jt, 1), 0) + jb * jt).astype(F32)
            acc_n = jnp.zeros((jt, LANES), F32)
            for tb in range(t // LANES):
                cn = cnt_scr[ei, :, tb * LANES:(tb + 1) * LANES]
                acc_n += jnp.where(cn <= jcol, 1.0, 0.0)
            tok = jnp.sum(acc_n, axis=1, keepdims=True)
            rows = slice(jb * jt, (jb + 1) * jt)
            icol_scr[rows, :] = jnp.where(lane == ei, tok, icol_scr[rows, :])
        return carry

    lax.fori_loop(0, e, per_expert, 0)
    idx_ref[0, :, 0, :] = icol_scr[...].T[:e, :cap].astype(I32)


def _route_select(aff_t, cap):
    b, e, t = aff_t.shape
    jt = min(cap, 64)
    return pl.pallas_call(
        functools.partial(_select_kernel, cap=cap, jt=jt),
        grid=(b,),
        in_specs=[pl.BlockSpec((1, e, t), lambda bi: (bi, 0, 0))],
        out_specs=pl.BlockSpec((1, e, 1, cap), lambda bi: (bi, 0, 0, 0)),
        out_shape=jax.ShapeDtypeStruct((b, e, 1, cap), I32),
        scratch_shapes=[pltpu.VMEM((e, 1, t), F32), pltpu.VMEM((max(cap, LANES), LANES), F32)],
        compiler_params=_params(("arbitrary",)),
        name="route_select",
    )(aff_t)


def _moe_kernel(idxp_ref, idxn_ref, hp_ref, g2_ref, wg_ref, wu_ref, wd_ref, fnw_ref, x_hbm, o_hbm,
                xe32_scr, xe_scr, gate_scr, y_scr, yg_scr, acc_scr, sem_in, sem_out,
                *, gsz, cap, t, n_groups, n_f, final_norm):
    grp = pl.program_id(0)
    e = pl.program_id(1)
    f = pl.program_id(2)
    n_e = pl.num_programs(1)
    half = hp_ref.shape[2] - LANES
    q_rows = cap // n_f

    def in_copy(gi):
        return pltpu.make_async_copy(x_hbm.at[pl.ds(gi * gsz, gsz)], acc_scr.at[:, pl.ds(0, t)], sem_in)

    def out_copy(gi):
        return pltpu.make_async_copy(acc_scr.at[:, pl.ds(0, t)], o_hbm.at[pl.ds(gi * gsz, gsz)], sem_out)

    def unpack(expert):
        p = xe32_scr[:, :half]
        xe_scr[:, :half] = lax.bitcast_convert_type(p << 16, F32).astype(BF16)
        xe_scr[:, half:] = lax.bitcast_convert_type(p & jnp.uint32(0xFFFF0000), F32).astype(BF16)
        aff = lax.bitcast_convert_type(xe32_scr[:, half:], F32)
        lane = lax.broadcasted_iota(I32, aff.shape, 1)
        gate = jnp.sum(jnp.where(lane == expert, aff, 0.0), axis=1, keepdims=True)
        gate_scr[...] = jnp.broadcast_to(gate, gate_scr.shape)

    def gated():
        for si in range(gsz):
            rs = slice(si * cap, (si + 1) * cap)
            g2 = g2_ref[si]
            for cbk in range(y_scr.shape[1] // LANES):
                ls = slice(cbk * LANES, (cbk + 1) * LANES)
                yg_scr[rs, ls] = y_scr[rs, ls] * gate_scr[rs, :] * g2[:, ls]

    @pl.when((e == 0) & (f == 0))
    def _():
        @pl.when(grp > 0)
        def _():
            out_copy(grp - 1).wait()
        in_copy(grp).start()
        for si in range(gsz):
            def gather(j, carry):
                tkn = idxp_ref[si, 0, 0, j]
                xe32_scr[pl.ds(si * cap + j, 1), :] = hp_ref[si, pl.ds(tkn, 1), :]
                return carry
            lax.fori_loop(0, cap, gather, 0, unroll=8)
        unpack(e)
        y_scr[...] = jnp.zeros_like(y_scr)
        yg_scr[...] = jnp.zeros_like(yg_scr)
        acc_scr[:, t:t + 8, :] = jnp.zeros((gsz, 8, acc_scr.shape[2]), F32)

    @pl.when((e > 0) & (f == 0))
    def _():
        gated()
        unpack(e)

    @pl.when((e == 1) & (f == 0))
    def _():
        in_copy(grp).wait()

    xe = xe_scr[...]
    a = jnp.dot(xe, wg_ref[0], preferred_element_type=F32)
    u = jnp.dot(xe, wu_ref[0], preferred_element_type=F32)
    hid = (a * _sigmoid(a) * u).astype(BF16)
    part = jnp.dot(hid, wd_ref[0].astype(BF16), preferred_element_type=F32)
    spare = jnp.where(e == 0, 1, 0)
    for si in range(gsz):
        for jj in range(q_rows):
            j = f * q_rows + jj
            tkn = idxn_ref[si, 0, 0, j]
            xe32_scr[pl.ds(si * cap + j, 1), :] = hp_ref[si, pl.ds(tkn, 1), :]
        for jj in range(q_rows):
            j = f * q_rows + jj
            tkn = jnp.where(spare == 1, t, idxp_ref[si, 0, 0, j])
            acc_scr[si, pl.ds(tkn, 1), :] += yg_scr[pl.ds(si * cap + j, 1), :]

    y_scr[...] = jnp.where(f > 0, y_scr[...], 0.0) + part

    @pl.when((e == n_e - 1) & (f == n_f - 1))
    def _():
        gated()
        for si in range(gsz):
            def scatter(j, carry):
                tkn = idxn_ref[si, 0, 0, j]
                acc_scr[si, pl.ds(tkn, 1), :] += yg_scr[pl.ds(si * cap + j, 1), :]
                return carry
            lax.fori_loop(0, cap, scatter, 0, unroll=8)
        if final_norm:
            def norm_rows(r, carry):
                rows = pl.ds(pl.multiple_of(r * NORM_ROWS, NORM_ROWS), NORM_ROWS)
                for si in range(gsz):
                    blk = acc_scr[si, rows, :]
                    inv = lax.rsqrt(jnp.mean(blk * blk, axis=-1, keepdims=True) + EPS)
                    acc_scr[si, rows, :] = blk * inv * fnw_ref[...]
                return carry
            lax.fori_loop(0, t // NORM_ROWS, norm_rows, 0)
        out_copy(grp).start()

        @pl.when(grp == n_groups - 1)
        def _():
            out_copy(grp).wait()


def _moe_ffn(idx, hp, g2, x, wg, wu, wd, fnw, layer, gsz, tf, final_norm):
    b, t, d = x.shape
    e, cap = idx.shape[1], idx.shape[3]
    ff = wg.shape[3]
    n_groups, n_f = b // gsz, ff // tf
    hw = hp.shape[2]
    smem_idx = lambda fn: pl.BlockSpec((gsz, 1, 1, cap), fn, memory_space=pltpu.SMEM)
    return pl.pallas_call(
        functools.partial(_moe_kernel, gsz=gsz, cap=cap, t=t, n_groups=n_groups, n_f=n_f, final_norm=final_norm),
        grid=(n_groups, e, n_f),
        in_specs=[smem_idx(lambda gi, ei, fi: (gi, jnp.maximum(ei - 1, 0), 0, 0)),
                  smem_idx(lambda gi, ei, fi: (gi, jnp.minimum(ei + 1, e - 1), 0, 0)),
                  pl.BlockSpec((gsz, t, hw), lambda gi, ei, fi: (gi, 0, 0), pipeline_mode=pl.Buffered(1)),
                  pl.BlockSpec((gsz, 1, d), lambda gi, ei, fi: (gi, 0, 0)),
                  pl.BlockSpec((None, 1, d, tf), lambda gi, ei, fi: (layer, ei, 0, fi)),
                  pl.BlockSpec((None, 1, d, tf), lambda gi, ei, fi: (layer, ei, 0, fi)),
                  pl.BlockSpec((None, 1, tf, d), lambda gi, ei, fi: (layer, ei, fi, 0)),
                  pl.BlockSpec((1, d), lambda gi, ei, fi: (0, 0)),
                  pl.BlockSpec(memory_space=pl.ANY)],
        out_specs=pl.BlockSpec(memory_space=pl.ANY),
        out_shape=jax.ShapeDtypeStruct((b, t, d), F32),
        scratch_shapes=[pltpu.VMEM((gsz * cap, hw), U32), pltpu.VMEM((gsz * cap, d), BF16),
                        pltpu.VMEM((gsz * cap, LANES), F32),
                        pltpu.VMEM((gsz * cap, d), F32), pltpu.VMEM((gsz * cap, d), F32),
                        pltpu.VMEM((gsz, t + 8, d), F32),
                        pltpu.SemaphoreType.DMA(()), pltpu.SemaphoreType.DMA(())],
        compiler_params=_params(("arbitrary", "arbitrary", "arbitrary")),
        name="moe_ffn",
    )(idx, idx, hp, g2, wg, wu, wd, fnw.reshape(1, d), x)


def _stack_hi_lo(w):
    hi, lo = _split_bf16(w)
    pad = jnp.zeros((w.shape[0], LANES - 2 * w.shape[1]), BF16)
    return jnp.concatenate([hi, lo, pad], axis=1)


def _moe(x, nw, shift, scale, g2, wr2, rb, wg, wu, wd, fnw, layer, gsz, tm, final_norm=False):
    t = x.shape[1]
    cap = CAPACITY_FACTOR * t // N_EXPERTS
    hp, aff_t = _router(x, nw, shift, scale, wr2, rb, tm)
    idx = _route_select(aff_t, cap)
    return _moe_ffn(idx, hp, g2, x, wg, wu, wd, fnw, layer, gsz, tf=1024, final_norm=final_norm)


def kernel(x, c, ctx, c_ctx, ada_w, ada_b, norm1_w, norm2_w, w_in, mlstm_gate_b, mlstm_norm_w, w_mlstm_out,
           conv_dw_w, conv_dw_b, conv_ln_w, conv_ln_b, w_conv_out, sg_ln_w, sg_ln_b, sg_w, sg_b, w_sg_out, w_o,
           router_w, router_b, expert_w_gate, expert_w_up, expert_w_down, final_norm_w):
    depth = ada_w.shape[0]
    b, t, d = x.shape
    dh = d // N_HEADS
    n_state = 3 * d + N_GATE_COLS
    cond = jnp.concatenate([c, c_ctx[None, :], jnp.zeros((16 - b - 1, d), F32)], axis=0)
    wg, wu, wd = expert_w_gate.astype(BF16), expert_w_up.astype(BF16), expert_w_down
    tc = ctx.shape[1]

    for layer in range(depth):
        need_ctx = layer < depth - 1
        mod = _modulation(cond, ada_w, ada_b, layer)
        lat = [m[:, None, :] for m in jnp.split(mod[:b], 6, axis=-1)]
        cx = [jnp.broadcast_to(m[None], (b, 1, d)) for m in jnp.split(mod[b:b + 1], 6, axis=-1)]

        wl = w_in[layer]
        w_qkv = wl[:, :3 * d].astype(BF16)
        w_main = jnp.concatenate([wl[:, n_state:].astype(BF16), w_qkv], axis=1)
        w_gate2 = _stack_hi_lo(wl[:, 3 * d:n_state])
        z_lat, gt_lat = _in_proj(x, norm1_w[layer], lat[0], lat[1], w_main, w_gate2, mlstm_gate_b[layer],
                                 tm=1024, tn=1536)
        z_ctx, gt_ctx = _in_proj(ctx.reshape(1, b * tc, d), norm1_w[layer], cx[0][:1], cx[1][:1],
                                 w_main if need_ctx else w_qkv, w_gate2, mlstm_gate_b[layer], tm=1024, tn=1536)
        z_ctx = z_ctx.reshape(b, tc, z_ctx.shape[2])
        gt_ctx = gt_ctx.reshape(N_GATE_COLS, b, tc).transpose(1, 0, 2)

        hcf, hcb, state = _mlstm(z_ctx, 2 if need_ctx else 0, gt_ctx, None, dh)
        hlf, hlb, _ = _mlstm(z_lat, 2, gt_lat, state, dh)

        bf = lambda a: a[layer].astype(BF16)
        p = dict(mlstm_norm_w=mlstm_norm_w[layer], w_mlstm_out=bf(w_mlstm_out), conv_dw_w=conv_dw_w[layer],
                 conv_dw_b=conv_dw_b[layer], conv_ln_w=conv_ln_w[layer], conv_ln_b=conv_ln_b[layer],
                 w_conv_out=bf(w_conv_out), sg_ln_w=sg_ln_w[layer], sg_ln_b=sg_ln_b[layer], sg_w=bf(sg_w),
                 sg_b=sg_b[layer], w_sg_out=bf(w_sg_out), w_o=bf(w_o))
        x = _mixer_out(z_lat, 0, hlf, hlb, x, lat[2], p, row_len=GRID_W, tm=256)
        if need_ctx:
            ctx = _mixer_out(z_ctx, 0, hcf, hcb, ctx, cx[2], p, row_len=ctx.shape[1], tm=256)

        wr2 = _stack_hi_lo(router_w[layer])
        x = _moe(x, norm2_w[layer], lat[3], lat[4], lat[5], wr2, router_b[layer], wg, wu, wd, final_norm_w, layer,
                 gsz=1, tm=1024, final_norm=layer == depth - 1)
        if need_ctx:
            ctx = _moe(ctx, norm2_w[layer], cx[3], cx[4], cx[5], wr2, router_b[layer], wg, wu, wd, final_norm_w,
                       layer, gsz=b, tm=256)

    return x
```

```python
import functools

import jax
import jax.numpy as jnp
from jax import lax
from jax.experimental import pallas as pl
from jax.experimental.pallas import tpu as pltpu

F32 = jnp.float32
BF16 = jnp.bfloat16
I32 = jnp.int32
U32 = jnp.uint32

EPS = 1e-6
LANES = 128
SUBLANES = 8
VMEM_LIMIT = 56 * 1024 * 1024

N_HEADS = 4
MLSTM_CHUNK = 128
N_DIRS = 2
N_GATE_COLS = N_DIRS * 2 * N_HEADS
DW_CONV_SIZE = 31
CONV_PAD = 16
SG_GROUPS = 4
SG_CHUNK = 128
N_EXPERTS = 16
CAPACITY_FACTOR = 2
GRID_W = 64
NORM_ROWS = 128


def _params(sem, vmem=VMEM_LIMIT):
    return pltpu.CompilerParams(dimension_semantics=sem, vmem_limit_bytes=vmem)


def _sigmoid(v):
    return 0.5 * jnp.tanh(0.5 * v) + 0.5


def _split_bf16(a):
    hi = a.astype(BF16)
    lo = (a - hi.astype(F32)).astype(BF16)
    return hi, lo


def _split3_bf16(a):
    hi = a.astype(BF16)
    r1 = a - hi.astype(F32)
    mid = r1.astype(BF16)
    lo = (r1 - mid.astype(F32)).astype(BF16)
    return hi, mid, lo


def _norm_mod(x, nw, shift, scale):
    ms = jnp.mean(x * x, axis=-1, keepdims=True)
    y = x * lax.rsqrt(ms + EPS) * nw
    return y * (1.0 + scale) + shift


def _modulation_kernel(a_ref, w_ref, b_ref, o_ref):
    a = a_ref[...]
    a = a * _sigmoid(a)
    a_hi, a_lo = _split_bf16(a)
    w_hi, w_lo = _split_bf16(w_ref[...])
    acc = jnp.dot(a_hi, w_hi, preferred_element_type=F32)
    acc += jnp.dot(a_hi, w_lo, preferred_element_type=F32)
    acc += jnp.dot(a_lo, w_hi, preferred_element_type=F32)
    o_ref[...] = acc + b_ref[...]


def _modulation(cond, w, b, layer, tn=512):
    m, d = cond.shape
    n = w.shape[2]
    return pl.pallas_call(
        _modulation_kernel,
        grid=(n // tn,),
        in_specs=[pl.BlockSpec((m, d), lambda j: (0, 0)),
                  pl.BlockSpec((None, d, tn), lambda j: (layer, 0, j)),
                  pl.BlockSpec((None, 1, tn), lambda j: (layer, 0, j))],
        out_specs=pl.BlockSpec((m, tn), lambda j: (0, j)),
        out_shape=jax.ShapeDtypeStruct((m, n), F32),
        compiler_params=_params(("arbitrary",)),
        name="modulation",
    )(cond, w, b.reshape(b.shape[0], 1, n))


def _in_proj_kernel(x_ref, nw_ref, sh_ref, sc_ref, w_ref, wg_ref, gb_ref, z_ref, gt_ref, h_scr):
    j = pl.program_id(2)

    @pl.when(j == 0)
    def _():
        h = _norm_mod(x_ref[0], nw_ref[...], sh_ref[0], sc_ref[0])
        h_hi, h_lo = _split_bf16(h)
        h_scr[...] = h_hi
        wg = wg_ref[...]
        raw = jnp.dot(h_hi, wg, preferred_element_type=F32) + jnp.dot(h_lo, wg, preferred_element_type=F32)
        raw = raw[:, :N_GATE_COLS] + raw[:, N_GATE_COLS:2 * N_GATE_COLS] + gb_ref[...]
        col = lax.broadcasted_iota(I32, raw.shape, 1)
        is_forget = ((col >> 2) & 1) == 1
        logsig = jnp.minimum(raw, 0.0) - jnp.log(1.0 + jnp.exp(-jnp.abs(raw)))
        g = jnp.where(is_forget, logsig, raw)
        gpad = jnp.concatenate([g, jnp.zeros((g.shape[0], LANES - N_GATE_COLS), F32)], axis=1)
        gt_ref[0] = gpad.T[:N_GATE_COLS, :]

    z_ref[0] = jnp.dot(h_scr[...], w_ref[...], preferred_element_type=F32).astype(BF16)


def _in_proj(x, nw, shift, scale, w_all, layer, col0, n, w_gate2, gate_b, tm, tn):
    b, t, d = x.shape
    tm = min(tm, t)
    jb0 = col0 // tn
    return pl.pallas_call(
        _in_proj_kernel,
        grid=(b, t // tm, n // tn),
        in_specs=[pl.BlockSpec((1, tm, d), lambda bi, i, j: (bi, i, 0)),
                  pl.BlockSpec((1, d), lambda bi, i, j: (0, 0)),
                  pl.BlockSpec((1, 1, d), lambda bi, i, j: (bi, 0, 0)),
                  pl.BlockSpec((1, 1, d), lambda bi, i, j: (bi, 0, 0)),
                  pl.BlockSpec((None, d, tn), lambda bi, i, j: (layer, 0, jb0 + j)),
                  pl.BlockSpec((d, LANES), lambda bi, i, j: (0, 0)),
                  pl.BlockSpec((1, N_GATE_COLS), lambda bi, i, j: (0, 0))],
        out_specs=[pl.BlockSpec((1, tm, tn), lambda bi, i, j: (bi, i, j)),
                   pl.BlockSpec((1, N_GATE_COLS, tm), lambda bi, i, j: (bi, 0, i))],
        out_shape=[jax.ShapeDtypeStruct((b, t, n), BF16),
                   jax.ShapeDtypeStruct((b, N_GATE_COLS, t), F32)],
        scratch_shapes=[pltpu.VMEM((tm, d), BF16)],
        compiler_params=_params(("arbitrary", "arbitrary", "arbitrary")),
        name="in_proj",
    )(x, nw.reshape(1, d), shift, scale, w_all, w_gate2, gate_b.reshape(1, N_GATE_COLS))


def _mlstm_kernel(*refs, dh, has_init, nc):
    if has_init:
        (zf_ref, zb_ref, gtf_ref, gtb_ref, c0_ref, n0_ref, m0_ref,
         hf_ref, hb_ref, cN_ref, nN_ref, mN_ref, c_scr, n_scr, m_scr) = refs
    else:
        zf_ref, zb_ref, gtf_ref, gtb_ref, hf_ref, hb_ref, cN_ref, nN_ref, mN_ref, c_scr, n_scr, m_scr = refs
    c = pl.program_id(1)
    L = MLSTM_CHUNK
    width = N_HEADS * dh
    scans = [(d, hh) for d in range(N_DIRS) for hh in range(N_HEADS)]
    n_s = len(scans)

    @pl.when(c == 0)
    def _():
        if has_init:
            c_scr[...] = c0_ref[0]
            n_scr[...] = n0_ref[0]
            m_scr[...] = m0_ref[0]
        else:
            c_scr[...] = jnp.zeros_like(c_scr)
            n_scr[...] = jnp.zeros_like(n_scr)
            m_scr[...] = jnp.zeros_like(m_scr)

    row = lax.broadcasted_iota(I32, (L, L), 0)
    colm = lax.broadcasted_iota(I32, (L, L), 1)
    eye = row == colm
    eye_bf = eye.astype(BF16)
    eye3 = jnp.concatenate([eye_bf, eye_bf, eye_bf], axis=1)
    keep_t = (row <= colm, row >= colm)
    tri = (keep_t[0].astype(BF16), keep_t[1].astype(BF16))

    gts = [(gtf_ref, gtb_ref)[d][0] for d in range(N_DIRS)]
    cums = [sum(jnp.dot(part, tri[d], preferred_element_type=F32) for part in _split3_bf16(gts[d]))
            for d in range(N_DIRS)]

    q, k, v, m_prev, i_row, b_row, b_last = [], [], [], [], [], [], []
    for si, (d, hh) in enumerate(scans):
        z_ref = (zf_ref, zb_ref)[d]
        q.append(z_ref[0, :, hh * dh:(hh + 1) * dh])
        k.append(z_ref[0, :, width + hh * dh:width + (hh + 1) * dh] * jnp.asarray(dh ** -0.5, BF16))
        v.append(z_ref[0, :, 2 * width + hh * dh:2 * width + (hh + 1) * dh])
        m_prev.append(m_scr[si][:, 0:1])
        ci = d * 2 * N_HEADS + hh
        cf = ci + N_HEADS
        i_row.append(gts[d][ci:ci + 1, :])
        b_row.append(cums[d][cf:cf + 1, :])
        b_last.append(b_row[si][:, L - 1:L] if d == 0 else b_row[si][:, 0:1])

    nt = (((1,), (1,)), ((), ()))
    s_t = [lax.dot_general(k[i], q[i], nt, preferred_element_type=F32) for i in range(n_s)]
    qc = [jnp.dot(q[i], c_scr[i].astype(BF16), preferred_element_type=F32).astype(BF16) for i in range(n_s)]
    qn = [lax.dot_general(n_scr[i].astype(BF16), q[i], nt, preferred_element_type=F32)[0:1, :] for i in range(n_s)]
    cb = []
    for i in range(n_s):
        parts = [jnp.broadcast_to(p, (LANES, L)) for p in _split3_bf16(b_row[i] - i_row[i])]
        cb.append(lax.dot_general(eye3, jnp.concatenate(parts, axis=1), nt, preferred_element_type=F32))

    for i in range(n_s):
        g_row = b_last[i] - b_row[i] + i_row[i]
        mn = jnp.maximum(b_last[i] + m_prev[i], jnp.max(g_row, axis=-1, keepdims=True))
        decay = jnp.exp(b_last[i] + m_prev[i] - mn)
        w_row = jnp.exp(g_row - mn)
        ktw = (k[i].T.astype(F32) * w_row).astype(BF16)
        kv = jnp.dot(ktw, v[i], preferred_element_type=F32)
        wk = jnp.dot(jnp.broadcast_to(w_row, (8, L)).astype(BF16), k[i], preferred_element_type=F32)
        c_new = decay * c_scr[i] + kv
        n_new = decay * n_scr[i] + wk
        c_scr[i] = c_new
        n_scr[i] = n_new
        m_scr[i] = jnp.broadcast_to(mn, (1, LANES))

    for i, (d, hh) in enumerate(scans):
        h_ref = (hf_ref, hb_ref)[d]
        log_d = jnp.where(keep_t[d], b_row[i] - cb[i], -jnp.inf)
        m_inter = b_row[i] + m_prev[i]
        m_t = jnp.maximum(m_inter, jnp.max(log_d, axis=0, keepdims=True))
        p_t = s_t[i] * jnp.exp(log_d - m_t)
        inter = jnp.exp(m_inter - m_t)
        den = jnp.sum(p_t, axis=0, keepdims=True) + inter * qn[i]
        scale = 1.0 / jnp.maximum(jnp.abs(den), jnp.exp(-m_t))
        lhs_t = jnp.concatenate([(p_t * scale).astype(BF16), jnp.where(eye, scale * inter, 0.0).astype(BF16)], axis=0)
        rhs = jnp.concatenate([v[i], qc[i]], axis=0)
        h = lax.dot_general(lhs_t, rhs, (((0,), (0,)), ((), ())), preferred_element_type=F32)
        h_ref[0, :, hh * dh:(hh + 1) * dh] = h.astype(h_ref.dtype)

    @pl.when(c == nc - 1)
    def _():
        cN_ref[0] = c_scr[...]
        nN_ref[0] = n_scr[...]
        mN_ref[0] = m_scr[...]


def _mlstm(z, zblk, gates_t, init, dh):
    b, t, _ = z.shape
    L = MLSTM_CHUNK
    nc = t // L
    width = N_HEADS * dh
    ns = N_DIRS * N_HEADS
    fwd = lambda bi, c: (bi, c, 0)
    bwd = lambda bi, c: (bi, nc - 1 - c, 0)
    st4 = lambda bi, c: (bi, 0, 0, 0)
    in_specs = [pl.BlockSpec((1, L, 3 * width), lambda bi, c: (bi, c, zblk)),
                pl.BlockSpec((1, L, 3 * width), lambda bi, c: (bi, nc - 1 - c, zblk)),
                pl.BlockSpec((1, N_GATE_COLS, L), lambda bi, c: (bi, 0, c)),
                pl.BlockSpec((1, N_GATE_COLS, L), lambda bi, c: (bi, 0, nc - 1 - c))]
    args = [z, z, gates_t, gates_t]
    state_specs = [pl.BlockSpec((1, ns, dh, dh), st4), pl.BlockSpec((1, ns, 8, dh), st4),
                   pl.BlockSpec((1, ns, 1, LANES), st4)]
    state_shapes = [jax.ShapeDtypeStruct((b, ns, dh, dh), F32), jax.ShapeDtypeStruct((b, ns, 8, dh), F32),
                    jax.ShapeDtypeStruct((b, ns, 1, LANES), F32)]
    if init is not None:
        in_specs += state_specs
        args += list(init)
    outs = pl.pallas_call(
        functools.partial(_mlstm_kernel, dh=dh, has_init=init is not None, nc=nc),
        grid=(b, nc),
        in_specs=in_specs,
        out_specs=[pl.BlockSpec((1, L, width), fwd), pl.BlockSpec((1, L, width), bwd)] + state_specs,
        out_shape=[jax.ShapeDtypeStruct((b, t, width), BF16), jax.ShapeDtypeStruct((b, t, width), BF16)] + state_shapes,
        scratch_shapes=[pltpu.VMEM((ns, dh, dh), F32), pltpu.VMEM((ns, 8, dh), F32), pltpu.VMEM((ns, 1, LANES), F32)],
        compiler_params=_params(("arbitrary", "arbitrary")),
        name="mlstm",
    )(*args)
    return outs[0], outs[1], tuple(outs[2:])


def _layer_norm(v, w, b):
    mu = jnp.mean(v, axis=-1, keepdims=True)
    vc = v - mu
    var = jnp.mean(vc * vc, axis=-1, keepdims=True)
    return vc * lax.rsqrt(var + EPS) * w + b


def _gelu_tanh(v):
    return 0.5 * v * (1.0 + jnp.tanh(0.7978845608028654 * (v + 0.044715 * (v * v * v))))


def _mixer_kernel(z_ref, hf_ref, hb_ref, x_ref, g1_ref, mnw_ref, wm_ref, dww_ref, dwb_ref, clw_ref, clb_ref,
                  wc_ref, slw_ref, slb_ref, sgw_ref, sgb_ref, ws_ref, wo_ref, o_ref, pad_scr, conv_scr,
                  *, d, row_len):
    tm = z_ref.shape[1]
    dh = d // N_HEADS
    cc = d // 2
    off_conv, off_sg, off_merge = d, 2 * d, 3 * d

    heads = []
    for hh in range(N_HEADS):
        sl = slice(hh * dh, (hh + 1) * dh)
        hm = hf_ref[0, :, sl].astype(F32) + hb_ref[0, :, sl].astype(F32)
        yn = hm * lax.rsqrt(jnp.mean(hm * hm, axis=-1, keepdims=True) + EPS) * mnw_ref[:, sl]
        heads.append(yn.astype(BF16) * _sigmoid(z_ref[0, :, sl]))
    y_m = jnp.dot(jnp.concatenate(heads, axis=1), wm_ref[...], preferred_element_type=F32)

    u = z_ref[0, :, off_conv:off_conv + cc].astype(F32) * _sigmoid(z_ref[0, :, off_conv + cc:off_conv + 2 * cc]).astype(F32)
    n_rows = tm // row_len
    zpad = jnp.zeros((SUBLANES, n_rows, CONV_PAD, cc), F32)
    pad_scr[:, :, 0:CONV_PAD, :] = zpad
    pad_scr[:, :, row_len:row_len + CONV_PAD, :] = zpad
    pad_scr[:, :, row_len + CONV_PAD:row_len + 2 * CONV_PAD, :] = zpad
    for j in range(SUBLANES):
        for r in range(n_rows):
            pad_scr[j, r, CONV_PAD - j:CONV_PAD - j + row_len, :] = u[r * row_len:(r + 1) * row_len, :]
    base = CONV_PAD - DW_CONV_SIZE // 2
    for cb in range(cc // LANES):
        ls = slice(cb * LANES, (cb + 1) * LANES)
        taps = [dww_ref[kk:kk + 1, ls] for kk in range(DW_CONV_SIZE)]
        for r in range(n_rows):
            acc = jnp.zeros((row_len, LANES), F32)
            for kk in range(DW_CONV_SIZE):
                hi, lo = divmod(base + kk, SUBLANES)
                acc += pad_scr[lo, r, hi * SUBLANES:hi * SUBLANES + row_len, ls] * taps[kk]
            conv_scr[r * row_len:(r + 1) * row_len, ls] = acc
    cv = _layer_norm(conv_scr[...] + dwb_ref[...], clw_ref[...], clb_ref[...])
    cv = cv * _sigmoid(cv)
    y_c = jnp.dot(cv.astype(BF16), wc_ref[...], preferred_element_type=F32)

    su = _gelu_tanh(z_ref[0, :, off_sg:off_sg + cc].astype(F32))
    sv = _gelu_tanh(z_ref[0, :, off_sg + cc:off_sg + 2 * cc].astype(F32))
    sv = _layer_norm(sv, slw_ref[...], slb_ref[...]).astype(BF16)
    gd = cc // SG_GROUPS
    for ch in range(tm // SG_CHUNK):
        rs = slice(ch * SG_CHUNK, (ch + 1) * SG_CHUNK)
        parts = []
        for gi in range(SG_GROUPS):
            mixed = jnp.dot(sgw_ref[gi], sv[rs, gi * gd:(gi + 1) * gd], preferred_element_type=F32)
            parts.append(mixed + sgb_ref[:, gi:gi + 1])
        gated = su[rs, :] * jnp.concatenate(parts, axis=1)
        conv_scr[rs, :] = gated
    y_s = jnp.dot(conv_scr[...].astype(BF16), ws_ref[...], preferred_element_type=F32)

    gm0 = _sigmoid(z_ref[0, :, off_merge:off_merge + d])
    gm1 = _sigmoid(z_ref[0, :, off_merge + d:off_merge + 2 * d])
    gm2 = _sigmoid(z_ref[0, :, off_merge + 2 * d:off_merge + 3 * d])
    merged = gm0 * y_m.astype(BF16) + gm1 * y_c.astype(BF16) + gm2 * y_s.astype(BF16)
    y = jnp.dot(merged, wo_ref[...], preferred_element_type=F32)
    o_ref[0] = x_ref[0] + g1_ref[0] * y


def _mixer_out(z, zblk, hf, hb, x, g1, p, row_len, tm):
    b, t, d = x.shape
    cc = d // 2
    tm = min(tm, t)
    nz = 6 * d
    full = lambda a: pl.BlockSpec(a.shape, lambda bi, i: (0,) * a.ndim)
    row = lambda a: a.reshape(1, -1)
    consts = [row(p["mlstm_norm_w"]), p["w_mlstm_out"], p["conv_dw_w"], row(p["conv_dw_b"]), row(p["conv_ln_w"]),
              row(p["conv_ln_b"]), p["w_conv_out"], row(p["sg_ln_w"]), row(p["sg_ln_b"]), p["sg_w"], p["sg_b"].T,
              p["w_sg_out"], p["w_o"]]
    tok = lambda w: pl.BlockSpec((1, tm, w), lambda bi, i: (bi, i, 0))
    return pl.pallas_call(
        functools.partial(_mixer_kernel, d=d, row_len=row_len),
        grid=(b, t // tm),
        in_specs=[pl.BlockSpec((1, tm, nz), lambda bi, i: (bi, i, zblk)), tok(d), tok(d), tok(d),
                  pl.BlockSpec((1, 1, d), lambda bi, i: (bi, 0, 0))] + [full(a) for a in consts],
        out_specs=tok(d),
        out_shape=jax.ShapeDtypeStruct((b, t, d), F32),
        scratch_shapes=[pltpu.VMEM((SUBLANES, tm // row_len, row_len + 2 * CONV_PAD, cc), F32),
                        pltpu.VMEM((tm, cc), F32)],
        compiler_params=_params(("arbitrary", "arbitrary")),
        name="mixer_out",
    )(z, hf, hb, x, g1, *consts)


def _router_kernel(x_ref, nw_ref, sh_ref, sc_ref, wr_ref, rb_ref, hp_ref, at_ref):
    h = _norm_mod(x_ref[0], nw_ref[...], sh_ref[0], sc_ref[0])
    h_hi, h_lo = _split_bf16(h)
    wr = wr_ref[...]
    raw = jnp.dot(h_hi, wr, preferred_element_type=F32) + jnp.dot(h_lo, wr, preferred_element_type=F32)
    logits = raw[:, :N_EXPERTS] + raw[:, N_EXPERTS:2 * N_EXPERTS] + rb_ref[...]
    mx = jnp.max(logits, axis=-1, keepdims=True)
    ex = jnp.exp(logits - mx)
    aff = ex / jnp.sum(ex, axis=-1, keepdims=True)
    apad = jnp.concatenate([aff, jnp.zeros((aff.shape[0], LANES - N_EXPERTS), F32)], axis=1)
    at_ref[0] = apad.T[:N_EXPERTS, :]
    half = h.shape[1] // 2
    bits = lax.bitcast_convert_type(h_hi.astype(F32), U32)
    hp_ref[0, :, :half] = (bits[:, :half] >> 16) | (bits[:, half:] & jnp.uint32(0xFFFF0000))
    hp_ref[0, :, half:] = lax.bitcast_convert_type(apad, U32)


def _router(x, nw, shift, scale, wr2, rb, tm):
    b, t, d = x.shape
    tm = min(tm, t)
    return pl.pallas_call(
        _router_kernel,
        grid=(b, t // tm),
        in_specs=[pl.BlockSpec((1, tm, d), lambda bi, i: (bi, i, 0)),
                  pl.BlockSpec((1, d), lambda bi, i: (0, 0)),
                  pl.BlockSpec((1, 1, d), lambda bi, i: (bi, 0, 0)),
                  pl.BlockSpec((1, 1, d), lambda bi, i: (bi, 0, 0)),
                  pl.BlockSpec((d, LANES), lambda bi, i: (0, 0)),
                  pl.BlockSpec((1, N_EXPERTS), lambda bi, i: (0, 0))],
        out_specs=[pl.BlockSpec((1, tm, d // 2 + LANES), lambda bi, i: (bi, i, 0)),
                   pl.BlockSpec((1, N_EXPERTS, tm), lambda bi, i: (bi, 0, i))],
        out_shape=[jax.ShapeDtypeStruct((b, t, d // 2 + LANES), U32), jax.ShapeDtypeStruct((b, N_EXPERTS, t), F32)],
        compiler_params=_params(("arbitrary", "arbitrary")),
        name="router",
    )(x, nw.reshape(1, d), shift, scale, wr2, rb.reshape(1, N_EXPERTS))


def _prefix_count(mask, tri):
    e, t = mask.shape
    nb = t // LANES
    stacked = jnp.concatenate([mask[:, c * LANES:(c + 1) * LANES] for c in range(nb)], axis=0).astype(BF16)
    local = jnp.dot(stacked, tri, preferred_element_type=F32)
    out = []
    off = jnp.zeros((e, 1), F32)
    for c in range(nb):
        blk = local[c * e:(c + 1) * e, :]
        out.append(blk + off)
        off = off + blk[:, LANES - 1:LANES]
    return jnp.concatenate(out, axis=1)


def _select_kernel(at_ref, idx_ref, cnt_scr, icol_scr, *, cap, jt):
    a = at_ref[0]
    e, t = a.shape
    bits = lax.bitcast_convert_type(a, I32)

    def search(i, v):
        cand = v | jnp.left_shift(jnp.int32(1), 30 - i)
        cnt = jnp.sum((bits >= cand).astype(I32), axis=1, keepdims=True)
        return jnp.where(cnt >= cap, cand, v)

    thr = lax.fori_loop(0, 31, search, jnp.zeros((e, 1), I32))
    r = lax.broadcasted_iota(I32, (LANES, LANES), 0)
    s = lax.broadcasted_iota(I32, (LANES, LANES), 1)
    tri = (r <= s).astype(BF16)
    gt = bits > thr
    eq = bits == thr
    need = (cap - jnp.sum(gt.astype(I32), axis=1, keepdims=True)).astype(F32)
    sel = gt | (eq & (_prefix_count(eq.astype(F32), tri) <= need))
    cnt = _prefix_count(sel.astype(F32), tri)
    for ei in range(e):
        cnt_scr[ei] = cnt[ei:ei + 1, :]
    icol_scr[...] = jnp.zeros_like(icol_scr)
    lane = lax.broadcasted_iota(I32, (jt, LANES), 1)

    def per_expert(ei, carry):
        for jb in range(cap // jt):
            jcol = (lax.broadcasted_iota(I32, (jt, 1), 0) + jb * jt).astype(F32)
            acc_n = jnp.zeros((jt, LANES), F32)
            for tb in range(t // LANES):
                cn = cnt_scr[ei, :, tb * LANES:(tb + 1) * LANES]
                acc_n += jnp.where(cn <= jcol, 1.0, 0.0)
            tok = jnp.sum(acc_n, axis=1, keepdims=True)
            rows = slice(jb * jt, (jb + 1) * jt)
            icol_scr[rows, :] = jnp.where(lane == ei, tok, icol_scr[rows, :])
        return carry

    lax.fori_loop(0, e, per_expert, 0)
    idx_ref[0, :, 0, :] = icol_scr[...].T[:e, :cap].astype(I32)


def _route_select(aff_t, cap):
    b, e, t = aff_t.shape
    jt = min(cap, 64)
    return pl.pallas_call(
        functools.partial(_select_kernel, cap=cap, jt=jt),
        grid=(b,),
        in_specs=[pl.BlockSpec((1, e, t), lambda bi: (bi, 0, 0))],
        out_specs=pl.BlockSpec((1, e, 1, cap), lambda bi: (bi, 0, 0, 0)),
        out_shape=jax.ShapeDtypeStruct((b, e, 1, cap), I32),
        scratch_shapes=[pltpu.VMEM((e, 1, t), F32), pltpu.VMEM((max(cap, LANES), LANES), F32)],
        compiler_params=_params(("arbitrary",)),
        name="route_select",
    )(aff_t)


def _moe_kernel(idxp_ref, idxn_ref, hp_ref, g2_ref, wg_ref, wu_ref, wd_ref, fnw_ref, x_hbm, o_hbm,
                xe32_scr, xe_scr, gate_scr, y_scr, yg_scr, acc_scr, sem_in, sem_out,
                *, gsz, cap, t, n_groups, n_f, final_norm):
    grp = pl.program_id(0)
    e = pl.program_id(1)
    f = pl.program_id(2)
    n_e = pl.num_programs(1)
    half = hp_ref.shape[2] - LANES
    q_rows = cap // n_f

    def in_copy(gi):
        return pltpu.make_async_copy(x_hbm.at[pl.ds(gi * gsz, gsz)], acc_scr.at[:, pl.ds(0, t)], sem_in)

    def out_copy(gi):
        return pltpu.make_async_copy(acc_scr.at[:, pl.ds(0, t)], o_hbm.at[pl.ds(gi * gsz, gsz)], sem_out)

    def unpack(expert):
        p = xe32_scr[:, :half]
        xe_scr[:, :half] = lax.bitcast_convert_type(p << 16, F32).astype(BF16)
        xe_scr[:, half:] = lax.bitcast_convert_type(p & jnp.uint32(0xFFFF0000), F32).astype(BF16)
        aff = lax.bitcast_convert_type(xe32_scr[:, half:], F32)
        lane = lax.broadcasted_iota(I32, aff.shape, 1)
        gate = jnp.sum(jnp.where(lane == expert, aff, 0.0), axis=1, keepdims=True)
        gate_scr[...] = jnp.broadcast_to(gate, gate_scr.shape)

    def gated():
        for si in range(gsz):
            rs = slice(si * cap, (si + 1) * cap)
            g2 = g2_ref[si]
            for cbk in range(y_scr.shape[1] // LANES):
                ls = slice(cbk * LANES, (cbk + 1) * LANES)
                yg_scr[rs, ls] = y_scr[rs, ls] * gate_scr[rs, :] * g2[:, ls]

    @pl.when((e == 0) & (f == 0))
    def _():
        @pl.when(grp > 0)
        def _():
            out_copy(grp - 1).wait()
        in_copy(grp).start()
        for si in range(gsz):
            def gather(j, carry):
                tkn = idxp_ref[si, 0, 0, j]
                xe32_scr[pl.ds(si * cap + j, 1), :] = hp_ref[si, pl.ds(tkn, 1), :]
                return carry
            lax.fori_loop(0, cap, gather, 0, unroll=8)
        unpack(e)
        y_scr[...] = jnp.zeros_like(y_scr)
        yg_scr[...] = jnp.zeros_like(yg_scr)
        acc_scr[:, t:t + 8, :] = jnp.zeros((gsz, 8, acc_scr.shape[2]), F32)

    @pl.when((e > 0) & (f == 0))
    def _():
        gated()
        unpack(e)

    @pl.when((e == 1) & (f == 0))
    def _():
        in_copy(grp).wait()

    xe = xe_scr[...]
    a = jnp.dot(xe, wg_ref[0], preferred_element_type=F32)
    u = jnp.dot(xe, wu_ref[0], preferred_element_type=F32)
    hid = (a * _sigmoid(a) * u).astype(BF16)
    part = jnp.dot(hid, wd_ref[0].astype(BF16), preferred_element_type=F32)
    spare = jnp.where(e == 0, 1, 0)
    for si in range(gsz):
        for jj in range(q_rows):
            j = f * q_rows + jj
            tkn = idxn_ref[si, 0, 0, j]
            xe32_scr[pl.ds(si * cap + j, 1), :] = hp_ref[si, pl.ds(tkn, 1), :]
        for jj in range(q_rows):
            j = f * q_rows + jj
            tkn = jnp.where(spare == 1, t, idxp_ref[si, 0, 0, j])
            acc_scr[si, pl.ds(tkn, 1), :] += yg_scr[pl.ds(si * cap + j, 1), :]

    y_scr[...] = jnp.where(f > 0, y_scr[...], 0.0) + part

    @pl.when((e == n_e - 1) & (f == n_f - 1))
    def _():
        gated()
        for si in range(gsz):
            def scatter(j, carry):
                tkn = idxn_ref[si, 0, 0, j]
                acc_scr[si, pl.ds(tkn, 1), :] += yg_scr[pl.ds(si * cap + j, 1), :]
                return carry
            lax.fori_loop(0, cap, scatter, 0, unroll=8)
        if final_norm:
            def norm_rows(r, carry):
                rows = pl.ds(pl.multiple_of(r * NORM_ROWS, NORM_ROWS), NORM_ROWS)
                for si in range(gsz):
                    blk = acc_scr[si, rows, :]
                    inv = lax.rsqrt(jnp.mean(blk * blk, axis=-1, keepdims=True) + EPS)
                    acc_scr[si, rows, :] = blk * inv * fnw_ref[...]
                return carry
            lax.fori_loop(0, t // NORM_ROWS, norm_rows, 0)
        out_copy(grp).start()

        @pl.when(grp == n_groups - 1)
        def _():
            out_copy(grp).wait()


def _moe_ffn(idx, hp, g2, x, wg, wu, wd, fnw, layer, gsz, tf, final_norm):
    b, t, d = x.shape
    e, cap = idx.shape[1], idx.shape[3]
    ff = wg.shape[3]
    n_groups, n_f = b // gsz, ff // tf
    hw = hp.shape[2]
    smem_idx = lambda fn: pl.BlockSpec((gsz, 1, 1, cap), fn, memory_space=pltpu.SMEM)
    return pl.pallas_call(
        functools.partial(_moe_kernel, gsz=gsz, cap=cap, t=t, n_groups=n_groups, n_f=n_f, final_norm=final_norm),
        grid=(n_groups, e, n_f),
        in_specs=[smem_idx(lambda gi, ei, fi: (gi, jnp.maximum(ei - 1, 0), 0, 0)),
                  smem_idx(lambda gi, ei, fi: (gi, jnp.minimum(ei + 1, e - 1), 0, 0)),
                  pl.BlockSpec((gsz, t, hw), lambda gi, ei, fi: (gi, 0, 0), pipeline_mode=pl.Buffered(1)),
                  pl.BlockSpec((gsz, 1, d), lambda gi, ei, fi: (gi, 0, 0)),
                  pl.BlockSpec((None, 1, d, tf), lambda gi, ei, fi: (layer, ei, 0, fi)),
                  pl.BlockSpec((None, 1, d, tf), lambda gi, ei, fi: (layer, ei, 0, fi)),
                  pl.BlockSpec((None, 1, tf, d), lambda gi, ei, fi: (layer, ei, fi, 0)),
                  pl.BlockSpec((1, d), lambda gi, ei, fi: (0, 0)),
                  pl.BlockSpec(memory_space=pl.ANY)],
        out_specs=pl.BlockSpec(memory_space=pl.ANY),
        out_shape=jax.ShapeDtypeStruct((b, t, d), F32),
        scratch_shapes=[pltpu.VMEM((gsz * cap, hw), U32), pltpu.VMEM((gsz * cap, d), BF16),
                        pltpu.VMEM((gsz * cap, LANES), F32),
                        pltpu.VMEM((gsz * cap, d), F32), pltpu.VMEM((gsz * cap, d), F32),
                        pltpu.VMEM((gsz, t + 8, d), F32),
                        pltpu.SemaphoreType.DMA(()), pltpu.SemaphoreType.DMA(())],
        compiler_params=_params(("arbitrary", "arbitrary", "arbitrary")),
        name="moe_ffn",
    )(idx, idx, hp, g2, wg, wu, wd, fnw.reshape(1, d), x)


def _stack_hi_lo(w):
    hi, lo = _split_bf16(w)
    pad = jnp.zeros((w.shape[0], LANES - 2 * w.shape[1]), BF16)
    return jnp.concatenate([hi, lo, pad], axis=1)


def _moe(x, nw, shift, scale, g2, wr2, rb, wg, wu, wd, fnw, layer, gsz, tm, final_norm=False):
    t = x.shape[1]
    cap = CAPACITY_FACTOR * t // N_EXPERTS
    hp, aff_t = _router(x, nw, shift, scale, wr2, rb, tm)
    idx = _route_select(aff_t, cap)
    return _moe_ffn(idx, hp, g2, x, wg, wu, wd, fnw, layer, gsz, tf=1024, final_norm=final_norm)


def kernel(x, c, ctx, c_ctx, ada_w, ada_b, norm1_w, norm2_w, w_in, mlstm_gate_b, mlstm_norm_w, w_mlstm_out,
           conv_dw_w, conv_dw_b, conv_ln_w, conv_ln_b, w_conv_out, sg_ln_w, sg_ln_b, sg_w, sg_b, w_sg_out, w_o,
           router_w, router_b, expert_w_gate, expert_w_up, expert_w_down, final_norm_w):
    depth = ada_w.shape[0]
    b, t, d = x.shape
    dh = d // N_HEADS
    n_state = 3 * d + N_GATE_COLS
    cond = jnp.concatenate([c, c_ctx[None, :], jnp.zeros((16 - b - 1, d), F32)], axis=0)
    wg, wu, wd = expert_w_gate.astype(BF16), expert_w_up.astype(BF16), expert_w_down
    tc = ctx.shape[1]
    w_all = jnp.concatenate([w_in[:, :, n_state:].astype(BF16), w_in[:, :, :3 * d].astype(BF16)], axis=2)

    for layer in range(depth):
        need_ctx = layer < depth - 1
        mod = _modulation(cond, ada_w, ada_b, layer)
        lat = [m[:, None, :] for m in jnp.split(mod[:b], 6, axis=-1)]
        cx = [jnp.broadcast_to(m[None], (b, 1, d)) for m in jnp.split(mod[b:b + 1], 6, axis=-1)]

        w_gate2 = _stack_hi_lo(w_in[layer, :, 3 * d:n_state])
        z_lat, gt_lat = _in_proj(x, norm1_w[layer], lat[0], lat[1], w_all, layer, 0, 9 * d, w_gate2,
                                 mlstm_gate_b[layer], tm=2048, tn=1536)
        col0, n_ctx = (0, 9 * d) if need_ctx else (6 * d, 3 * d)
        z_ctx, gt_ctx = _in_proj(ctx.reshape(1, b * tc, d), norm1_w[layer], cx[0][:1], cx[1][:1], w_all, layer,
                                 col0, n_ctx, w_gate2, mlstm_gate_b[layer], tm=2048, tn=1536)
        z_ctx = z_ctx.reshape(b, tc, z_ctx.shape[2])
        gt_ctx = gt_ctx.reshape(N_GATE_COLS, b, tc).transpose(1, 0, 2)

        hcf, hcb, state = _mlstm(z_ctx, 2 if need_ctx else 0, gt_ctx, None, dh)
        hlf, hlb, _ = _mlstm(z_lat, 2, gt_lat, state, dh)

        bf = lambda a: a[layer].astype(BF16)
        p = dict(mlstm_norm_w=mlstm_norm_w[layer], w_mlstm_out=bf(w_mlstm_out), conv_dw_w=conv_dw_w[layer],
                 conv_dw_b=conv_dw_b[layer], conv_ln_w=conv_ln_w[layer], conv_ln_b=conv_ln_b[layer],
                 w_conv_out=bf(w_conv_out), sg_ln_w=sg_ln_w[layer], sg_ln_b=sg_ln_b[layer], sg_w=bf(sg_w),
                 sg_b=sg_b[layer], w_sg_out=bf(w_sg_out), w_o=bf(w_o))
        x = _mixer_out(z_lat, 0, hlf, hlb, x, lat[2], p, row_len=GRID_W, tm=512)
        if need_ctx:
            ctx = _mixer_out(z_ctx, 0, hcf, hcb, ctx, cx[2], p, row_len=ctx.shape[1], tm=256)

        wr2 = _stack_hi_lo(router_w[layer])
        x = _moe(x, norm2_w[layer], lat[3], lat[4], lat[5], wr2, router_b[layer], wg, wu, wd, final_norm_w, layer,
                 gsz=1, tm=1024, final_norm=layer == depth - 1)
        if need_ctx:
            ctx = _moe(ctx, norm2_w[layer], cx[3], cx[4], cx[5], wr2, router_b[layer], wg, wu, wd, final_norm_w,
                       layer, gsz=b, tm=256)

    return x
```

```python
import functools

import jax
import jax.numpy as jnp
from jax import lax
from jax.experimental import pallas as pl
from jax.experimental.pallas import tpu as pltpu

F32 = jnp.float32
BF16 = jnp.bfloat16
I32 = jnp.int32
U32 = jnp.uint32

EPS = 1e-6
LANES = 128
SUBLANES = 8
VMEM_LIMIT = 56 * 1024 * 1024

N_HEADS = 4
MLSTM_CHUNK = 128
N_DIRS = 2
N_GATE_COLS = N_DIRS * 2 * N_HEADS
DW_CONV_SIZE = 31
CONV_PAD = 16
SG_GROUPS = 4
SG_CHUNK = 128
N_EXPERTS = 16
CAPACITY_FACTOR = 2
GRID_W = 64
NORM_ROWS = 128


def _params(sem, vmem=VMEM_LIMIT):
    return pltpu.CompilerParams(dimension_semantics=sem, vmem_limit_bytes=vmem)


def _sigmoid(v):
    return 0.5 * jnp.tanh(0.5 * v) + 0.5


def _split_bf16(a):
    hi = a.astype(BF16)
    lo = (a - hi.astype(F32)).astype(BF16)
    return hi, lo


def _split3_bf16(a):
    hi = a.astype(BF16)
    r1 = a - hi.astype(F32)
    mid = r1.astype(BF16)
    lo = (r1 - mid.astype(F32)).astype(BF16)
    return hi, mid, lo


def _norm_mod(x, nw, shift, scale):
    ms = jnp.mean(x * x, axis=-1, keepdims=True)
    y = x * lax.rsqrt(ms + EPS) * nw
    return y * (1.0 + scale) + shift


def _modulation_kernel(a_ref, w_ref, b_ref, o_ref):
    a = a_ref[...]
    a = a * _sigmoid(a)
    a_hi, a_lo = _split_bf16(a)
    w_hi, w_lo = _split_bf16(w_ref[...])
    acc = jnp.dot(a_hi, w_hi, preferred_element_type=F32)
    acc += jnp.dot(a_hi, w_lo, preferred_element_type=F32)
    acc += jnp.dot(a_lo, w_hi, preferred_element_type=F32)
    o_ref[...] = acc + b_ref[...]


def _modulation(cond, w, b, layer, tn=1536):
    m, d = cond.shape
    n = w.shape[2]
    return pl.pallas_call(
        _modulation_kernel,
        grid=(n // tn,),
        in_specs=[pl.BlockSpec((m, d), lambda j: (0, 0)),
                  pl.BlockSpec((None, d, tn), lambda j: (layer, 0, j)),
                  pl.BlockSpec((None, 1, tn), lambda j: (layer, 0, j))],
        out_specs=pl.BlockSpec((m, tn), lambda j: (0, j)),
        out_shape=jax.ShapeDtypeStruct((m, n), F32),
        compiler_params=_params(("arbitrary",)),
        name="modulation",
    )(cond, w, b.reshape(b.shape[0], 1, n))


def _w_regroup_kernel(w_ref, o_ref, *, n_qkv, n_state):
    n_rest = w_ref.shape[1] - n_state
    o_ref[:, :n_rest] = w_ref[:, n_state:].astype(BF16)
    o_ref[:, n_rest:] = w_ref[:, :n_qkv].astype(BF16)


def _w_regroup(w_in, n_qkv, n_state, tr=256):
    nl, d, n_in = w_in.shape
    n_out = n_in - (n_state - n_qkv)
    return pl.pallas_call(
        functools.partial(_w_regroup_kernel, n_qkv=n_qkv, n_state=n_state),
        grid=(nl, d // tr),
        in_specs=[pl.BlockSpec((None, tr, n_in), lambda li, i: (li, i, 0))],
        out_specs=pl.BlockSpec((None, tr, n_out), lambda li, i: (li, i, 0)),
        out_shape=jax.ShapeDtypeStruct((nl, d, n_out), BF16),
        compiler_params=_params(("arbitrary", "arbitrary")),
        name="w_regroup",
    )(w_in)


def _in_proj_kernel(x_ref, nw_ref, sh_ref, sc_ref, w_ref, wg_ref, gb_ref, z_ref, gt_ref, h_scr):
    j = pl.program_id(2)

    @pl.when(j == 0)
    def _():
        h = _norm_mod(x_ref[0], nw_ref[...], sh_ref[0], sc_ref[0])
        h_hi, h_lo = _split_bf16(h)
        h_scr[...] = h_hi
        wg = wg_ref[...]
        raw = jnp.dot(h_hi, wg, preferred_element_type=F32) + jnp.dot(h_lo, wg, preferred_element_type=F32)
        raw = raw[:, :N_GATE_COLS] + raw[:, N_GATE_COLS:2 * N_GATE_COLS] + gb_ref[...]
        col = lax.broadcasted_iota(I32, raw.shape, 1)
        is_forget = ((col >> 2) & 1) == 1
        logsig = jnp.minimum(raw, 0.0) - jnp.log(1.0 + jnp.exp(-jnp.abs(raw)))
        g = jnp.where(is_forget, logsig, raw)
        gpad = jnp.concatenate([g, jnp.zeros((g.shape[0], LANES - N_GATE_COLS), F32)], axis=1)
        gt_ref[0] = gpad.T[:N_GATE_COLS, :]

    z_ref[0] = jnp.dot(h_scr[...], w_ref[...], preferred_element_type=F32).astype(BF16)


def _in_proj(x, nw, shift, scale, w_all, layer, col0, n, w_gate2, gate_b, tm, tn):
    b, t, d = x.shape
    tm = min(tm, t)
    jb0 = col0 // tn
    return pl.pallas_call(
        _in_proj_kernel,
        grid=(b, t // tm, n // tn),
        in_specs=[pl.BlockSpec((1, tm, d), lambda bi, i, j: (bi, i, 0)),
                  pl.BlockSpec((1, d), lambda bi, i, j: (0, 0)),
                  pl.BlockSpec((1, 1, d), lambda bi, i, j: (bi, 0, 0)),
                  pl.BlockSpec((1, 1, d), lambda bi, i, j: (bi, 0, 0)),
                  pl.BlockSpec((None, d, tn), lambda bi, i, j: (layer, 0, jb0 + j)),
                  pl.BlockSpec((d, LANES), lambda bi, i, j: (0, 0)),
                  pl.BlockSpec((1, N_GATE_COLS), lambda bi, i, j: (0, 0))],
        out_specs=[pl.BlockSpec((1, tm, tn), lambda bi, i, j: (bi, i, j)),
                   pl.BlockSpec((1, N_GATE_COLS, tm), lambda bi, i, j: (bi, 0, i))],
        out_shape=[jax.ShapeDtypeStruct((b, t, n), BF16),
                   jax.ShapeDtypeStruct((b, N_GATE_COLS, t), F32)],
        scratch_shapes=[pltpu.VMEM((tm, d), BF16)],
        compiler_params=_params(("arbitrary", "arbitrary", "arbitrary")),
        name="in_proj",
    )(x, nw.reshape(1, d), shift, scale, w_all, w_gate2, gate_b.reshape(1, N_GATE_COLS))


def _mlstm_chunk(zf_ref, zb_ref, gtf_ref, gtb_ref, hf_ref, hb_ref, c_scr, n_scr, m_scr, r0, dh):
    L = MLSTM_CHUNK
    width = N_HEADS * dh
    scans = [(d, hh) for d in range(N_DIRS) for hh in range(N_HEADS)]
    n_s = len(scans)
    rows = [slice(r, r + L) for r in r0]

    row = lax.broadcasted_iota(I32, (L, L), 0)
    colm = lax.broadcasted_iota(I32, (L, L), 1)
    eye = row == colm
    eye_bf = eye.astype(BF16)
    eye3 = jnp.concatenate([eye_bf, eye_bf, eye_bf], axis=1)
    keep_t = (row <= colm, row >= colm)
    tri = (keep_t[0].astype(BF16), keep_t[1].astype(BF16))

    gts = [(gtf_ref, gtb_ref)[d][0, :, rows[d]] for d in range(N_DIRS)]
    cums = [sum(jnp.dot(part, tri[d], preferred_element_type=F32) for part in _split3_bf16(gts[d]))
            for d in range(N_DIRS)]

    q, k, v, m_prev, i_row, b_row, b_last = [], [], [], [], [], [], []
    for si, (d, hh) in enumerate(scans):
        z_ref = (zf_ref, zb_ref)[d]
        q.append(z_ref[0, rows[d], hh * dh:(hh + 1) * dh])
        k.append(z_ref[0, rows[d], width + hh * dh:width + (hh + 1) * dh] * jnp.asarray(dh ** -0.5, BF16))
        v.append(z_ref[0, rows[d], 2 * width + hh * dh:2 * width + (hh + 1) * dh])
        m_prev.append(m_scr[si][:, 0:1])
        ci = d * 2 * N_HEADS + hh
        cf = ci + N_HEADS
        i_row.append(gts[d][ci:ci + 1, :])
        b_row.append(cums[d][cf:cf + 1, :])
        b_last.append(b_row[si][:, L - 1:L] if d == 0 else b_row[si][:, 0:1])

    nt = (((1,), (1,)), ((), ()))
    s_t = [lax.dot_general(k[i], q[i], nt, preferred_element_type=F32) for i in range(n_s)]
    qc = [jnp.dot(q[i], c_scr[i].astype(BF16), preferred_element_type=F32).astype(BF16) for i in range(n_s)]
    qn = [lax.dot_general(n_scr[i].astype(BF16), q[i], nt, preferred_element_type=F32)[0:1, :] for i in range(n_s)]
    cb = []
    for i in range(n_s):
        parts = [jnp.broadcast_to(p, (LANES, L)) for p in _split3_bf16(b_row[i] - i_row[i])]
        cb.append(lax.dot_general(eye3, jnp.concatenate(parts, axis=1), nt, preferred_element_type=F32))

    for i in range(n_s):
        g_row = b_last[i] - b_row[i] + i_row[i]
        mn = jnp.maximum(b_last[i] + m_prev[i], jnp.max(g_row, axis=-1, keepdims=True))
        decay = jnp.exp(b_last[i] + m_prev[i] - mn)
        w_row = jnp.exp(g_row - mn)
        ktw = (k[i].T.astype(F32) * w_row).astype(BF16)
        kv = jnp.dot(ktw, v[i], preferred_element_type=F32)
        wk = jnp.dot(jnp.broadcast_to(w_row, (8, L)).astype(BF16), k[i], preferred_element_type=F32)
        c_new = decay * c_scr[i] + kv
        n_new = decay * n_scr[i] + wk
        c_scr[i] = c_new
        n_scr[i] = n_new
        m_scr[i] = jnp.broadcast_to(mn, (1, LANES))

    for i, (d, hh) in enumerate(scans):
        h_ref = (hf_ref, hb_ref)[d]
        log_d = jnp.where(keep_t[d], b_row[i] - cb[i], -jnp.inf)
        m_inter = b_row[i] + m_prev[i]
        m_t = jnp.maximum(m_inter, jnp.max(log_d, axis=0, keepdims=True))
        p_t = s_t[i] * jnp.exp(log_d - m_t)
        inter = jnp.exp(m_inter - m_t)
        den = jnp.sum(p_t, axis=0, keepdims=True) + inter * qn[i]
        scale = 1.0 / jnp.maximum(jnp.abs(den), jnp.exp(-m_t))
        lhs_t = jnp.concatenate([(p_t * scale).astype(BF16), jnp.where(eye, scale * inter, 0.0).astype(BF16)], axis=0)
        rhs = jnp.concatenate([v[i], qc[i]], axis=0)
        h = lax.dot_general(lhs_t, rhs, (((0,), (0,)), ((), ())), preferred_element_type=F32)
        h_ref[0, rows[d], hh * dh:(hh + 1) * dh] = h.astype(h_ref.dtype)


def _mlstm_kernel(*refs, dh, has_init, n_steps, ch):
    if has_init:
        (zf_ref, zb_ref, gtf_ref, gtb_ref, c0_ref, n0_ref, m0_ref,
         hf_ref, hb_ref, cN_ref, nN_ref, mN_ref, c_scr, n_scr, m_scr) = refs
    else:
        zf_ref, zb_ref, gtf_ref, gtb_ref, hf_ref, hb_ref, cN_ref, nN_ref, mN_ref, c_scr, n_scr, m_scr = refs
    c = pl.program_id(1)

    @pl.when(c == 0)
    def _():
        if has_init:
            c_scr[...] = c0_ref[0]
            n_scr[...] = n0_ref[0]
            m_scr[...] = m0_ref[0]
        else:
            c_scr[...] = jnp.zeros_like(c_scr)
            n_scr[...] = jnp.zeros_like(n_scr)
            m_scr[...] = jnp.zeros_like(m_scr)

    for sc in range(ch):
        r0 = (sc * MLSTM_CHUNK, (ch - 1 - sc) * MLSTM_CHUNK)
        _mlstm_chunk(zf_ref, zb_ref, gtf_ref, gtb_ref, hf_ref, hb_ref, c_scr, n_scr, m_scr, r0, dh)

    @pl.when(c == n_steps - 1)
    def _():
        cN_ref[0] = c_scr[...]
        nN_ref[0] = n_scr[...]
        mN_ref[0] = m_scr[...]


def _mlstm(z, zblk, gates_t, init, dh, ch):
    b, t, _ = z.shape
    L = ch * MLSTM_CHUNK
    nc = t // L
    width = N_HEADS * dh
    ns = N_DIRS * N_HEADS
    fwd = lambda bi, c: (bi, c, 0)
    bwd = lambda bi, c: (bi, nc - 1 - c, 0)
    st4 = lambda bi, c: (bi, 0, 0, 0)
    in_specs = [pl.BlockSpec((1, L, 3 * width), lambda bi, c: (bi, c, zblk)),
                pl.BlockSpec((1, L, 3 * width), lambda bi, c: (bi, nc - 1 - c, zblk)),
                pl.BlockSpec((1, N_GATE_COLS, L), lambda bi, c: (bi, 0, c)),
                pl.BlockSpec((1, N_GATE_COLS, L), lambda bi, c: (bi, 0, nc - 1 - c))]
    args = [z, z, gates_t, gates_t]
    state_specs = [pl.BlockSpec((1, ns, dh, dh), st4), pl.BlockSpec((1, ns, 8, dh), st4),
                   pl.BlockSpec((1, ns, 1, LANES), st4)]
    state_shapes = [jax.ShapeDtypeStruct((b, ns, dh, dh), F32), jax.ShapeDtypeStruct((b, ns, 8, dh), F32),
                    jax.ShapeDtypeStruct((b, ns, 1, LANES), F32)]
    if init is not None:
        in_specs += state_specs
        args += list(init)
    outs = pl.pallas_call(
        functools.partial(_mlstm_kernel, dh=dh, has_init=init is not None, n_steps=nc, ch=ch),
        grid=(b, nc),
        in_specs=in_specs,
        out_specs=[pl.BlockSpec((1, L, width), fwd), pl.BlockSpec((1, L, width), bwd)] + state_specs,
        out_shape=[jax.ShapeDtypeStruct((b, t, width), BF16), jax.ShapeDtypeStruct((b, t, width), BF16)] + state_shapes,
        scratch_shapes=[pltpu.VMEM((ns, dh, dh), F32), pltpu.VMEM((ns, 8, dh), F32), pltpu.VMEM((ns, 1, LANES), F32)],
        compiler_params=_params(("arbitrary", "arbitrary")),
        name="mlstm",
    )(*args)
    return outs[0], outs[1], tuple(outs[2:])


def _layer_norm(v, w, b):
    mu = jnp.mean(v, axis=-1, keepdims=True)
    vc = v - mu
    var = jnp.mean(vc * vc, axis=-1, keepdims=True)
    return vc * lax.rsqrt(var + EPS) * w + b


def _gelu_tanh(v):
    return 0.5 * v * (1.0 + jnp.tanh(0.7978845608028654 * (v + 0.044715 * (v * v * v))))


def _mixer_kernel(z_ref, hf_ref, hb_ref, x_ref, g1_ref, mnw_ref, wm_ref, dww_ref, dwb_ref, clw_ref, clb_ref,
                  wc_ref, slw_ref, slb_ref, sgw_ref, sgb_ref, ws_ref, wo_ref, o_ref, pad_scr, conv_scr,
                  *, d, row_len):
    tm = z_ref.shape[1]
    dh = d // N_HEADS
    cc = d // 2
    off_conv, off_sg, off_merge = d, 2 * d, 3 * d

    heads = []
    for hh in range(N_HEADS):
        sl = slice(hh * dh, (hh + 1) * dh)
        hm = hf_ref[0, :, sl].astype(F32) + hb_ref[0, :, sl].astype(F32)
        yn = hm * lax.rsqrt(jnp.mean(hm * hm, axis=-1, keepdims=True) + EPS) * mnw_ref[:, sl]
        heads.append(yn.astype(BF16) * _sigmoid(z_ref[0, :, sl]))
    y_m = jnp.dot(jnp.concatenate(heads, axis=1), wm_ref[...], preferred_element_type=F32)

    u = z_ref[0, :, off_conv:off_conv + cc].astype(F32) * _sigmoid(z_ref[0, :, off_conv + cc:off_conv + 2 * cc]).astype(F32)
    n_rows = tm // row_len
    zpad = jnp.zeros((SUBLANES, n_rows, CONV_PAD, cc), F32)
    pad_scr[:, :, 0:CONV_PAD, :] = zpad
    pad_scr[:, :, row_len:row_len + CONV_PAD, :] = zpad
    pad_scr[:, :, row_len + CONV_PAD:row_len + 2 * CONV_PAD, :] = zpad
    for j in range(SUBLANES):
        for r in range(n_rows):
            pad_scr[j, r, CONV_PAD - j:CONV_PAD - j + row_len, :] = u[r * row_len:(r + 1) * row_len, :]
    base = CONV_PAD - DW_CONV_SIZE // 2
    for cb in range(cc // LANES):
        ls = slice(cb * LANES, (cb + 1) * LANES)
        taps = [dww_ref[kk:kk + 1, ls] for kk in range(DW_CONV_SIZE)]
        for r in range(n_rows):
            acc = jnp.zeros((row_len, LANES), F32)
            for kk in range(DW_CONV_SIZE):
                hi, lo = divmod(base + kk, SUBLANES)
                acc += pad_scr[lo, r, hi * SUBLANES:hi * SUBLANES + row_len, ls] * taps[kk]
            conv_scr[r * row_len:(r + 1) * row_len, ls] = acc
    cv = _layer_norm(conv_scr[...] + dwb_ref[...], clw_ref[...], clb_ref[...])
    cv = cv * _sigmoid(cv)
    y_c = jnp.dot(cv.astype(BF16), wc_ref[...], preferred_element_type=F32)

    su = _gelu_tanh(z_ref[0, :, off_sg:off_sg + cc].astype(F32))
    sv = _gelu_tanh(z_ref[0, :, off_sg + cc:off_sg + 2 * cc].astype(F32))
    sv = _layer_norm(sv, slw_ref[...], slb_ref[...]).astype(BF16)
    gd = cc // SG_GROUPS
    for ch in range(tm // SG_CHUNK):
        rs = slice(ch * SG_CHUNK, (ch + 1) * SG_CHUNK)
        parts = []
        for gi in range(SG_GROUPS):
            mixed = jnp.dot(sgw_ref[gi], sv[rs, gi * gd:(gi + 1) * gd], preferred_element_type=F32)
            parts.append(mixed + sgb_ref[:, gi:gi + 1])
        gated = su[rs, :] * jnp.concatenate(parts, axis=1)
        conv_scr[rs, :] = gated
    y_s = jnp.dot(conv_scr[...].astype(BF16), ws_ref[...], preferred_element_type=F32)

    gm0 = _sigmoid(z_ref[0, :, off_merge:off_merge + d])
    gm1 = _sigmoid(z_ref[0, :, off_merge + d:off_merge + 2 * d])
    gm2 = _sigmoid(z_ref[0, :, off_merge + 2 * d:off_merge + 3 * d])
    merged = gm0 * y_m.astype(BF16) + gm1 * y_c.astype(BF16) + gm2 * y_s.astype(BF16)
    y = jnp.dot(merged, wo_ref[...], preferred_element_type=F32)
    o_ref[0] = x_ref[0] + g1_ref[0] * y


def _mixer_out(z, zblk, hf, hb, x, g1, p, row_len, tm):
    b, t, d = x.shape
    cc = d // 2
    tm = min(tm, t)
    nz = 6 * d
    full = lambda a: pl.BlockSpec(a.shape, lambda bi, i: (0,) * a.ndim)
    row = lambda a: a.reshape(1, -1)
    consts = [row(p["mlstm_norm_w"]), p["w_mlstm_out"], p["conv_dw_w"], row(p["conv_dw_b"]), row(p["conv_ln_w"]),
              row(p["conv_ln_b"]), p["w_conv_out"], row(p["sg_ln_w"]), row(p["sg_ln_b"]), p["sg_w"], p["sg_b"].T,
              p["w_sg_out"], p["w_o"]]
    tok = lambda w: pl.BlockSpec((1, tm, w), lambda bi, i: (bi, i, 0))
    return pl.pallas_call(
        functools.partial(_mixer_kernel, d=d, row_len=row_len),
        grid=(b, t // tm),
        in_specs=[pl.BlockSpec((1, tm, nz), lambda bi, i: (bi, i, zblk)), tok(d), tok(d), tok(d),
                  pl.BlockSpec((1, 1, d), lambda bi, i: (bi, 0, 0))] + [full(a) for a in consts],
        out_specs=tok(d),
        out_shape=jax.ShapeDtypeStruct((b, t, d), F32),
        scratch_shapes=[pltpu.VMEM((SUBLANES, tm // row_len, row_len + 2 * CONV_PAD, cc), F32),
                        pltpu.VMEM((tm, cc), F32)],
        compiler_params=_params(("arbitrary", "arbitrary")),
        name="mixer_out",
    )(z, hf, hb, x, g1, *consts)


def _router_kernel(x_ref, nw_ref, sh_ref, sc_ref, wr_ref, rb_ref, hp_ref, at_ref):
    h = _norm_mod(x_ref[0], nw_ref[...], sh_ref[0], sc_ref[0])
    h_hi, h_lo = _split_bf16(h)
    wr = wr_ref[...]
    raw = jnp.dot(h_hi, wr, preferred_element_type=F32) + jnp.dot(h_lo, wr, preferred_element_type=F32)
    logits = raw[:, :N_EXPERTS] + raw[:, N_EXPERTS:2 * N_EXPERTS] + rb_ref[...]
    mx = jnp.max(logits, axis=-1, keepdims=True)
    ex = jnp.exp(logits - mx)
    aff = ex / jnp.sum(ex, axis=-1, keepdims=True)
    apad = jnp.concatenate([aff, jnp.zeros((aff.shape[0], LANES - N_EXPERTS), F32)], axis=1)
    at_ref[0] = apad.T[:N_EXPERTS, :]
    half = h.shape[1] // 2
    bits = lax.bitcast_convert_type(h_hi.astype(F32), U32)
    hp_ref[0, :, :half] = (bits[:, :half] >> 16) | (bits[:, half:] & jnp.uint32(0xFFFF0000))
    hp_ref[0, :, half:] = lax.bitcast_convert_type(apad, U32)


def _router(x, nw, shift, scale, wr2, rb, tm):
    b, t, d = x.shape
    tm = min(tm, t)
    return pl.pallas_call(
        _router_kernel,
        grid=(b, t // tm),
        in_specs=[pl.BlockSpec((1, tm, d), lambda bi, i: (bi, i, 0)),
                  pl.BlockSpec((1, d), lambda bi, i: (0, 0)),
                  pl.BlockSpec((1, 1, d), lambda bi, i: (bi, 0, 0)),
                  pl.BlockSpec((1, 1, d), lambda bi, i: (bi, 0, 0)),
                  pl.BlockSpec((d, LANES), lambda bi, i: (0, 0)),
                  pl.BlockSpec((1, N_EXPERTS), lambda bi, i: (0, 0))],
        out_specs=[pl.BlockSpec((1, tm, d // 2 + LANES), lambda bi, i: (bi, i, 0)),
                   pl.BlockSpec((1, N_EXPERTS, tm), lambda bi, i: (bi, 0, i))],
        out_shape=[jax.ShapeDtypeStruct((b, t, d // 2 + LANES), U32), jax.ShapeDtypeStruct((b, N_EXPERTS, t), F32)],
        compiler_params=_params(("arbitrary", "arbitrary")),
        name="router",
    )(x, nw.reshape(1, d), shift, scale, wr2, rb.reshape(1, N_EXPERTS))


def _prefix_count(mask, tri):
    e, t = mask.shape
    nb = t // LANES
    stacked = jnp.concatenate([mask[:, c * LANES:(c + 1) * LANES] for c in range(nb)], axis=0).astype(BF16)
    local = jnp.dot(stacked, tri, preferred_element_type=F32)
    out = []
    off = jnp.zeros((e, 1), F32)
    for c in range(nb):
        blk = local[c * e:(c + 1) * e, :]
        out.append(blk + off)
        off = off + blk[:, LANES - 1:LANES]
    return jnp.concatenate(out, axis=1)


def _select_kernel(at_ref, idx_ref, cnt_scr, icol_scr, *, cap, jt):
    a = at_ref[0]
    e, t = a.shape
    bits = lax.bitcast_convert_type(a, I32)

    def search(i, v):
        cand = v | jnp.left_shift(jnp.int32(1), 30 - i)
        cnt = jnp.sum((bits >= cand).astype(I32), axis=1, keepdims=True)
        return jnp.where(cnt >= cap, cand, v)

    thr = lax.fori_loop(0, 31, search, jnp.zeros((e, 1), I32))
    r = lax.broadcasted_iota(I32, (LANES, LANES), 0)
    s = lax.broadcasted_iota(I32, (LANES, LANES), 1)
    tri = (r <= s).astype(BF16)
    gt = bits > thr
    eq = bits == thr
    need = (cap - jnp.sum(gt.astype(I32), axis=1, keepdims=True)).astype(F32)
    sel = gt | (eq & (_prefix_count(eq.astype(F32), tri) <= need))
    cnt = _prefix_count(sel.astype(F32), tri)
    for ei in range(e):
        cnt_scr[ei] = cnt[ei:ei + 1, :]
    icol_scr[...] = jnp.zeros_like(icol_scr)
    lane = lax.broadcasted_iota(I32, (jt, LANES), 1)

    def per_expert(ei, carry):
        for jb in range(cap // jt):
            jcol = (lax.broadcasted_iota(I32, (jt, 1), 0) + jb * jt).astype(F32)
            acc_n = jnp.zeros((jt, LANES), F32)
            for tb in range(t // LANES):
                cn = cnt_scr[ei, :, tb * LANES:(tb + 1) * LANES]
                acc_n += jnp.where(cn <= jcol, 1.0, 0.0)
            tok = jnp.sum(acc_n, axis=1, keepdims=True)
            rows = slice(jb * jt, (jb + 1) * jt)
            icol_scr[rows, :] = jnp.where(lane == ei, tok, icol_scr[rows, :])
        return carry

    lax.fori_loop(0, e, per_expert, 0)
    idx_ref[0, :, 0, :] = icol_scr[...].T[:e, :cap].astype(I32)


def _route_select(aff_t, cap):
    b, e, t = aff_t.shape
    jt = min(cap, 64)
    return pl.pallas_call(
        functools.partial(_select_kernel, cap=cap, jt=jt),
        grid=(b,),
        in_specs=[pl.BlockSpec((1, e, t), lambda bi: (bi, 0, 0))],
        out_specs=pl.BlockSpec((1, e, 1, cap), lambda bi: (bi, 0, 0, 0)),
        out_shape=jax.ShapeDtypeStruct((b, e, 1, cap), I32),
        scratch_shapes=[pltpu.VMEM((e, 1, t), F32), pltpu.VMEM((max(cap, LANES), LANES), F32)],
        compiler_params=_params(("arbitrary",)),
        name="route_select",
    )(aff_t)


def _moe_kernel(idxp_ref, idxn_ref, hp_ref, g2_ref, wg_ref, wu_ref, wd_ref, fnw_ref, x_hbm, o_hbm,
                xe32_scr, xe_scr, gate_scr, y_scr, yg_scr, acc_scr, sem_in, sem_out,
                *, gsz, cap, t, n_groups, n_f, final_norm):
    grp = pl.program_id(0)
    e = pl.program_id(1)
    f = pl.program_id(2)
    n_e = pl.num_programs(1)
    half = hp_ref.shape[2] - LANES
    q_rows = cap // n_f

    def in_copy(gi):
        return pltpu.make_async_copy(x_hbm.at[pl.ds(gi * gsz, gsz)], acc_scr.at[:, pl.ds(0, t)], sem_in)

    def out_copy(gi):
        return pltpu.make_async_copy(acc_scr.at[:, pl.ds(0, t)], o_hbm.at[pl.ds(gi * gsz, gsz)], sem_out)

    def unpack(expert):
        p = xe32_scr[:, :half]
        xe_scr[:, :half] = lax.bitcast_convert_type(p << 16, F32).astype(BF16)
        xe_scr[:, half:] = lax.bitcast_convert_type(p & jnp.uint32(0xFFFF0000), F32).astype(BF16)
        aff = lax.bitcast_convert_type(xe32_scr[:, half:], F32)
        lane = lax.broadcasted_iota(I32, aff.shape, 1)
        gate = jnp.sum(jnp.where(lane == expert, aff, 0.0), axis=1, keepdims=True)
        gate_scr[...] = jnp.broadcast_to(gate, gate_scr.shape)

    def gated():
        for si in range(gsz):
            rs = slice(si * cap, (si + 1) * cap)
            g2 = g2_ref[si]
            for cbk in range(y_scr.shape[1] // LANES):
                ls = slice(cbk * LANES, (cbk + 1) * LANES)
                yg_scr[rs, ls] = y_scr[rs, ls] * gate_scr[rs, :] * g2[:, ls]

    @pl.when((e == 0) & (f == 0))
    def _():
        @pl.when(grp > 0)
        def _():
            out_copy(grp - 1).wait()
        in_copy(grp).start()
        for si in range(gsz):
            def gather(j, carry):
                tkn = idxp_ref[si, 0, 0, j]
                xe32_scr[pl.ds(si * cap + j, 1), :] = hp_ref[si, pl.ds(tkn, 1), :]
                return carry
            lax.fori_loop(0, cap, gather, 0, unroll=8)
        unpack(e)
        y_scr[...] = jnp.zeros_like(y_scr)
        yg_scr[...] = jnp.zeros_like(yg_scr)
        acc_scr[:, t:t + 8, :] = jnp.zeros((gsz, 8, acc_scr.shape[2]), F32)

    @pl.when((e > 0) & (f == 0))
    def _():
        gated()
        unpack(e)

    @pl.when((e == 1) & (f == 0))
    def _():
        in_copy(grp).wait()

    xe = xe_scr[...]
    a = jnp.dot(xe, wg_ref[0], preferred_element_type=F32)
    u = jnp.dot(xe, wu_ref[0], preferred_element_type=F32)
    hid = (a * _sigmoid(a) * u).astype(BF16)
    part = jnp.dot(hid, wd_ref[0].astype(BF16), preferred_element_type=F32)
    spare = jnp.where(e == 0, 1, 0)
    for si in range(gsz):
        for jj in range(q_rows):
            j = f * q_rows + jj
            tkn = idxn_ref[si, 0, 0, j]
            xe32_scr[pl.ds(si * cap + j, 1), :] = hp_ref[si, pl.ds(tkn, 1), :]
        for jj in range(q_rows):
            j = f * q_rows + jj
            tkn = jnp.where(spare == 1, t, idxp_ref[si, 0, 0, j])
            acc_scr[si, pl.ds(tkn, 1), :] += yg_scr[pl.ds(si * cap + j, 1), :]

    y_scr[...] = jnp.where(f > 0, y_scr[...], 0.0) + part

    @pl.when((e == n_e - 1) & (f == n_f - 1))
    def _():
        gated()
        for si in range(gsz):
            def scatter(j, carry):
                tkn = idxn_ref[si, 0, 0, j]
                acc_scr[si, pl.ds(tkn, 1), :] += yg_scr[pl.ds(si * cap + j, 1), :]
                return carry
            lax.fori_loop(0, cap, scatter, 0, unroll=8)
        if final_norm:
            def norm_rows(r, carry):
                rows = pl.ds(pl.multiple_of(r * NORM_ROWS, NORM_ROWS), NORM_ROWS)
                for si in range(gsz):
                    blk = acc_scr[si, rows, :]
                    inv = lax.rsqrt(jnp.mean(blk * blk, axis=-1, keepdims=True) + EPS)
                    acc_scr[si, rows, :] = blk * inv * fnw_ref[...]
                return carry
            lax.fori_loop(0, t // NORM_ROWS, norm_rows, 0)
        out_copy(grp).start()

        @pl.when(grp == n_groups - 1)
        def _():
            out_copy(grp).wait()


def _moe_ffn(idx, hp, g2, x, wg, wu, wd, fnw, layer, gsz, tf, final_norm):
    b, t, d = x.shape
    e, cap = idx.shape[1], idx.shape[3]
    ff = wg.shape[3]
    n_groups, n_f = b // gsz, ff // tf
    hw = hp.shape[2]
    smem_idx = lambda fn: pl.BlockSpec((gsz, 1, 1, cap), fn, memory_space=pltpu.SMEM)
    return pl.pallas_call(
        functools.partial(_moe_kernel, gsz=gsz, cap=cap, t=t, n_groups=n_groups, n_f=n_f, final_norm=final_norm),
        grid=(n_groups, e, n_f),
        in_specs=[smem_idx(lambda gi, ei, fi: (gi, jnp.maximum(ei - 1, 0), 0, 0)),
                  smem_idx(lambda gi, ei, fi: (gi, jnp.minimum(ei + 1, e - 1), 0, 0)),
                  pl.BlockSpec((gsz, t, hw), lambda gi, ei, fi: (gi, 0, 0), pipeline_mode=pl.Buffered(1)),
                  pl.BlockSpec((gsz, 1, d), lambda gi, ei, fi: (gi, 0, 0)),
                  pl.BlockSpec((None, 1, d, tf), lambda gi, ei, fi: (layer, ei, 0, fi)),
                  pl.BlockSpec((None, 1, d, tf), lambda gi, ei, fi: (layer, ei, 0, fi)),
                  pl.BlockSpec((None, 1, tf, d), lambda gi, ei, fi: (layer, ei, fi, 0)),
                  pl.BlockSpec((1, d), lambda gi, ei, fi: (0, 0)),
                  pl.BlockSpec(memory_space=pl.ANY)],
        out_specs=pl.BlockSpec(memory_space=pl.ANY),
        out_shape=jax.ShapeDtypeStruct((b, t, d), F32),
        scratch_shapes=[pltpu.VMEM((gsz * cap, hw), U32), pltpu.VMEM((gsz * cap, d), BF16),
                        pltpu.VMEM((gsz * cap, LANES), F32),
                        pltpu.VMEM((gsz * cap, d), F32), pltpu.VMEM((gsz * cap, d), F32),
                        pltpu.VMEM((gsz, t + 8, d), F32),
                        pltpu.SemaphoreType.DMA(()), pltpu.SemaphoreType.DMA(())],
        compiler_params=_params(("arbitrary", "arbitrary", "arbitrary")),
        name="moe_ffn",
    )(idx, idx, hp, g2, wg, wu, wd, fnw.reshape(1, d), x)


def _stack_hi_lo(w):
    hi, lo = _split_bf16(w)
    pad = jnp.zeros((w.shape[0], LANES - 2 * w.shape[1]), BF16)
    return jnp.concatenate([hi, lo, pad], axis=1)


def _moe(x, nw, shift, scale, g2, wr2, rb, wg, wu, wd, fnw, layer, gsz, tm, final_norm=False):
    t = x.shape[1]
    cap = CAPACITY_FACTOR * t // N_EXPERTS
    hp, aff_t = _router(x, nw, shift, scale, wr2, rb, tm)
    idx = _route_select(aff_t, cap)
    return _moe_ffn(idx, hp, g2, x, wg, wu, wd, fnw, layer, gsz, tf=1024, final_norm=final_norm)


def kernel(x, c, ctx, c_ctx, ada_w, ada_b, norm1_w, norm2_w, w_in, mlstm_gate_b, mlstm_norm_w, w_mlstm_out,
           conv_dw_w, conv_dw_b, conv_ln_w, conv_ln_b, w_conv_out, sg_ln_w, sg_ln_b, sg_w, sg_b, w_sg_out, w_o,
           router_w, router_b, expert_w_gate, expert_w_up, expert_w_down, final_norm_w):
    depth = ada_w.shape[0]
    b, t, d = x.shape
    dh = d // N_HEADS
    n_state = 3 * d + N_GATE_COLS
    cond = jnp.concatenate([c, c_ctx[None, :], jnp.zeros((16 - b - 1, d), F32)], axis=0)
    wg, wu, wd = expert_w_gate.astype(BF16), expert_w_up.astype(BF16), expert_w_down
    tc = ctx.shape[1]
    w_all = _w_regroup(w_in, 3 * d, n_state)

    for layer in range(depth):
        need_ctx = layer < depth - 1
        mod = _modulation(cond, ada_w, ada_b, layer)
        lat = [m[:, None, :] for m in jnp.split(mod[:b], 6, axis=-1)]
        cx = [jnp.broadcast_to(m[None], (b, 1, d)) for m in jnp.split(mod[b:b + 1], 6, axis=-1)]

        w_gate2 = _stack_hi_lo(w_in[layer, :, 3 * d:n_state])
        z_lat, gt_lat = _in_proj(x, norm1_w[layer], lat[0], lat[1], w_all, layer, 0, 9 * d, w_gate2,
                                 mlstm_gate_b[layer], tm=2048, tn=1536)
        col0, n_ctx = (0, 9 * d) if need_ctx else (6 * d, 3 * d)
        z_ctx, gt_ctx = _in_proj(ctx.reshape(1, b * tc, d), norm1_w[layer], cx[0][:1], cx[1][:1], w_all, layer,
                                 col0, n_ctx, w_gate2, mlstm_gate_b[layer], tm=2048, tn=1536)
        z_ctx = z_ctx.reshape(b, tc, z_ctx.shape[2])
        gt_ctx = gt_ctx.reshape(N_GATE_COLS, b, tc).transpose(1, 0, 2)

        hcf, hcb, state = _mlstm(z_ctx, 2 if need_ctx else 0, gt_ctx, None, dh, ch=2)
        hlf, hlb, _ = _mlstm(z_lat, 2, gt_lat, state, dh, ch=4)

        bf = lambda a: a[layer].astype(BF16)
        p = dict(mlstm_norm_w=mlstm_norm_w[layer], w_mlstm_out=bf(w_mlstm_out), conv_dw_w=conv_dw_w[layer],
                 conv_dw_b=conv_dw_b[layer], conv_ln_w=conv_ln_w[layer], conv_ln_b=conv_ln_b[layer],
                 w_conv_out=bf(w_conv_out), sg_ln_w=sg_ln_w[layer], sg_ln_b=sg_ln_b[layer], sg_w=bf(sg_w),
                 sg_b=sg_b[layer], w_sg_out=bf(w_sg_out), w_o=bf(w_o))
        x = _mixer_out(z_lat, 0, hlf, hlb, x, lat[2], p, row_len=GRID_W, tm=512)
        if need_ctx:
            ctx = _mixer_out(z_ctx, 0, hcf, hcb, ctx, cx[2], p, row_len=ctx.shape[1], tm=256)

        wr2 = _stack_hi_lo(router_w[layer])
        x = _moe(x, norm2_w[layer], lat[3], lat[4], lat[5], wr2, router_b[layer], wg, wu, wd, final_norm_w, layer,
                 gsz=1, tm=2048, final_norm=layer == depth - 1)
        if need_ctx:
            ctx = _moe(ctx, norm2_w[layer], cx[3], cx[4], cx[5], wr2, router_b[layer], wg, wu, wd, final_norm_w,
                       layer, gsz=b, tm=256)

    return x
```

```python
import functools

import jax
import jax.numpy as jnp
from jax import lax
from jax.experimental import pallas as pl
from jax.experimental.pallas import tpu as pltpu

F32 = jnp.float32
BF16 = jnp.bfloat16
I32 = jnp.int32
U32 = jnp.uint32

EPS = 1e-6
LANES = 128
SUBLANES = 8
VMEM_LIMIT = 56 * 1024 * 1024

N_HEADS = 4
MLSTM_CHUNK = 128
N_DIRS = 2
N_GATE_COLS = N_DIRS * 2 * N_HEADS
DW_CONV_SIZE = 31
CONV_PAD = 16
SG_GROUPS = 4
SG_CHUNK = 128
N_EXPERTS = 16
CAPACITY_FACTOR = 2
GRID_W = 64
NORM_ROWS = 128


def _params(sem, vmem=VMEM_LIMIT):
    return pltpu.CompilerParams(dimension_semantics=sem, vmem_limit_bytes=vmem)


def _sigmoid(v):
    return 0.5 * jnp.tanh(0.5 * v) + 0.5


def _split_bf16(a):
    hi = a.astype(BF16)
    lo = (a - hi.astype(F32)).astype(BF16)
    return hi, lo


def _split3_bf16(a):
    hi = a.astype(BF16)
    r1 = a - hi.astype(F32)
    mid = r1.astype(BF16)
    lo = (r1 - mid.astype(F32)).astype(BF16)
    return hi, mid, lo


def _norm_mod(x, nw, shift, scale):
    ms = jnp.mean(x * x, axis=-1, keepdims=True)
    y = x * lax.rsqrt(ms + EPS) * nw
    return y * (1.0 + scale) + shift


def _modulation_kernel(a_ref, w_ref, b_ref, o_ref):
    a = a_ref[...]
    a = a * _sigmoid(a)
    a_hi, a_lo = _split_bf16(a)
    w_hi, w_lo = _split_bf16(w_ref[...])
    acc = jnp.dot(a_hi, w_hi, preferred_element_type=F32)
    acc += jnp.dot(a_hi, w_lo, preferred_element_type=F32)
    acc += jnp.dot(a_lo, w_hi, preferred_element_type=F32)
    o_ref[...] = acc + b_ref[...]


def _modulation(cond, w, b, layer, tn=1536):
    m, d = cond.shape
    n = w.shape[2]
    return pl.pallas_call(
        _modulation_kernel,
        grid=(n // tn,),
        in_specs=[pl.BlockSpec((m, d), lambda j: (0, 0)),
                  pl.BlockSpec((None, d, tn), lambda j: (layer, 0, j)),
                  pl.BlockSpec((None, 1, tn), lambda j: (layer, 0, j))],
        out_specs=pl.BlockSpec((m, tn), lambda j: (0, j)),
        out_shape=jax.ShapeDtypeStruct((m, n), F32),
        compiler_params=_params(("arbitrary",)),
        name="modulation",
    )(cond, w, b.reshape(b.shape[0], 1, n))


def _w_regroup_kernel(w_ref, o_ref, *, n_qkv, n_state):
    n_rest = w_ref.shape[1] - n_state
    o_ref[:, :n_rest] = w_ref[:, n_state:].astype(BF16)
    o_ref[:, n_rest:] = w_ref[:, :n_qkv].astype(BF16)


def _w_regroup(w_in, n_qkv, n_state, tr=256):
    nl, d, n_in = w_in.shape
    n_out = n_in - (n_state - n_qkv)
    return pl.pallas_call(
        functools.partial(_w_regroup_kernel, n_qkv=n_qkv, n_state=n_state),
        grid=(nl, d // tr),
        in_specs=[pl.BlockSpec((None, tr, n_in), lambda li, i: (li, i, 0))],
        out_specs=pl.BlockSpec((None, tr, n_out), lambda li, i: (li, i, 0)),
        out_shape=jax.ShapeDtypeStruct((nl, d, n_out), BF16),
        compiler_params=_params(("arbitrary", "arbitrary")),
        name="w_regroup",
    )(w_in)


def _in_proj_kernel(x_ref, nw_ref, sh_ref, sc_ref, w_ref, wg_ref, gb_ref, z_ref, gt_ref, h_scr):
    j = pl.program_id(2)

    @pl.when(j == 0)
    def _():
        h = _norm_mod(x_ref[0], nw_ref[...], sh_ref[0], sc_ref[0])
        h_hi, h_lo = _split_bf16(h)
        h_scr[...] = h_hi
        wg = wg_ref[...]
        raw = jnp.dot(h_hi, wg, preferred_element_type=F32) + jnp.dot(h_lo, wg, preferred_element_type=F32)
        raw_t = raw.T
        raw_t = raw_t[:N_GATE_COLS, :] + raw_t[N_GATE_COLS:2 * N_GATE_COLS, :] + gb_ref[...]
        gate_id = lax.broadcasted_iota(I32, raw_t.shape, 0)
        is_forget = ((gate_id >> 2) & 1) == 1
        logsig = jnp.minimum(raw_t, 0.0) - jnp.log(1.0 + jnp.exp(-jnp.abs(raw_t)))
        gt_ref[0] = jnp.where(is_forget, logsig, raw_t)

    z_ref[0] = jnp.dot(h_scr[...], w_ref[...], preferred_element_type=F32).astype(BF16)


def _in_proj(x, nw, shift, scale, w_all, layer, col0, n, w_gate2, gate_b, tm, tn):
    b, t, d = x.shape
    tm = min(tm, t)
    jb0 = col0 // tn
    return pl.pallas_call(
        _in_proj_kernel,
        grid=(b, t // tm, n // tn),
        in_specs=[pl.BlockSpec((1, tm, d), lambda bi, i, j: (bi, i, 0)),
                  pl.BlockSpec((1, d), lambda bi, i, j: (0, 0)),
                  pl.BlockSpec((1, 1, d), lambda bi, i, j: (bi, 0, 0)),
                  pl.BlockSpec((1, 1, d), lambda bi, i, j: (bi, 0, 0)),
                  pl.BlockSpec((None, d, tn), lambda bi, i, j: (layer, 0, jb0 + j)),
                  pl.BlockSpec((d, LANES), lambda bi, i, j: (0, 0)),
                  pl.BlockSpec((N_GATE_COLS, 1), lambda bi, i, j: (0, 0))],
        out_specs=[pl.BlockSpec((1, tm, tn), lambda bi, i, j: (bi, i, j)),
                   pl.BlockSpec((1, N_GATE_COLS, tm), lambda bi, i, j: (bi, 0, i))],
        out_shape=[jax.ShapeDtypeStruct((b, t, n), BF16),
                   jax.ShapeDtypeStruct((b, N_GATE_COLS, t), F32)],
        scratch_shapes=[pltpu.VMEM((tm, d), BF16)],
        compiler_params=_params(("arbitrary", "arbitrary", "arbitrary")),
        name="in_proj",
    )(x, nw.reshape(1, d), shift, scale, w_all, w_gate2, gate_b.reshape(N_GATE_COLS, 1))


def _mlstm_chunk(zf_ref, zb_ref, gtf_ref, gtb_ref, hf_ref, hb_ref, c_scr, n_scr, m_scr, r0, dh):
    L = MLSTM_CHUNK
    width = N_HEADS * dh
    scans = [(d, hh) for d in range(N_DIRS) for hh in range(N_HEADS)]
    n_s = len(scans)
    rows = [slice(r, r + L) for r in r0]

    row = lax.broadcasted_iota(I32, (L, L), 0)
    colm = lax.broadcasted_iota(I32, (L, L), 1)
    eye = row == colm
    eye_bf = eye.astype(BF16)
    eye3 = jnp.concatenate([eye_bf, eye_bf, eye_bf], axis=1)
    keep_t = (row <= colm, row >= colm)
    tri = (keep_t[0].astype(BF16), keep_t[1].astype(BF16))

    gts = [(gtf_ref, gtb_ref)[d][0, :, rows[d]] for d in range(N_DIRS)]
    cums = [sum(jnp.dot(part, tri[d], preferred_element_type=F32) for part in _split3_bf16(gts[d]))
            for d in range(N_DIRS)]

    q, k, v, m_prev, i_row, b_row, b_last = [], [], [], [], [], [], []
    for si, (d, hh) in enumerate(scans):
        z_ref = (zf_ref, zb_ref)[d]
        q.append(z_ref[0, rows[d], hh * dh:(hh + 1) * dh])
        k.append(z_ref[0, rows[d], width + hh * dh:width + (hh + 1) * dh] * jnp.asarray(dh ** -0.5, BF16))
        v.append(z_ref[0, rows[d], 2 * width + hh * dh:2 * width + (hh + 1) * dh])
        m_prev.append(m_scr[si][:, 0:1])
        ci = d * 2 * N_HEADS + hh
        cf = ci + N_HEADS
        i_row.append(gts[d][ci:ci + 1, :])
        b_row.append(cums[d][cf:cf + 1, :])
        b_last.append(b_row[si][:, L - 1:L] if d == 0 else b_row[si][:, 0:1])

    nt = (((1,), (1,)), ((), ()))
    s_t = [lax.dot_general(k[i], q[i], nt, preferred_element_type=F32) for i in range(n_s)]
    qc = [jnp.dot(q[i], c_scr[i].astype(BF16), preferred_element_type=F32).astype(BF16) for i in range(n_s)]
    qn = [lax.dot_general(n_scr[i].astype(BF16), q[i], nt, preferred_element_type=F32)[0:1, :] for i in range(n_s)]
    cb = []
    for i in range(n_s):
        parts = [jnp.broadcast_to(p, (LANES, L)) for p in _split3_bf16(b_row[i] - i_row[i])]
        cb.append(lax.dot_general(eye3, jnp.concatenate(parts, axis=1), nt, preferred_element_type=F32))

    for i in range(n_s):
        g_row = b_last[i] - b_row[i] + i_row[i]
        mn = jnp.maximum(b_last[i] + m_prev[i], jnp.max(g_row, axis=-1, keepdims=True))
        decay = jnp.exp(b_last[i] + m_prev[i] - mn)
        w_row = jnp.exp(g_row - mn)
        ktw = (k[i].T.astype(F32) * w_row).astype(BF16)
        kv = jnp.dot(ktw, v[i], preferred_element_type=F32)
        wk = jnp.dot(jnp.broadcast_to(w_row, (8, L)).astype(BF16), k[i], preferred_element_type=F32)
        c_new = decay * c_scr[i] + kv
        n_new = decay * n_scr[i] + wk
        c_scr[i] = c_new
        n_scr[i] = n_new
        m_scr[i] = jnp.broadcast_to(mn, (1, LANES))

    for i, (d, hh) in enumerate(scans):
        h_ref = (hf_ref, hb_ref)[d]
        log_d = jnp.where(keep_t[d], b_row[i] - cb[i], -jnp.inf)
        m_inter = b_row[i] + m_prev[i]
        m_t = jnp.maximum(m_inter, jnp.max(log_d, axis=0, keepdims=True))
        p_t = s_t[i] * jnp.exp(log_d - m_t)
        inter = jnp.exp(m_inter - m_t)
        den = jnp.sum(p_t, axis=0, keepdims=True) + inter * qn[i]
        scale = 1.0 / jnp.maximum(jnp.abs(den), jnp.exp(-m_t))
        lhs_t = jnp.concatenate([(p_t * scale).astype(BF16), jnp.where(eye, scale * inter, 0.0).astype(BF16)], axis=0)
        rhs = jnp.concatenate([v[i], qc[i]], axis=0)
        h = lax.dot_general(lhs_t, rhs, (((0,), (0,)), ((), ())), preferred_element_type=F32)
        h_ref[0, rows[d], hh * dh:(hh + 1) * dh] = h.astype(h_ref.dtype)


def _mlstm_kernel(*refs, dh, has_init, n_steps, ch):
    if has_init:
        (zf_ref, zb_ref, gtf_ref, gtb_ref, c0_ref, n0_ref, m0_ref,
         hf_ref, hb_ref, cN_ref, nN_ref, mN_ref, c_scr, n_scr, m_scr) = refs
    else:
        zf_ref, zb_ref, gtf_ref, gtb_ref, hf_ref, hb_ref, cN_ref, nN_ref, mN_ref, c_scr, n_scr, m_scr = refs
    c = pl.program_id(1)

    @pl.when(c == 0)
    def _():
        if has_init:
            c_scr[...] = c0_ref[0]
            n_scr[...] = n0_ref[0]
            m_scr[...] = m0_ref[0]
        else:
            c_scr[...] = jnp.zeros_like(c_scr)
            n_scr[...] = jnp.zeros_like(n_scr)
            m_scr[...] = jnp.zeros_like(m_scr)

    for sc in range(ch):
        r0 = (sc * MLSTM_CHUNK, (ch - 1 - sc) * MLSTM_CHUNK)
        _mlstm_chunk(zf_ref, zb_ref, gtf_ref, gtb_ref, hf_ref, hb_ref, c_scr, n_scr, m_scr, r0, dh)

    @pl.when(c == n_steps - 1)
    def _():
        cN_ref[0] = c_scr[...]
        nN_ref[0] = n_scr[...]
        mN_ref[0] = m_scr[...]


def _mlstm(z, zblk, gates_t, init, dh, ch):
    b, t, _ = z.shape
    L = ch * MLSTM_CHUNK
    nc = t // L
    width = N_HEADS * dh
    ns = N_DIRS * N_HEADS
    fwd = lambda bi, c: (bi, c, 0)
    bwd = lambda bi, c: (bi, nc - 1 - c, 0)
    st4 = lambda bi, c: (bi, 0, 0, 0)
    in_specs = [pl.BlockSpec((1, L, 3 * width), lambda bi, c: (bi, c, zblk)),
                pl.BlockSpec((1, L, 3 * width), lambda bi, c: (bi, nc - 1 - c, zblk)),
                pl.BlockSpec((1, N_GATE_COLS, L), lambda bi, c: (bi, 0, c)),
                pl.BlockSpec((1, N_GATE_COLS, L), lambda bi, c: (bi, 0, nc - 1 - c))]
    args = [z, z, gates_t, gates_t]
    state_specs = [pl.BlockSpec((1, ns, dh, dh), st4), pl.BlockSpec((1, ns, 8, dh), st4),
                   pl.BlockSpec((1, ns, 1, LANES), st4)]
    state_shapes = [jax.ShapeDtypeStruct((b, ns, dh, dh), F32), jax.ShapeDtypeStruct((b, ns, 8, dh), F32),
                    jax.ShapeDtypeStruct((b, ns, 1, LANES), F32)]
    if init is not None:
        in_specs += state_specs
        args += list(init)
    outs = pl.pallas_call(
        functools.partial(_mlstm_kernel, dh=dh, has_init=init is not None, n_steps=nc, ch=ch),
        grid=(b, nc),
        in_specs=in_specs,
        out_specs=[pl.BlockSpec((1, L, width), fwd), pl.BlockSpec((1, L, width), bwd)] + state_specs,
        out_shape=[jax.ShapeDtypeStruct((b, t, width), BF16), jax.ShapeDtypeStruct((b, t, width), BF16)] + state_shapes,
        scratch_shapes=[pltpu.VMEM((ns, dh, dh), F32), pltpu.VMEM((ns, 8, dh), F32), pltpu.VMEM((ns, 1, LANES), F32)],
        compiler_params=_params(("arbitrary", "arbitrary")),
        name="mlstm",
    )(*args)
    return outs[0], outs[1], tuple(outs[2:])


def _layer_norm(v, w, b):
    mu = jnp.mean(v, axis=-1, keepdims=True)
    vc = v - mu
    var = jnp.mean(vc * vc, axis=-1, keepdims=True)
    return vc * lax.rsqrt(var + EPS) * w + b


def _gelu_tanh(v):
    return 0.5 * v * (1.0 + jnp.tanh(0.7978845608028654 * (v + 0.044715 * (v * v * v))))


def _mixer_kernel(z_ref, hf_ref, hb_ref, x_ref, g1_ref, mnw_ref, wm_ref, dww_ref, dwb_ref, clw_ref, clb_ref,
                  wc_ref, slw_ref, slb_ref, sgw_ref, sgb_ref, ws_ref, wo_ref, o_ref, pad_scr, conv_scr,
                  *, d, row_len):
    tm = z_ref.shape[1]
    dh = d // N_HEADS
    cc = d // 2
    off_conv, off_sg, off_merge = d, 2 * d, 3 * d

    heads = []
    for hh in range(N_HEADS):
        sl = slice(hh * dh, (hh + 1) * dh)
        hm = hf_ref[0, :, sl].astype(F32) + hb_ref[0, :, sl].astype(F32)
        yn = hm * lax.rsqrt(jnp.mean(hm * hm, axis=-1, keepdims=True) + EPS) * mnw_ref[:, sl]
        heads.append(yn.astype(BF16) * _sigmoid(z_ref[0, :, sl]))
    y_m = jnp.dot(jnp.concatenate(heads, axis=1), wm_ref[...], preferred_element_type=F32)

    u = z_ref[0, :, off_conv:off_conv + cc].astype(F32) * _sigmoid(z_ref[0, :, off_conv + cc:off_conv + 2 * cc]).astype(F32)
    n_rows = tm // row_len
    zpad = jnp.zeros((SUBLANES, n_rows, CONV_PAD, cc), F32)
    pad_scr[:, :, 0:CONV_PAD, :] = zpad
    pad_scr[:, :, row_len:row_len + CONV_PAD, :] = zpad
    pad_scr[:, :, row_len + CONV_PAD:row_len + 2 * CONV_PAD, :] = zpad
    for j in range(SUBLANES):
        for r in range(n_rows):
            pad_scr[j, r, CONV_PAD - j:CONV_PAD - j + row_len, :] = u[r * row_len:(r + 1) * row_len, :]
    base = CONV_PAD - DW_CONV_SIZE // 2
    for cb in range(cc // LANES):
        ls = slice(cb * LANES, (cb + 1) * LANES)
        taps = [dww_ref[kk:kk + 1, ls] for kk in range(DW_CONV_SIZE)]
        for r in range(n_rows):
            acc = jnp.zeros((row_len, LANES), F32)
            for kk in range(DW_CONV_SIZE):
                hi, lo = divmod(base + kk, SUBLANES)
                acc += pad_scr[lo, r, hi * SUBLANES:hi * SUBLANES + row_len, ls] * taps[kk]
            conv_scr[r * row_len:(r + 1) * row_len, ls] = acc
    cv = _layer_norm(conv_scr[...] + dwb_ref[...], clw_ref[...], clb_ref[...])
    cv = cv * _sigmoid(cv)
    y_c = jnp.dot(cv.astype(BF16), wc_ref[...], preferred_element_type=F32)

    su = _gelu_tanh(z_ref[0, :, off_sg:off_sg + cc].astype(F32))
    sv = _gelu_tanh(z_ref[0, :, off_sg + cc:off_sg + 2 * cc].astype(F32))
    sv = _layer_norm(sv, slw_ref[...], slb_ref[...]).astype(BF16)
    gd = cc // SG_GROUPS
    for ch in range(tm // SG_CHUNK):
        rs = slice(ch * SG_CHUNK, (ch + 1) * SG_CHUNK)
        parts = []
        for gi in range(SG_GROUPS):
            mixed = jnp.dot(sgw_ref[gi], sv[rs, gi * gd:(gi + 1) * gd], preferred_element_type=F32)
            parts.append(mixed + sgb_ref[:, gi:gi + 1])
        gated = su[rs, :] * jnp.concatenate(parts, axis=1)
        conv_scr[rs, :] = gated
    y_s = jnp.dot(conv_scr[...].astype(BF16), ws_ref[...], preferred_element_type=F32)

    gm0 = _sigmoid(z_ref[0, :, off_merge:off_merge + d])
    gm1 = _sigmoid(z_ref[0, :, off_merge + d:off_merge + 2 * d])
    gm2 = _sigmoid(z_ref[0, :, off_merge + 2 * d:off_merge + 3 * d])
    merged = gm0 * y_m.astype(BF16) + gm1 * y_c.astype(BF16) + gm2 * y_s.astype(BF16)
    y = jnp.dot(merged, wo_ref[...], preferred_element_type=F32)
    o_ref[0] = x_ref[0] + g1_ref[0] * y


def _mixer_out(z, zblk, hf, hb, x, g1, p, row_len, tm):
    b, t, d = x.shape
    cc = d // 2
    tm = min(tm, t)
    nz = 6 * d
    full = lambda a: pl.BlockSpec(a.shape, lambda bi, i: (0,) * a.ndim)
    row = lambda a: a.reshape(1, -1)
    consts = [row(p["mlstm_norm_w"]), p["w_mlstm_out"], p["conv_dw_w"], row(p["conv_dw_b"]), row(p["conv_ln_w"]),
              row(p["conv_ln_b"]), p["w_conv_out"], row(p["sg_ln_w"]), row(p["sg_ln_b"]), p["sg_w"], p["sg_b"].T,
              p["w_sg_out"], p["w_o"]]
    tok = lambda w: pl.BlockSpec((1, tm, w), lambda bi, i: (bi, i, 0))
    return pl.pallas_call(
        functools.partial(_mixer_kernel, d=d, row_len=row_len),
        grid=(b, t // tm),
        in_specs=[pl.BlockSpec((1, tm, nz), lambda bi, i: (bi, i, zblk)), tok(d), tok(d), tok(d),
                  pl.BlockSpec((1, 1, d), lambda bi, i: (bi, 0, 0))] + [full(a) for a in consts],
        out_specs=tok(d),
        out_shape=jax.ShapeDtypeStruct((b, t, d), F32),
        scratch_shapes=[pltpu.VMEM((SUBLANES, tm // row_len, row_len + 2 * CONV_PAD, cc), F32),
                        pltpu.VMEM((tm, cc), F32)],
        compiler_params=_params(("arbitrary", "arbitrary")),
        name="mixer_out",
    )(z, hf, hb, x, g1, *consts)


def _router_kernel(x_ref, nw_ref, sh_ref, sc_ref, wr_ref, rb_ref, hp_ref, at_ref):
    h = _norm_mod(x_ref[0], nw_ref[...], sh_ref[0], sc_ref[0])
    h_hi, h_lo = _split_bf16(h)
    wr = wr_ref[...]
    raw = jnp.dot(h_hi, wr, preferred_element_type=F32) + jnp.dot(h_lo, wr, preferred_element_type=F32)
    logits = raw[:, :N_EXPERTS] + raw[:, N_EXPERTS:2 * N_EXPERTS] + rb_ref[...]
    mx = jnp.max(logits, axis=-1, keepdims=True)
    ex = jnp.exp(logits - mx)
    aff = ex / jnp.sum(ex, axis=-1, keepdims=True)
    apad = jnp.concatenate([aff, jnp.zeros((aff.shape[0], LANES - N_EXPERTS), F32)], axis=1)
    at_ref[0] = apad.T[:N_EXPERTS, :]
    half = h.shape[1] // 2
    bits = lax.bitcast_convert_type(h_hi.astype(F32), U32)
    hp_ref[0, :, :half] = (bits[:, :half] >> 16) | (bits[:, half:] & jnp.uint32(0xFFFF0000))
    hp_ref[0, :, half:] = lax.bitcast_convert_type(apad, U32)


def _router(x, nw, shift, scale, wr2, rb, tm):
    b, t, d = x.shape
    tm = min(tm, t)
    return pl.pallas_call(
        _router_kernel,
        grid=(b, t // tm),
        in_specs=[pl.BlockSpec((1, tm, d), lambda bi, i: (bi, i, 0)),
                  pl.BlockSpec((1, d), lambda bi, i: (0, 0)),
                  pl.BlockSpec((1, 1, d), lambda bi, i: (bi, 0, 0)),
                  pl.BlockSpec((1, 1, d), lambda bi, i: (bi, 0, 0)),
                  pl.BlockSpec((d, LANES), lambda bi, i: (0, 0)),
                  pl.BlockSpec((1, N_EXPERTS), lambda bi, i: (0, 0))],
        out_specs=[pl.BlockSpec((1, tm, d // 2 + LANES), lambda bi, i: (bi, i, 0)),
                   pl.BlockSpec((1, N_EXPERTS, tm), lambda bi, i: (bi, 0, i))],
        out_shape=[jax.ShapeDtypeStruct((b, t, d // 2 + LANES), U32), jax.ShapeDtypeStruct((b, N_EXPERTS, t), F32)],
        compiler_params=_params(("arbitrary", "arbitrary")),
        name="router",
    )(x, nw.reshape(1, d), shift, scale, wr2, rb.reshape(1, N_EXPERTS))


def _prefix_count(mask, tri):
    e, t = mask.shape
    nb = t // LANES
    stacked = jnp.concatenate([mask[:, c * LANES:(c + 1) * LANES] for c in range(nb)], axis=0).astype(BF16)
    local = jnp.dot(stacked, tri, preferred_element_type=F32)
    out = []
    off = jnp.zeros((e, 1), F32)
    for c in range(nb):
        blk = local[c * e:(c + 1) * e, :]
        out.append(blk + off)
        off = off + blk[:, LANES - 1:LANES]
    return jnp.concatenate(out, axis=1)


def _slots_two_level(sel, tri, idx_ref, cap):
    e, t = sel.shape
    nb = t // LANES
    n = nb * e
    stacked = jnp.concatenate([sel[:, c * LANES:(c + 1) * LANES] for c in range(nb)], axis=0).astype(BF16)
    local_bm = jnp.dot(stacked, tri, preferred_element_type=F32)
    ri = lax.broadcasted_iota(I32, (n, n), 0)
    ci = lax.broadcasted_iota(I32, (n, n), 1)
    perm = (ci == (ri % nb) * e + ri // nb).astype(BF16)
    local = jnp.dot(perm, local_bm.astype(BF16), preferred_element_type=F32)
    r128 = lax.broadcasted_iota(I32, (LANES, LANES), 0)
    last = (r128 == LANES - 1).astype(BF16)
    tot = jnp.dot(local.astype(BF16), last, preferred_element_type=F32)
    before = ((ri // nb == ci // nb) & (ci < ri)).astype(BF16)
    offs = jnp.dot(before, tot.astype(BF16), preferred_element_type=F32)
    cnt = local + offs
    c_end = tot + offs
    c_hi = jnp.floor(cnt * (1.0 / 64.0))
    c_lo = cnt - 64.0 * c_hi
    jrow = lax.broadcasted_iota(I32, (1, cap), 1).astype(F32)
    blk_id = lax.broadcasted_iota(I32, (nb, cap), 0).astype(F32)
    ones_b = jnp.ones((SUBLANES, nb), BF16)
    ones_r = jnp.ones((SUBLANES, LANES), BF16)
    tn = (((0,), (0,)), ((), ()))
    for ei in range(e):
        rs = slice(ei * nb, (ei + 1) * nb)
        ce = jnp.concatenate([c_end[rs, :]] * (cap // LANES), axis=1)
        done = jnp.where(ce <= jrow, 1.0, 0.0).astype(BF16)
        bj = jnp.dot(ones_b, done, preferred_element_type=F32)[0:1, :]
        pick = jnp.where(blk_id == bj, 1.0, 0.0).astype(BF16)
        row = (64.0 * lax.dot_general(c_hi[rs, :].astype(BF16), pick, tn, preferred_element_type=F32)
               + lax.dot_general(c_lo[rs, :].astype(BF16), pick, tn, preferred_element_type=F32))
        inside = jnp.where(row <= jrow, 1.0, 0.0).astype(BF16)
        off = jnp.dot(ones_r, inside, preferred_element_type=F32)[0:1, :]
        idx_ref[0, ei, :, :] = (LANES * bj + off).astype(I32)


def _select_kernel(at_ref, idx_ref, cnt_scr, icol_scr, *, cap, jt, two_level):
    a = at_ref[0]
    e, t = a.shape
    bits = lax.bitcast_convert_type(a, I32)

    def search(i, v):
        cand = v | jnp.left_shift(jnp.int32(1), 30 - i)
        cnt = jnp.sum((bits >= cand).astype(I32), axis=1, keepdims=True)
        return jnp.where(cnt >= cap, cand, v)

    thr = lax.fori_loop(0, 31, search, jnp.zeros((e, 1), I32))
    r = lax.broadcasted_iota(I32, (LANES, LANES), 0)
    s = lax.broadcasted_iota(I32, (LANES, LANES), 1)
    tri = (r <= s).astype(BF16)
    gt = bits > thr
    eq = bits == thr
    need = (cap - jnp.sum(gt.astype(I32), axis=1, keepdims=True)).astype(F32)
    sel = gt | (eq & (_prefix_count(eq.astype(F32), tri) <= need))
    if two_level:
        _slots_two_level(sel.astype(F32), tri, idx_ref, cap)
        return
    cnt = _prefix_count(sel.astype(F32), tri)
    for ei in range(e):
        cnt_scr[ei] = cnt[ei:ei + 1, :]
    icol_scr[...] = jnp.zeros_like(icol_scr)
    lane = lax.broadcasted_iota(I32, (jt, LANES), 1)

    def per_expert(ei, carry):
        for jb in range(cap // jt):
            jcol = (lax.broadcasted_iota(I32, (jt, 1), 0) + jb * jt).astype(F32)
            acc_n = jnp.zeros((jt, LANES), F32)
            for tb in range(t // LANES):
                cn = cnt_scr[ei, :, tb * LANES:(tb + 1) * LANES]
                acc_n += jnp.where(cn <= jcol, 1.0, 0.0)
            tok = jnp.sum(acc_n, axis=1, keepdims=True)
            rows = slice(jb * jt, (jb + 1) * jt)
            icol_scr[rows, :] = jnp.where(lane == ei, tok, icol_scr[rows, :])
        return carry

    lax.fori_loop(0, e, per_expert, 0)
    idx_ref[0, :, 0, :] = icol_scr[...].T[:e, :cap].astype(I32)


def _route_select(aff_t, cap):
    b, e, t = aff_t.shape
    jt = min(cap, 64)
    return pl.pallas_call(
        functools.partial(_select_kernel, cap=cap, jt=jt, two_level=(t // LANES) % SUBLANES == 0),
        grid=(b,),
        in_specs=[pl.BlockSpec((1, e, t), lambda bi: (bi, 0, 0))],
        out_specs=pl.BlockSpec((1, e, 1, cap), lambda bi: (bi, 0, 0, 0)),
        out_shape=jax.ShapeDtypeStruct((b, e, 1, cap), I32),
        scratch_shapes=[pltpu.VMEM((e, 1, t), F32), pltpu.VMEM((max(cap, LANES), LANES), F32)],
        compiler_params=_params(("arbitrary",)),
        name="route_select",
    )(aff_t)


def _moe_kernel(idxp_ref, idxn_ref, hp_ref, g2_ref, wg_ref, wu_ref, wd_ref, fnw_ref, x_hbm, o_hbm,
                xe32_scr, xe_scr, gate_scr, y_scr, yg_scr, acc_scr, sem_in, sem_out,
                *, gsz, cap, t, n_groups, n_f, final_norm):
    grp = pl.program_id(0)
    e = pl.program_id(1)
    f = pl.program_id(2)
    n_e = pl.num_programs(1)
    half = hp_ref.shape[2] - LANES
    q_rows = cap // n_f

    def in_copy(gi):
        return pltpu.make_async_copy(x_hbm.at[pl.ds(gi * gsz, gsz)], acc_scr.at[:, pl.ds(0, t)], sem_in)

    def out_copy(gi):
        return pltpu.make_async_copy(acc_scr.at[:, pl.ds(0, t)], o_hbm.at[pl.ds(gi * gsz, gsz)], sem_out)

    def unpack(expert):
        p = xe32_scr[:, :half]
        xe_scr[:, :half] = lax.bitcast_convert_type(p << 16, F32).astype(BF16)
        xe_scr[:, half:] = lax.bitcast_convert_type(p & jnp.uint32(0xFFFF0000), F32).astype(BF16)
        aff = lax.bitcast_convert_type(xe32_scr[:, half:], F32)
        lane = lax.broadcasted_iota(I32, aff.shape, 1)
        gate = jnp.sum(jnp.where(lane == expert, aff, 0.0), axis=1, keepdims=True)
        gate_scr[...] = jnp.broadcast_to(gate, gate_scr.shape)

    def gated():
        for si in range(gsz):
            rs = slice(si * cap, (si + 1) * cap)
            g2 = g2_ref[si]
            for cbk in range(y_scr.shape[1] // LANES):
                ls = slice(cbk * LANES, (cbk + 1) * LANES)
                yg_scr[rs, ls] = y_scr[rs, ls] * gate_scr[rs, :] * g2[:, ls]

    @pl.when((e == 0) & (f == 0))
    def _():
        @pl.when(grp > 0)
        def _():
            out_copy(grp - 1).wait()
        in_copy(grp).start()
        for si in range(gsz):
            def gather(j, carry):
                tkn = idxp_ref[si, 0, 0, j]
                xe32_scr[pl.ds(si * cap + j, 1), :] = hp_ref[si, pl.ds(tkn, 1), :]
                return carry
            lax.fori_loop(0, cap, gather, 0, unroll=8)
        unpack(e)
        y_scr[...] = jnp.zeros_like(y_scr)
        yg_scr[...] = jnp.zeros_like(yg_scr)
        acc_scr[:, t:t + 8, :] = jnp.zeros((gsz, 8, acc_scr.shape[2]), F32)

    @pl.when((e > 0) & (f == 0))
    def _():
        gated()
        unpack(e)

    @pl.when((e == 1) & (f == 0))
    def _():
        in_copy(grp).wait()

    xe = xe_scr[...]
    a = jnp.dot(xe, wg_ref[0], preferred_element_type=F32)
    u = jnp.dot(xe, wu_ref[0], preferred_element_type=F32)
    hid = (a * _sigmoid(a) * u).astype(BF16)
    part = jnp.dot(hid, wd_ref[0].astype(BF16), preferred_element_type=F32)
    spare = jnp.where(e == 0, 1, 0)
    for si in range(gsz):
        for jj in range(q_rows):
            j = f * q_rows + jj
            tkn = idxn_ref[si, 0, 0, j]
            xe32_scr[pl.ds(si * cap + j, 1), :] = hp_ref[si, pl.ds(tkn, 1), :]
        for jj in range(q_rows):
            j = f * q_rows + jj
            tkn = jnp.where(spare == 1, t, idxp_ref[si, 0, 0, j])
            acc_scr[si, pl.ds(tkn, 1), :] += yg_scr[pl.ds(si * cap + j, 1), :]

    y_scr[...] = jnp.where(f > 0, y_scr[...], 0.0) + part

    @pl.when((e == n_e - 1) & (f == n_f - 1))
    def _():
        gated()
        for si in range(gsz):
            def scatter(j, carry):
                tkn = idxn_ref[si, 0, 0, j]
                acc_scr[si, pl.ds(tkn, 1), :] += yg_scr[pl.ds(si * cap + j, 1), :]
                return carry
            lax.fori_loop(0, cap, scatter, 0, unroll=8)
        if final_norm:
            def norm_rows(r, carry):
                rows = pl.ds(pl.multiple_of(r * NORM_ROWS, NORM_ROWS), NORM_ROWS)
                for si in range(gsz):
                    blk = acc_scr[si, rows, :]
                    inv = lax.rsqrt(jnp.mean(blk * blk, axis=-1, keepdims=True) + EPS)
                    acc_scr[si, rows, :] = blk * inv * fnw_ref[...]
                return carry
            lax.fori_loop(0, t // NORM_ROWS, norm_rows, 0)
        out_copy(grp).start()

        @pl.when(grp == n_groups - 1)
        def _():
            out_copy(grp).wait()


def _moe_ffn(idx, hp, g2, x, wg, wu, wd, fnw, layer, gsz, tf, final_norm):
    b, t, d = x.shape
    e, cap = idx.shape[1], idx.shape[3]
    ff = wg.shape[3]
    n_groups, n_f = b // gsz, ff // tf
    hw = hp.shape[2]
    smem_idx = lambda fn: pl.BlockSpec((gsz, 1, 1, cap), fn, memory_space=pltpu.SMEM)
    return pl.pallas_call(
        functools.partial(_moe_kernel, gsz=gsz, cap=cap, t=t, n_groups=n_groups, n_f=n_f, final_norm=final_norm),
        grid=(n_groups, e, n_f),
        in_specs=[smem_idx(lambda gi, ei, fi: (gi, jnp.maximum(ei - 1, 0), 0, 0)),
                  smem_idx(lambda gi, ei, fi: (gi, jnp.minimum(ei + 1, e - 1), 0, 0)),
                  pl.BlockSpec((gsz, t, hw), lambda gi, ei, fi: (gi, 0, 0), pipeline_mode=pl.Buffered(1)),
                  pl.BlockSpec((gsz, 1, d), lambda gi, ei, fi: (gi, 0, 0)),
                  pl.BlockSpec((None, 1, d, tf), lambda gi, ei, fi: (layer, ei, 0, fi)),
                  pl.BlockSpec((None, 1, d, tf), lambda gi, ei, fi: (layer, ei, 0, fi)),
                  pl.BlockSpec((None, 1, tf, d), lambda gi, ei, fi: (layer, ei, fi, 0)),
                  pl.BlockSpec((1, d), lambda gi, ei, fi: (0, 0)),
                  pl.BlockSpec(memory_space=pl.ANY)],
        out_specs=pl.BlockSpec(memory_space=pl.ANY),
        out_shape=jax.ShapeDtypeStruct((b, t, d), F32),
        scratch_shapes=[pltpu.VMEM((gsz * cap, hw), U32), pltpu.VMEM((gsz * cap, d), BF16),
                        pltpu.VMEM((gsz * cap, LANES), F32),
                        pltpu.VMEM((gsz * cap, d), F32), pltpu.VMEM((gsz * cap, d), F32),
                        pltpu.VMEM((gsz, t + 8, d), F32),
                        pltpu.SemaphoreType.DMA(()), pltpu.SemaphoreType.DMA(())],
        compiler_params=_params(("arbitrary", "arbitrary", "arbitrary")),
        name="moe_ffn",
    )(idx, idx, hp, g2, wg, wu, wd, fnw.reshape(1, d), x)


def _stack_hi_lo(w):
    hi, lo = _split_bf16(w)
    pad = jnp.zeros((w.shape[0], LANES - 2 * w.shape[1]), BF16)
    return jnp.concatenate([hi, lo, pad], axis=1)


def _moe(x, nw, shift, scale, g2, wr2, rb, wg, wu, wd, fnw, layer, gsz, tm, final_norm=False):
    t = x.shape[1]
    cap = CAPACITY_FACTOR * t // N_EXPERTS
    hp, aff_t = _router(x, nw, shift, scale, wr2, rb, tm)
    idx = _route_select(aff_t, cap)
    return _moe_ffn(idx, hp, g2, x, wg, wu, wd, fnw, layer, gsz, tf=1024, final_norm=final_norm)


def kernel(x, c, ctx, c_ctx, ada_w, ada_b, norm1_w, norm2_w, w_in, mlstm_gate_b, mlstm_norm_w, w_mlstm_out,
           conv_dw_w, conv_dw_b, conv_ln_w, conv_ln_b, w_conv_out, sg_ln_w, sg_ln_b, sg_w, sg_b, w_sg_out, w_o,
           router_w, router_b, expert_w_gate, expert_w_up, expert_w_down, final_norm_w):
    depth = ada_w.shape[0]
    b, t, d = x.shape
    dh = d // N_HEADS
    n_state = 3 * d + N_GATE_COLS
    cond = jnp.concatenate([c, c_ctx[None, :], jnp.zeros((16 - b - 1, d), F32)], axis=0)
    wg, wu, wd = expert_w_gate.astype(BF16), expert_w_up.astype(BF16), expert_w_down
    tc = ctx.shape[1]
    w_all = _w_regroup(w_in, 3 * d, n_state)

    for layer in range(depth):
        need_ctx = layer < depth - 1
        mod = _modulation(cond, ada_w, ada_b, layer)
        lat = [m[:, None, :] for m in jnp.split(mod[:b], 6, axis=-1)]
        cx = [jnp.broadcast_to(m[None], (b, 1, d)) for m in jnp.split(mod[b:b + 1], 6, axis=-1)]

        w_gate2 = _stack_hi_lo(w_in[layer, :, 3 * d:n_state])
        z_lat, gt_lat = _in_proj(x, norm1_w[layer], lat[0], lat[1], w_all, layer, 0, 9 * d, w_gate2,
                                 mlstm_gate_b[layer], tm=2048, tn=1536)
        col0, n_ctx = (0, 9 * d) if need_ctx else (6 * d, 3 * d)
        z_ctx, gt_ctx = _in_proj(ctx.reshape(1, b * tc, d), norm1_w[layer], cx[0][:1], cx[1][:1], w_all, layer,
                                 col0, n_ctx, w_gate2, mlstm_gate_b[layer], tm=2048, tn=1536)
        z_ctx = z_ctx.reshape(b, tc, z_ctx.shape[2])
        gt_ctx = gt_ctx.reshape(N_GATE_COLS, b, tc).transpose(1, 0, 2)

        hcf, hcb, state = _mlstm(z_ctx, 2 if need_ctx else 0, gt_ctx, None, dh, ch=2)
        hlf, hlb, _ = _mlstm(z_lat, 2, gt_lat, state, dh, ch=4)

        bf = lambda a: a[layer].astype(BF16)
        p = dict(mlstm_norm_w=mlstm_norm_w[layer], w_mlstm_out=bf(w_mlstm_out), conv_dw_w=conv_dw_w[layer],
                 conv_dw_b=conv_dw_b[layer], conv_ln_w=conv_ln_w[layer], conv_ln_b=conv_ln_b[layer],
                 w_conv_out=bf(w_conv_out), sg_ln_w=sg_ln_w[layer], sg_ln_b=sg_ln_b[layer], sg_w=bf(sg_w),
                 sg_b=sg_b[layer], w_sg_out=bf(w_sg_out), w_o=bf(w_o))
        x = _mixer_out(z_lat, 0, hlf, hlb, x, lat[2], p, row_len=GRID_W, tm=512)
        if need_ctx:
            ctx = _mixer_out(z_ctx, 0, hcf, hcb, ctx, cx[2], p, row_len=ctx.shape[1], tm=256)

        wr2 = _stack_hi_lo(router_w[layer])
        x = _moe(x, norm2_w[layer], lat[3], lat[4], lat[5], wr2, router_b[layer], wg, wu, wd, final_norm_w, layer,
                 gsz=1, tm=1024, final_norm=layer == depth - 1)
        if need_ctx:
            ctx = _moe(ctx, norm2_w[layer], cx[3], cx[4], cx[5], wr2, router_b[layer], wg, wu, wd, final_norm_w,
                       layer, gsz=b, tm=256)

    return x
```

```python
import functools

import jax
import jax.numpy as jnp
from jax import lax
from jax.experimental import pallas as pl
from jax.experimental.pallas import tpu as pltpu

F32 = jnp.float32
BF16 = jnp.bfloat16
I32 = jnp.int32
U32 = jnp.uint32

EPS = 1e-6
LANES = 128
SUBLANES = 8
VMEM_LIMIT = 56 * 1024 * 1024

N_HEADS = 4
MLSTM_CHUNK = 128
N_DIRS = 2
N_GATE_COLS = N_DIRS * 2 * N_HEADS
DW_CONV_SIZE = 31
CONV_PAD = 16
SG_GROUPS = 4
SG_CHUNK = 128
N_EXPERTS = 16
CAPACITY_FACTOR = 2
GRID_W = 64
NORM_ROWS = 128


def _params(sem, vmem=VMEM_LIMIT):
    return pltpu.CompilerParams(dimension_semantics=sem, vmem_limit_bytes=vmem)


def _sigmoid(v):
    return 0.5 * jnp.tanh(0.5 * v) + 0.5


def _split_bf16(a):
    hi = a.astype(BF16)
    lo = (a - hi.astype(F32)).astype(BF16)
    return hi, lo


def _split3_bf16(a):
    hi = a.astype(BF16)
    r1 = a - hi.astype(F32)
    mid = r1.astype(BF16)
    lo = (r1 - mid.astype(F32)).astype(BF16)
    return hi, mid, lo


def _norm_mod(x, nw, shift, scale):
    ms = jnp.mean(x * x, axis=-1, keepdims=True)
    y = x * lax.rsqrt(ms + EPS) * nw
    return y * (1.0 + scale) + shift


def _modulation_kernel(a_ref, w_ref, b_ref, o_ref):
    a = a_ref[...]
    a = a * _sigmoid(a)
    a_hi, a_lo = _split_bf16(a)
    w_hi, w_lo = _split_bf16(w_ref[...])
    acc = jnp.dot(a_hi, w_hi, preferred_element_type=F32)
    acc += jnp.dot(a_hi, w_lo, preferred_element_type=F32)
    acc += jnp.dot(a_lo, w_hi, preferred_element_type=F32)
    o_ref[...] = acc + b_ref[...]


def _modulation(cond, w, b, layer, tn=1536):
    m, d = cond.shape
    n = w.shape[2]
    return pl.pallas_call(
        _modulation_kernel,
        grid=(n // tn,),
        in_specs=[pl.BlockSpec((m, d), lambda j: (0, 0)),
                  pl.BlockSpec((None, d, tn), lambda j: (layer, 0, j)),
                  pl.BlockSpec((None, 1, tn), lambda j: (layer, 0, j))],
        out_specs=pl.BlockSpec((m, tn), lambda j: (0, j)),
        out_shape=jax.ShapeDtypeStruct((m, n), F32),
        compiler_params=_params(("arbitrary",)),
        name="modulation",
    )(cond, w, b.reshape(b.shape[0], 1, n))


def _w_regroup_kernel(w_ref, o_ref, *, n_qkv, n_state):
    n_rest = w_ref.shape[1] - n_state
    o_ref[:, :n_rest] = w_ref[:, n_state:].astype(BF16)
    o_ref[:, n_rest:] = w_ref[:, :n_qkv].astype(BF16)


def _w_regroup(w_in, n_qkv, n_state, tr=256):
    nl, d, n_in = w_in.shape
    n_out = n_in - (n_state - n_qkv)
    return pl.pallas_call(
        functools.partial(_w_regroup_kernel, n_qkv=n_qkv, n_state=n_state),
        grid=(nl, d // tr),
        in_specs=[pl.BlockSpec((None, tr, n_in), lambda li, i: (li, i, 0))],
        out_specs=pl.BlockSpec((None, tr, n_out), lambda li, i: (li, i, 0)),
        out_shape=jax.ShapeDtypeStruct((nl, d, n_out), BF16),
        compiler_params=_params(("arbitrary", "arbitrary")),
        name="w_regroup",
    )(w_in)


def _in_proj_kernel(x_ref, nw_ref, sh_ref, sc_ref, w_ref, wg_ref, gb_ref, z_ref, gt_ref, h_scr):
    j = pl.program_id(2)

    @pl.when(j == 0)
    def _():
        h = _norm_mod(x_ref[0], nw_ref[...], sh_ref[0], sc_ref[0])
        h_hi, h_lo = _split_bf16(h)
        h_scr[...] = h_hi
        wg = wg_ref[...]
        raw = jnp.dot(h_hi, wg, preferred_element_type=F32) + jnp.dot(h_lo, wg, preferred_element_type=F32)
        raw_t = raw.T
        raw_t = raw_t[:N_GATE_COLS, :] + raw_t[N_GATE_COLS:2 * N_GATE_COLS, :] + gb_ref[...]
        gate_id = lax.broadcasted_iota(I32, raw_t.shape, 0)
        is_forget = ((gate_id >> 2) & 1) == 1
        logsig = jnp.minimum(raw_t, 0.0) - jnp.log(1.0 + jnp.exp(-jnp.abs(raw_t)))
        gt_ref[0] = jnp.where(is_forget, logsig, raw_t)

    z_ref[0] = jnp.dot(h_scr[...], w_ref[...], preferred_element_type=F32).astype(BF16)


def _in_proj(x, nw, shift, scale, w_all, layer, col0, n, w_gate2, gate_b, tm, tn):
    b, t, d = x.shape
    tm = min(tm, t)
    jb0 = col0 // tn
    return pl.pallas_call(
        _in_proj_kernel,
        grid=(b, t // tm, n // tn),
        in_specs=[pl.BlockSpec((1, tm, d), lambda bi, i, j: (bi, i, 0)),
                  pl.BlockSpec((1, d), lambda bi, i, j: (0, 0)),
                  pl.BlockSpec((1, 1, d), lambda bi, i, j: (bi, 0, 0)),
                  pl.BlockSpec((1, 1, d), lambda bi, i, j: (bi, 0, 0)),
                  pl.BlockSpec((None, d, tn), lambda bi, i, j: (layer, 0, jb0 + j)),
                  pl.BlockSpec((d, LANES), lambda bi, i, j: (0, 0)),
                  pl.BlockSpec((N_GATE_COLS, 1), lambda bi, i, j: (0, 0))],
        out_specs=[pl.BlockSpec((1, tm, tn), lambda bi, i, j: (bi, i, j)),
                   pl.BlockSpec((1, N_GATE_COLS, tm), lambda bi, i, j: (bi, 0, i))],
        out_shape=[jax.ShapeDtypeStruct((b, t, n), BF16),
                   jax.ShapeDtypeStruct((b, N_GATE_COLS, t), F32)],
        scratch_shapes=[pltpu.VMEM((tm, d), BF16)],
        compiler_params=_params(("arbitrary", "arbitrary", "arbitrary")),
        name="in_proj",
    )(x, nw.reshape(1, d), shift, scale, w_all, w_gate2, gate_b.reshape(N_GATE_COLS, 1))


def _mlstm_chunk(zf_ref, zb_ref, gtf_ref, gtb_ref, hf_ref, hb_ref, c_scr, n_scr, m_scr, r0, dh):
    L = MLSTM_CHUNK
    width = N_HEADS * dh
    scans = [(d, hh) for d in range(N_DIRS) for hh in range(N_HEADS)]
    n_s = len(scans)
    rows = [slice(r, r + L) for r in r0]

    row = lax.broadcasted_iota(I32, (L, L), 0)
    colm = lax.broadcasted_iota(I32, (L, L), 1)
    eye = row == colm
    eye_bf = eye.astype(BF16)
    eye3 = jnp.concatenate([eye_bf, eye_bf, eye_bf], axis=1)
    keep_t = (row <= colm, row >= colm)
    tri = (keep_t[0].astype(BF16), keep_t[1].astype(BF16))

    gts = [(gtf_ref, gtb_ref)[d][0, :, rows[d]] for d in range(N_DIRS)]
    cums = [sum(jnp.dot(part, tri[d], preferred_element_type=F32) for part in _split3_bf16(gts[d]))
            for d in range(N_DIRS)]

    q, k, v, m_prev, i_row, b_row, b_last = [], [], [], [], [], [], []
    for si, (d, hh) in enumerate(scans):
        z_ref = (zf_ref, zb_ref)[d]
        q.append(z_ref[0, rows[d], hh * dh:(hh + 1) * dh])
        k.append(z_ref[0, rows[d], width + hh * dh:width + (hh + 1) * dh] * jnp.asarray(dh ** -0.5, BF16))
        v.append(z_ref[0, rows[d], 2 * width + hh * dh:2 * width + (hh + 1) * dh])
        m_prev.append(m_scr[si][:, 0:1])
        ci = d * 2 * N_HEADS + hh
        cf = ci + N_HEADS
        i_row.append(gts[d][ci:ci + 1, :])
        b_row.append(cums[d][cf:cf + 1, :])
        b_last.append(b_row[si][:, L - 1:L] if d == 0 else b_row[si][:, 0:1])

    nt = (((1,), (1,)), ((), ()))
    s_t = [lax.dot_general(k[i], q[i], nt, preferred_element_type=F32) for i in range(n_s)]
    qc = [jnp.dot(q[i], c_scr[i].astype(BF16), preferred_element_type=F32).astype(BF16) for i in range(n_s)]
    qn = [lax.dot_general(n_scr[i].astype(BF16), q[i], nt, preferred_element_type=F32)[0:1, :] for i in range(n_s)]
    cb = []
    for i in range(n_s):
        parts = [jnp.broadcast_to(p, (LANES, L)) for p in _split3_bf16(b_row[i] - i_row[i])]
        cb.append(lax.dot_general(eye3, jnp.concatenate(parts, axis=1), nt, preferred_element_type=F32))

    for i in range(n_s):
        g_row = b_last[i] - b_row[i] + i_row[i]
        mn = jnp.maximum(b_last[i] + m_prev[i], jnp.max(g_row, axis=-1, keepdims=True))
        decay = jnp.exp(b_last[i] + m_prev[i] - mn)
        w_row = jnp.exp(g_row - mn)
        ktw = (k[i].T.astype(F32) * w_row).astype(BF16)
        kv = jnp.dot(ktw, v[i], preferred_element_type=F32)
        wk = jnp.dot(jnp.broadcast_to(w_row, (8, L)).astype(BF16), k[i], preferred_element_type=F32)
        c_new = decay * c_scr[i] + kv
        n_new = decay * n_scr[i] + wk
        c_scr[i] = c_new
        n_scr[i] = n_new
        m_scr[i] = jnp.broadcast_to(mn, (1, LANES))

    for i, (d, hh) in enumerate(scans):
        h_ref = (hf_ref, hb_ref)[d]
        log_d = jnp.where(keep_t[d], b_row[i] - cb[i], -jnp.inf)
        m_inter = b_row[i] + m_prev[i]
        m_t = jnp.maximum(m_inter, jnp.max(log_d, axis=0, keepdims=True))
        p_t = s_t[i] * jnp.exp(log_d - m_t)
        inter = jnp.exp(m_inter - m_t)
        den = jnp.sum(p_t, axis=0, keepdims=True) + inter * qn[i]
        scale = 1.0 / jnp.maximum(jnp.abs(den), jnp.exp(-m_t))
        lhs_t = jnp.concatenate([(p_t * scale).astype(BF16), jnp.where(eye, scale * inter, 0.0).astype(BF16)], axis=0)
        rhs = jnp.concatenate([v[i], qc[i]], axis=0)
        h = lax.dot_general(lhs_t, rhs, (((0,), (0,)), ((), ())), preferred_element_type=F32)
        h_ref[0, rows[d], hh * dh:(hh + 1) * dh] = h.astype(h_ref.dtype)


def _mlstm_kernel(*refs, dh, has_init, n_steps, ch, n_cast):
    refs = list(refs)
    zf_ref, zb_ref, gtf_ref, gtb_ref = refs[:4]
    n_in = 4 + (3 if has_init else 0)
    if has_init:
        c0_ref, n0_ref, m0_ref = refs[4:7]
    cast_in = refs[n_in:n_in + n_cast]
    hf_ref, hb_ref, cN_ref, nN_ref, mN_ref = refs[n_in + n_cast:n_in + n_cast + 5]
    cast_out = refs[n_in + n_cast + 5:n_in + 2 * n_cast + 5]
    c_scr, n_scr, m_scr = refs[n_in + 2 * n_cast + 5:]
    c = pl.program_id(1)
    for src, dst in zip(cast_in, cast_out):
        dst[...] = src[...].astype(dst.dtype)

    @pl.when(c == 0)
    def _():
        if has_init:
            c_scr[...] = c0_ref[0]
            n_scr[...] = n0_ref[0]
            m_scr[...] = m0_ref[0]
        else:
            c_scr[...] = jnp.zeros_like(c_scr)
            n_scr[...] = jnp.zeros_like(n_scr)
            m_scr[...] = jnp.zeros_like(m_scr)

    for sc in range(ch):
        r0 = (sc * MLSTM_CHUNK, (ch - 1 - sc) * MLSTM_CHUNK)
        _mlstm_chunk(zf_ref, zb_ref, gtf_ref, gtb_ref, hf_ref, hb_ref, c_scr, n_scr, m_scr, r0, dh)

    @pl.when(c == n_steps - 1)
    def _():
        cN_ref[0] = c_scr[...]
        nN_ref[0] = n_scr[...]
        mN_ref[0] = m_scr[...]


def _mlstm(z, zblk, gates_t, init, dh, ch, cast=(), cast_layer=0):
    b, t, _ = z.shape
    L = ch * MLSTM_CHUNK
    nc = t // L
    width = N_HEADS * dh
    ns = N_DIRS * N_HEADS
    fwd = lambda bi, c: (bi, c, 0)
    bwd = lambda bi, c: (bi, nc - 1 - c, 0)
    st4 = lambda bi, c: (bi, 0, 0, 0)
    in_specs = [pl.BlockSpec((1, L, 3 * width), lambda bi, c: (bi, c, zblk)),
                pl.BlockSpec((1, L, 3 * width), lambda bi, c: (bi, nc - 1 - c, zblk)),
                pl.BlockSpec((1, N_GATE_COLS, L), lambda bi, c: (bi, 0, c)),
                pl.BlockSpec((1, N_GATE_COLS, L), lambda bi, c: (bi, 0, nc - 1 - c))]
    args = [z, z, gates_t, gates_t]
    state_specs = [pl.BlockSpec((1, ns, dh, dh), st4), pl.BlockSpec((1, ns, 8, dh), st4),
                   pl.BlockSpec((1, ns, 1, LANES), st4)]
    state_shapes = [jax.ShapeDtypeStruct((b, ns, dh, dh), F32), jax.ShapeDtypeStruct((b, ns, 8, dh), F32),
                    jax.ShapeDtypeStruct((b, ns, 1, LANES), F32)]
    if init is not None:
        in_specs += state_specs
        args += list(init)
    cast_specs, cast_shapes = [], []
    for w in cast:
        nl, ne, wr, wc = w.shape
        slab = ne * wr // (b * nc)
        assert slab * b * nc == ne * wr and slab % 16 == 0
        in_specs.append(pl.BlockSpec((None, slab, wc), lambda bi, c: (cast_layer, bi * nc + c, 0)))
        args.append(w.reshape(nl, ne * wr, wc))
        cast_specs.append(pl.BlockSpec((slab, wc), lambda bi, c: (bi * nc + c, 0)))
        cast_shapes.append(jax.ShapeDtypeStruct((ne * wr, wc), BF16))
    outs = pl.pallas_call(
        functools.partial(_mlstm_kernel, dh=dh, has_init=init is not None, n_steps=nc, ch=ch, n_cast=len(cast)),
        grid=(b, nc),
        in_specs=in_specs,
        out_specs=[pl.BlockSpec((1, L, width), fwd), pl.BlockSpec((1, L, width), bwd)] + state_specs + cast_specs,
        out_shape=([jax.ShapeDtypeStruct((b, t, width), BF16), jax.ShapeDtypeStruct((b, t, width), BF16)]
                   + state_shapes + cast_shapes),
        scratch_shapes=[pltpu.VMEM((ns, dh, dh), F32), pltpu.VMEM((ns, 8, dh), F32), pltpu.VMEM((ns, 1, LANES), F32)],
        compiler_params=_params(("arbitrary", "arbitrary")),
        name="mlstm",
    )(*args)
    w_bf = tuple(o.reshape(w.shape[1:]) for o, w in zip(outs[5:], cast))
    return outs[0], outs[1], tuple(outs[2:5]), w_bf


def _layer_norm(v, w, b):
    mu = jnp.mean(v, axis=-1, keepdims=True)
    vc = v - mu
    var = jnp.mean(vc * vc, axis=-1, keepdims=True)
    return vc * lax.rsqrt(var + EPS) * w + b


def _gelu_tanh(v):
    return 0.5 * v * (1.0 + jnp.tanh(0.7978845608028654 * (v + 0.044715 * (v * v * v))))


def _mixer_kernel(z_ref, hf_ref, hb_ref, x_ref, g1_ref, mnw_ref, wm_ref, dww_ref, dwb_ref, clw_ref, clb_ref,
                  wc_ref, slw_ref, slb_ref, sgw_ref, sgb_ref, ws_ref, wo_ref, o_ref, pad_scr, conv_scr,
                  *, d, row_len):
    tm = z_ref.shape[1]
    dh = d // N_HEADS
    cc = d // 2
    off_conv, off_sg, off_merge = d, 2 * d, 3 * d

    heads = []
    for hh in range(N_HEADS):
        sl = slice(hh * dh, (hh + 1) * dh)
        hm = hf_ref[0, :, sl].astype(F32) + hb_ref[0, :, sl].astype(F32)
        yn = hm * lax.rsqrt(jnp.mean(hm * hm, axis=-1, keepdims=True) + EPS) * mnw_ref[:, sl]
        heads.append(yn.astype(BF16) * _sigmoid(z_ref[0, :, sl]))
    y_m = jnp.dot(jnp.concatenate(heads, axis=1), wm_ref[...], preferred_element_type=F32)

    u = z_ref[0, :, off_conv:off_conv + cc].astype(F32) * _sigmoid(z_ref[0, :, off_conv + cc:off_conv + 2 * cc]).astype(F32)
    n_rows = tm // row_len
    zpad = jnp.zeros((SUBLANES, n_rows, CONV_PAD, cc), F32)
    pad_scr[:, :, 0:CONV_PAD, :] = zpad
    pad_scr[:, :, row_len:row_len + CONV_PAD, :] = zpad
    pad_scr[:, :, row_len + CONV_PAD:row_len + 2 * CONV_PAD, :] = zpad
    for j in range(SUBLANES):
        for r in range(n_rows):
            pad_scr[j, r, CONV_PAD - j:CONV_PAD - j + row_len, :] = u[r * row_len:(r + 1) * row_len, :]
    base = CONV_PAD - DW_CONV_SIZE // 2
    for cb in range(cc // LANES):
        ls = slice(cb * LANES, (cb + 1) * LANES)
        taps = [dww_ref[kk:kk + 1, ls] for kk in range(DW_CONV_SIZE)]
        for r in range(n_rows):
            acc = jnp.zeros((row_len, LANES), F32)
            for kk in range(DW_CONV_SIZE):
                hi, lo = divmod(base + kk, SUBLANES)
                acc += pad_scr[lo, r, hi * SUBLANES:hi * SUBLANES + row_len, ls] * taps[kk]
            conv_scr[r * row_len:(r + 1) * row_len, ls] = acc
    cv = _layer_norm(conv_scr[...] + dwb_ref[...], clw_ref[...], clb_ref[...])
    cv = cv * _sigmoid(cv)
    y_c = jnp.dot(cv.astype(BF16), wc_ref[...], preferred_element_type=F32)

    su = _gelu_tanh(z_ref[0, :, off_sg:off_sg + cc].astype(F32))
    sv = _gelu_tanh(z_ref[0, :, off_sg + cc:off_sg + 2 * cc].astype(F32))
    sv = _layer_norm(sv, slw_ref[...], slb_ref[...]).astype(BF16)
    gd = cc // SG_GROUPS
    for ch in range(tm // SG_CHUNK):
        rs = slice(ch * SG_CHUNK, (ch + 1) * SG_CHUNK)
        parts = []
        for gi in range(SG_GROUPS):
            mixed = jnp.dot(sgw_ref[gi], sv[rs, gi * gd:(gi + 1) * gd], preferred_element_type=F32)
            parts.append(mixed + sgb_ref[:, gi:gi + 1])
        gated = su[rs, :] * jnp.concatenate(parts, axis=1)
        conv_scr[rs, :] = gated
    y_s = jnp.dot(conv_scr[...].astype(BF16), ws_ref[...], preferred_element_type=F32)

    gm0 = _sigmoid(z_ref[0, :, off_merge:off_merge + d])
    gm1 = _sigmoid(z_ref[0, :, off_merge + d:off_merge + 2 * d])
    gm2 = _sigmoid(z_ref[0, :, off_merge + 2 * d:off_merge + 3 * d])
    merged = gm0 * y_m.astype(BF16) + gm1 * y_c.astype(BF16) + gm2 * y_s.astype(BF16)
    y = jnp.dot(merged, wo_ref[...], preferred_element_type=F32)
    o_ref[0] = x_ref[0] + g1_ref[0] * y


def _mixer_out(z, zblk, hf, hb, x, g1, p, row_len, tm):
    b, t, d = x.shape
    cc = d // 2
    tm = min(tm, t)
    nz = 6 * d
    full = lambda a: pl.BlockSpec(a.shape, lambda bi, i: (0,) * a.ndim)
    row = lambda a: a.reshape(1, -1)
    consts = [row(p["mlstm_norm_w"]), p["w_mlstm_out"], p["conv_dw_w"], row(p["conv_dw_b"]), row(p["conv_ln_w"]),
              row(p["conv_ln_b"]), p["w_conv_out"], row(p["sg_ln_w"]), row(p["sg_ln_b"]), p["sg_w"], p["sg_b"].T,
              p["w_sg_out"], p["w_o"]]
    tok = lambda w: pl.BlockSpec((1, tm, w), lambda bi, i: (bi, i, 0))
    return pl.pallas_call(
        functools.partial(_mixer_kernel, d=d, row_len=row_len),
        grid=(b, t // tm),
        in_specs=[pl.BlockSpec((1, tm, nz), lambda bi, i: (bi, i, zblk)), tok(d), tok(d), tok(d),
                  pl.BlockSpec((1, 1, d), lambda bi, i: (bi, 0, 0))] + [full(a) for a in consts],
        out_specs=tok(d),
        out_shape=jax.ShapeDtypeStruct((b, t, d), F32),
        scratch_shapes=[pltpu.VMEM((SUBLANES, tm // row_len, row_len + 2 * CONV_PAD, cc), F32),
                        pltpu.VMEM((tm, cc), F32)],
        compiler_params=_params(("arbitrary", "arbitrary")),
        name="mixer_out",
    )(z, hf, hb, x, g1, *consts)


def _router_kernel(x_ref, nw_ref, sh_ref, sc_ref, wr_ref, rb_ref, hp_ref, at_ref):
    h = _norm_mod(x_ref[0], nw_ref[...], sh_ref[0], sc_ref[0])
    h_hi, h_lo = _split_bf16(h)
    wr = wr_ref[...]
    raw = jnp.dot(h_hi, wr, preferred_element_type=F32) + jnp.dot(h_lo, wr, preferred_element_type=F32)
    logits = raw[:, :N_EXPERTS] + raw[:, N_EXPERTS:2 * N_EXPERTS] + rb_ref[...]
    mx = jnp.max(logits, axis=-1, keepdims=True)
    ex = jnp.exp(logits - mx)
    aff = ex / jnp.sum(ex, axis=-1, keepdims=True)
    apad = jnp.concatenate([aff, jnp.zeros((aff.shape[0], LANES - N_EXPERTS), F32)], axis=1)
    at_ref[0] = apad.T[:N_EXPERTS, :]
    half = h.shape[1] // 2
    bits = lax.bitcast_convert_type(h_hi.astype(F32), U32)
    hp_ref[0, :, :half] = (bits[:, :half] >> 16) | (bits[:, half:] & jnp.uint32(0xFFFF0000))
    hp_ref[0, :, half:] = lax.bitcast_convert_type(apad, U32)


def _router(x, nw, shift, scale, wr2, rb, tm):
    b, t, d = x.shape
    tm = min(tm, t)
    return pl.pallas_call(
        _router_kernel,
        grid=(b, t // tm),
        in_specs=[pl.BlockSpec((1, tm, d), lambda bi, i: (bi, i, 0)),
                  pl.BlockSpec((1, d), lambda bi, i: (0, 0)),
                  pl.BlockSpec((1, 1, d), lambda bi, i: (bi, 0, 0)),
                  pl.BlockSpec((1, 1, d), lambda bi, i: (bi, 0, 0)),
                  pl.BlockSpec((d, LANES), lambda bi, i: (0, 0)),
                  pl.BlockSpec((1, N_EXPERTS), lambda bi, i: (0, 0))],
        out_specs=[pl.BlockSpec((1, tm, d // 2 + LANES), lambda bi, i: (bi, i, 0)),
                   pl.BlockSpec((1, N_EXPERTS, tm), lambda bi, i: (bi, 0, i))],
        out_shape=[jax.ShapeDtypeStruct((b, t, d // 2 + LANES), U32), jax.ShapeDtypeStruct((b, N_EXPERTS, t), F32)],
        compiler_params=_params(("arbitrary", "arbitrary")),
        name="router",
    )(x, nw.reshape(1, d), shift, scale, wr2, rb.reshape(1, N_EXPERTS))


def _prefix_count(mask, tri):
    e, t = mask.shape
    nb = t // LANES
    stacked = jnp.concatenate([mask[:, c * LANES:(c + 1) * LANES] for c in range(nb)], axis=0).astype(BF16)
    local = jnp.dot(stacked, tri, preferred_element_type=F32)
    out = []
    off = jnp.zeros((e, 1), F32)
    for c in range(nb):
        blk = local[c * e:(c + 1) * e, :]
        out.append(blk + off)
        off = off + blk[:, LANES - 1:LANES]
    return jnp.concatenate(out, axis=1)


def _slots_two_level(sel, tri, idx_ref, cap):
    e, t = sel.shape
    nb = t // LANES
    n = nb * e
    stacked = jnp.concatenate([sel[:, c * LANES:(c + 1) * LANES] for c in range(nb)], axis=0).astype(BF16)
    local_bm = jnp.dot(stacked, tri, preferred_element_type=F32)
    ri = lax.broadcasted_iota(I32, (n, n), 0)
    ci = lax.broadcasted_iota(I32, (n, n), 1)
    perm = (ci == (ri % nb) * e + ri // nb).astype(BF16)
    local = jnp.dot(perm, local_bm.astype(BF16), preferred_element_type=F32)
    r128 = lax.broadcasted_iota(I32, (LANES, LANES), 0)
    last = (r128 == LANES - 1).astype(BF16)
    tot = jnp.dot(local.astype(BF16), last, preferred_element_type=F32)
    before = ((ri // nb == ci // nb) & (ci < ri)).astype(BF16)
    offs = jnp.dot(before, tot.astype(BF16), preferred_element_type=F32)
    cnt = local + offs
    c_end = tot + offs
    c_hi = jnp.floor(cnt * (1.0 / 64.0))
    c_lo = cnt - 64.0 * c_hi
    jrow = lax.broadcasted_iota(I32, (1, cap), 1).astype(F32)
    blk_id = lax.broadcasted_iota(I32, (nb, cap), 0).astype(F32)
    ones_b = jnp.ones((SUBLANES, nb), BF16)
    ones_r = jnp.ones((SUBLANES, LANES), BF16)
    tn = (((0,), (0,)), ((), ()))
    for ei in range(e):
        rs = slice(ei * nb, (ei + 1) * nb)
        ce = jnp.concatenate([c_end[rs, :]] * (cap // LANES), axis=1)
        done = jnp.where(ce <= jrow, 1.0, 0.0).astype(BF16)
        bj = jnp.dot(ones_b, done, preferred_element_type=F32)[0:1, :]
        pick = jnp.where(blk_id == bj, 1.0, 0.0).astype(BF16)
        row = (64.0 * lax.dot_general(c_hi[rs, :].astype(BF16), pick, tn, preferred_element_type=F32)
               + lax.dot_general(c_lo[rs, :].astype(BF16), pick, tn, preferred_element_type=F32))
        inside = jnp.where(row <= jrow, 1.0, 0.0).astype(BF16)
        off = jnp.dot(ones_r, inside, preferred_element_type=F32)[0:1, :]
        idx_ref[0, ei, :, :] = (LANES * bj + off).astype(I32)


def _select_kernel(at_ref, idx_ref, cnt_scr, icol_scr, *, cap, jt, two_level):
    a = at_ref[0]
    e, t = a.shape
    bits = lax.bitcast_convert_type(a, I32)

    def search(i, v):
        cand = v | jnp.left_shift(jnp.int32(1), 30 - i)
        cnt = jnp.sum((bits >= cand).astype(I32), axis=1, keepdims=True)
        return jnp.where(cnt >= cap, cand, v)

    thr = lax.fori_loop(0, 31, search, jnp.zeros((e, 1), I32))
    r = lax.broadcasted_iota(I32, (LANES, LANES), 0)
    s = lax.broadcasted_iota(I32, (LANES, LANES), 1)
    tri = (r <= s).astype(BF16)
    gt = bits > thr
    eq = bits == thr
    need = (cap - jnp.sum(gt.astype(I32), axis=1, keepdims=True)).astype(F32)
    sel = gt | (eq & (_prefix_count(eq.astype(F32), tri) <= need))
    if two_level:
        _slots_two_level(sel.astype(F32), tri, idx_ref, cap)
        return
    cnt = _prefix_count(sel.astype(F32), tri)
    for ei in range(e):
        cnt_scr[ei] = cnt[ei:ei + 1, :]
    icol_scr[...] = jnp.zeros_like(icol_scr)
    lane = lax.broadcasted_iota(I32, (jt, LANES), 1)

    def per_expert(ei, carry):
        for jb in range(cap // jt):
            jcol = (lax.broadcasted_iota(I32, (jt, 1), 0) + jb * jt).astype(F32)
            acc_n = jnp.zeros((jt, LANES), F32)
            for tb in range(t // LANES):
                cn = cnt_scr[ei, :, tb * LANES:(tb + 1) * LANES]
                acc_n += jnp.where(cn <= jcol, 1.0, 0.0)
            tok = jnp.sum(acc_n, axis=1, keepdims=True)
            rows = slice(jb * jt, (jb + 1) * jt)
            icol_scr[rows, :] = jnp.where(lane == ei, tok, icol_scr[rows, :])
        return carry

    lax.fori_loop(0, e, per_expert, 0)
    idx_ref[0, :, 0, :] = icol_scr[...].T[:e, :cap].astype(I32)


def _route_select(aff_t, cap):
    b, e, t = aff_t.shape
    jt = min(cap, 64)
    return pl.pallas_call(
        functools.partial(_select_kernel, cap=cap, jt=jt, two_level=(t // LANES) % SUBLANES == 0),
        grid=(b,),
        in_specs=[pl.BlockSpec((1, e, t), lambda bi: (bi, 0, 0))],
        out_specs=pl.BlockSpec((1, e, 1, cap), lambda bi: (bi, 0, 0, 0)),
        out_shape=jax.ShapeDtypeStruct((b, e, 1, cap), I32),
        scratch_shapes=[pltpu.VMEM((e, 1, t), F32), pltpu.VMEM((max(cap, LANES), LANES), F32)],
        compiler_params=_params(("arbitrary",)),
        name="route_select",
    )(aff_t)


def _moe_kernel(idxp_ref, idxn_ref, hp_ref, g2_ref, wg_ref, wu_ref, wd_ref, fnw_ref, x_hbm, o_hbm,
                xe32_scr, xe_scr, gate_scr, y_scr, yg_scr, acc_scr, sem_in, sem_out,
                *, gsz, cap, t, n_groups, n_f, final_norm):
    grp = pl.program_id(0)
    e = pl.program_id(1)
    f = pl.program_id(2)
    n_e = pl.num_programs(1)
    half = hp_ref.shape[2] - LANES
    q_rows = cap // n_f

    def in_copy(gi):
        return pltpu.make_async_copy(x_hbm.at[pl.ds(gi * gsz, gsz)], acc_scr.at[:, pl.ds(0, t)], sem_in)

    def out_copy(gi):
        return pltpu.make_async_copy(acc_scr.at[:, pl.ds(0, t)], o_hbm.at[pl.ds(gi * gsz, gsz)], sem_out)

    def unpack(expert):
        p = xe32_scr[:, :half]
        xe_scr[:, :half] = lax.bitcast_convert_type(p << 16, F32).astype(BF16)
        xe_scr[:, half:] = lax.bitcast_convert_type(p & jnp.uint32(0xFFFF0000), F32).astype(BF16)
        aff = lax.bitcast_convert_type(xe32_scr[:, half:], F32)
        lane = lax.broadcasted_iota(I32, aff.shape, 1)
        gate = jnp.sum(jnp.where(lane == expert, aff, 0.0), axis=1, keepdims=True)
        gate_scr[...] = jnp.broadcast_to(gate, gate_scr.shape)

    def gated():
        for si in range(gsz):
            rs = slice(si * cap, (si + 1) * cap)
            g2 = g2_ref[si]
            for cbk in range(y_scr.shape[1] // LANES):
                ls = slice(cbk * LANES, (cbk + 1) * LANES)
                yg_scr[rs, ls] = y_scr[rs, ls] * gate_scr[rs, :] * g2[:, ls]

    @pl.when((e == 0) & (f == 0))
    def _():
        @pl.when(grp > 0)
        def _():
            out_copy(grp - 1).wait()
        in_copy(grp).start()
        for si in range(gsz):
            def gather(j, carry):
                tkn = idxp_ref[si, 0, 0, j]
                xe32_scr[pl.ds(si * cap + j, 1), :] = hp_ref[si, pl.ds(tkn, 1), :]
                return carry
            lax.fori_loop(0, cap, gather, 0, unroll=8)
        unpack(e)
        y_scr[...] = jnp.zeros_like(y_scr)
        yg_scr[...] = jnp.zeros_like(yg_scr)
        acc_scr[:, t:t + 8, :] = jnp.zeros((gsz, 8, acc_scr.shape[2]), F32)

    @pl.when((e > 0) & (f == 0))
    def _():
        gated()
        unpack(e)

    @pl.when((e == 1) & (f == 0))
    def _():
        in_copy(grp).wait()

    xe = xe_scr[...]
    a = jnp.dot(xe, wg_ref[0], preferred_element_type=F32)
    u = jnp.dot(xe, wu_ref[0], preferred_element_type=F32)
    hid = (a * _sigmoid(a) * u).astype(BF16)
    part = jnp.dot(hid, wd_ref[0].astype(BF16), preferred_element_type=F32)
    spare = jnp.where(e == 0, 1, 0)
    for si in range(gsz):
        for jj in range(q_rows):
            j = f * q_rows + jj
            tkn = idxn_ref[si, 0, 0, j]
            xe32_scr[pl.ds(si * cap + j, 1), :] = hp_ref[si, pl.ds(tkn, 1), :]
        for jj in range(q_rows):
            j = f * q_rows + jj
            tkn = jnp.where(spare == 1, t, idxp_ref[si, 0, 0, j])
            acc_scr[si, pl.ds(tkn, 1), :] += yg_scr[pl.ds(si * cap + j, 1), :]

    y_scr[...] = jnp.where(f > 0, y_scr[...], 0.0) + part

    @pl.when((e == n_e - 1) & (f == n_f - 1))
    def _():
        gated()
        for si in range(gsz):
            def scatter(j, carry):
                tkn = idxn_ref[si, 0, 0, j]
                acc_scr[si, pl.ds(tkn, 1), :] += yg_scr[pl.ds(si * cap + j, 1), :]
                return carry
            lax.fori_loop(0, cap, scatter, 0, unroll=8)
        if final_norm:
            def norm_rows(r, carry):
                rows = pl.ds(pl.multiple_of(r * NORM_ROWS, NORM_ROWS), NORM_ROWS)
                for si in range(gsz):
                    blk = acc_scr[si, rows, :]
                    inv = lax.rsqrt(jnp.mean(blk * blk, axis=-1, keepdims=True) + EPS)
                    acc_scr[si, rows, :] = blk * inv * fnw_ref[...]
                return carry
            lax.fori_loop(0, t // NORM_ROWS, norm_rows, 0)
        out_copy(grp).start()

        @pl.when(grp == n_groups - 1)
        def _():
            out_copy(grp).wait()


def _moe_ffn(idx, hp, g2, x, wg, wu, wd, fnw, layer, gsz, tf, final_norm):
    b, t, d = x.shape
    e, cap = idx.shape[1], idx.shape[3]
    ff = wg.shape[2]
    n_groups, n_f = b // gsz, ff // tf
    hw = hp.shape[2]
    smem_idx = lambda fn: pl.BlockSpec((gsz, 1, 1, cap), fn, memory_space=pltpu.SMEM)
    return pl.pallas_call(
        functools.partial(_moe_kernel, gsz=gsz, cap=cap, t=t, n_groups=n_groups, n_f=n_f, final_norm=final_norm),
        grid=(n_groups, e, n_f),
        in_specs=[smem_idx(lambda gi, ei, fi: (gi, jnp.maximum(ei - 1, 0), 0, 0)),
                  smem_idx(lambda gi, ei, fi: (gi, jnp.minimum(ei + 1, e - 1), 0, 0)),
                  pl.BlockSpec((gsz, t, hw), lambda gi, ei, fi: (gi, 0, 0), pipeline_mode=pl.Buffered(1)),
                  pl.BlockSpec((gsz, 1, d), lambda gi, ei, fi: (gi, 0, 0)),
                  pl.BlockSpec((1, d, tf), lambda gi, ei, fi: (ei, 0, fi)),
                  pl.BlockSpec((1, d, tf), lambda gi, ei, fi: (ei, 0, fi)),
                  pl.BlockSpec((None, 1, tf, d), lambda gi, ei, fi: (layer, ei, fi, 0)),
                  pl.BlockSpec((1, d), lambda gi, ei, fi: (0, 0)),
                  pl.BlockSpec(memory_space=pl.ANY)],
        out_specs=pl.BlockSpec(memory_space=pl.ANY),
        out_shape=jax.ShapeDtypeStruct((b, t, d), F32),
        scratch_shapes=[pltpu.VMEM((gsz * cap, hw), U32), pltpu.VMEM((gsz * cap, d), BF16),
                        pltpu.VMEM((gsz * cap, LANES), F32),
                        pltpu.VMEM((gsz * cap, d), F32), pltpu.VMEM((gsz * cap, d), F32),
                        pltpu.VMEM((gsz, t + 8, d), F32),
                        pltpu.SemaphoreType.DMA(()), pltpu.SemaphoreType.DMA(())],
        compiler_params=_params(("arbitrary", "arbitrary", "arbitrary")),
        name="moe_ffn",
    )(idx, idx, hp, g2, wg, wu, wd, fnw.reshape(1, d), x)


def _stack_hi_lo(w):
    hi, lo = _split_bf16(w)
    pad = jnp.zeros((w.shape[0], LANES - 2 * w.shape[1]), BF16)
    return jnp.concatenate([hi, lo, pad], axis=1)


def _moe(x, nw, shift, scale, g2, wr2, rb, wg, wu, wd, fnw, layer, gsz, tm, final_norm=False):
    t = x.shape[1]
    cap = CAPACITY_FACTOR * t // N_EXPERTS
    hp, aff_t = _router(x, nw, shift, scale, wr2, rb, tm)
    idx = _route_select(aff_t, cap)
    return _moe_ffn(idx, hp, g2, x, wg, wu, wd, fnw, layer, gsz, tf=1024, final_norm=final_norm)


def kernel(x, c, ctx, c_ctx, ada_w, ada_b, norm1_w, norm2_w, w_in, mlstm_gate_b, mlstm_norm_w, w_mlstm_out,
           conv_dw_w, conv_dw_b, conv_ln_w, conv_ln_b, w_conv_out, sg_ln_w, sg_ln_b, sg_w, sg_b, w_sg_out, w_o,
           router_w, router_b, expert_w_gate, expert_w_up, expert_w_down, final_norm_w):
    depth = ada_w.shape[0]
    b, t, d = x.shape
    dh = d // N_HEADS
    n_state = 3 * d + N_GATE_COLS
    cond = jnp.concatenate([c, c_ctx[None, :], jnp.zeros((16 - b - 1, d), F32)], axis=0)
    wd = expert_w_down
    tc = ctx.shape[1]
    w_all = _w_regroup(w_in, 3 * d, n_state)

    for layer in range(depth):
        need_ctx = layer < depth - 1
        mod = _modulation(cond, ada_w, ada_b, layer)
        lat = [m[:, None, :] for m in jnp.split(mod[:b], 6, axis=-1)]
        cx = [jnp.broadcast_to(m[None], (b, 1, d)) for m in jnp.split(mod[b:b + 1], 6, axis=-1)]

        w_gate2 = _stack_hi_lo(w_in[layer, :, 3 * d:n_state])
        z_lat, gt_lat = _in_proj(x, norm1_w[layer], lat[0], lat[1], w_all, layer, 0, 9 * d, w_gate2,
                                 mlstm_gate_b[layer], tm=2048, tn=1536)
        col0, n_ctx = (0, 9 * d) if need_ctx else (6 * d, 3 * d)
        z_ctx, gt_ctx = _in_proj(ctx.reshape(1, b * tc, d), norm1_w[layer], cx[0][:1], cx[1][:1], w_all, layer,
                                 col0, n_ctx, w_gate2, mlstm_gate_b[layer], tm=2048, tn=1536)
        z_ctx = z_ctx.reshape(b, tc, z_ctx.shape[2])
        gt_ctx = gt_ctx.reshape(N_GATE_COLS, b, tc).transpose(1, 0, 2)

        hcf, hcb, state, _ = _mlstm(z_ctx, 2 if need_ctx else 0, gt_ctx, None, dh, ch=2)
        hlf, hlb, _, (wg, wu) = _mlstm(z_lat, 2, gt_lat, state, dh, ch=4,
                                       cast=(expert_w_gate, expert_w_up), cast_layer=layer)

        bf = lambda a: a[layer].astype(BF16)
        p = dict(mlstm_norm_w=mlstm_norm_w[layer], w_mlstm_out=bf(w_mlstm_out), conv_dw_w=conv_dw_w[layer],
                 conv_dw_b=conv_dw_b[layer], conv_ln_w=conv_ln_w[layer], conv_ln_b=conv_ln_b[layer],
                 w_conv_out=bf(w_conv_out), sg_ln_w=sg_ln_w[layer], sg_ln_b=sg_ln_b[layer], sg_w=bf(sg_w),
                 sg_b=sg_b[layer], w_sg_out=bf(w_sg_out), w_o=bf(w_o))
        x = _mixer_out(z_lat, 0, hlf, hlb, x, lat[2], p, row_len=GRID_W, tm=512)
        if need_ctx:
            ctx = _mixer_out(z_ctx, 0, hcf, hcb, ctx, cx[2], p, row_len=ctx.shape[1], tm=256)

        wr2 = _stack_hi_lo(router_w[layer])
        x = _moe(x, norm2_w[layer], lat[3], lat[4], lat[5], wr2, router_b[layer], wg, wu, wd, final_norm_w, layer,
                 gsz=1, tm=1024, final_norm=layer == depth - 1)
        if need_ctx:
            ctx = _moe(ctx, norm2_w[layer], cx[3], cx[4], cx[5], wr2, router_b[layer], wg, wu, wd, final_norm_w,
                       layer, gsz=b, tm=256)

    return x
```

```python
import functools

import jax
import jax.numpy as jnp
from jax import lax
from jax.experimental import pallas as pl
from jax.experimental.pallas import tpu as pltpu

F32 = jnp.float32
BF16 = jnp.bfloat16
I32 = jnp.int32
U32 = jnp.uint32

EPS = 1e-6
LANES = 128
SUBLANES = 8
VMEM_LIMIT = 56 * 1024 * 1024

N_HEADS = 4
MLSTM_CHUNK = 128
N_DIRS = 2
N_GATE_COLS = N_DIRS * 2 * N_HEADS
DW_CONV_SIZE = 31
CONV_PAD = 16
SG_GROUPS = 4
SG_CHUNK = 128
N_EXPERTS = 16
CAPACITY_FACTOR = 2
GRID_W = 64
NORM_ROWS = 128


def _params(sem, vmem=VMEM_LIMIT):
    return pltpu.CompilerParams(dimension_semantics=sem, vmem_limit_bytes=vmem)


def _sigmoid(v):
    return 0.5 * jnp.tanh(0.5 * v) + 0.5


def _split_bf16(a):
    hi = a.astype(BF16)
    lo = (a - hi.astype(F32)).astype(BF16)
    return hi, lo


def _split3_bf16(a):
    hi = a.astype(BF16)
    r1 = a - hi.astype(F32)
    mid = r1.astype(BF16)
    lo = (r1 - mid.astype(F32)).astype(BF16)
    return hi, mid, lo


def _norm_mod(x, nw, shift, scale):
    ms = jnp.mean(x * x, axis=-1, keepdims=True)
    y = x * lax.rsqrt(ms + EPS) * nw
    return y * (1.0 + scale) + shift


def _modulation_kernel(a_ref, w_ref, b_ref, o_ref):
    a = a_ref[...]
    a = a * _sigmoid(a)
    a_hi, a_lo = _split_bf16(a)
    w_hi, w_lo = _split_bf16(w_ref[...])
    acc = jnp.dot(a_hi, w_hi, preferred_element_type=F32)
    acc += jnp.dot(a_hi, w_lo, preferred_element_type=F32)
    acc += jnp.dot(a_lo, w_hi, preferred_element_type=F32)
    o_ref[...] = acc + b_ref[...]


def _modulation(cond, w, b, layer, tn=1536):
    m, d = cond.shape
    n = w.shape[2]
    return pl.pallas_call(
        _modulation_kernel,
        grid=(n // tn,),
        in_specs=[pl.BlockSpec((m, d), lambda j: (0, 0)),
                  pl.BlockSpec((None, d, tn), lambda j: (layer, 0, j)),
                  pl.BlockSpec((None, 1, tn), lambda j: (layer, 0, j))],
        out_specs=pl.BlockSpec((m, tn), lambda j: (0, j)),
        out_shape=jax.ShapeDtypeStruct((m, n), F32),
        compiler_params=_params(("arbitrary",)),
        name="modulation",
    )(cond, w, b.reshape(b.shape[0], 1, n))


def _w_regroup_kernel(w_ref, o_ref, *, n_qkv, n_state):
    n_rest = w_ref.shape[1] - n_state
    o_ref[:, :n_rest] = w_ref[:, n_state:].astype(BF16)
    o_ref[:, n_rest:] = w_ref[:, :n_qkv].astype(BF16)


def _w_regroup(w_in, n_qkv, n_state, tr=256):
    nl, d, n_in = w_in.shape
    n_out = n_in - (n_state - n_qkv)
    return pl.pallas_call(
        functools.partial(_w_regroup_kernel, n_qkv=n_qkv, n_state=n_state),
        grid=(nl, d // tr),
        in_specs=[pl.BlockSpec((None, tr, n_in), lambda li, i: (li, i, 0))],
        out_specs=pl.BlockSpec((None, tr, n_out), lambda li, i: (li, i, 0)),
        out_shape=jax.ShapeDtypeStruct((nl, d, n_out), BF16),
        compiler_params=_params(("arbitrary", "arbitrary")),
        name="w_regroup",
    )(w_in)


def _in_proj_kernel(x_ref, nw_ref, sh_ref, sc_ref, w_ref, wg_ref, gb_ref, z_ref, gt_ref, h_scr):
    j = pl.program_id(2)

    @pl.when(j == 0)
    def _():
        h = _norm_mod(x_ref[0], nw_ref[...], sh_ref[0], sc_ref[0])
        h_hi, h_lo = _split_bf16(h)
        h_scr[...] = h_hi
        wg = wg_ref[...]
        raw = jnp.dot(h_hi, wg, preferred_element_type=F32) + jnp.dot(h_lo, wg, preferred_element_type=F32)
        raw_t = raw.T
        raw_t = raw_t[:N_GATE_COLS, :] + raw_t[N_GATE_COLS:2 * N_GATE_COLS, :] + gb_ref[...]
        gate_id = lax.broadcasted_iota(I32, raw_t.shape, 0)
        is_forget = ((gate_id >> 2) & 1) == 1
        logsig = jnp.minimum(raw_t, 0.0) - jnp.log(1.0 + jnp.exp(-jnp.abs(raw_t)))
        gt_ref[0] = jnp.where(is_forget, logsig, raw_t)

    z_ref[0] = jnp.dot(h_scr[...], w_ref[...], preferred_element_type=F32).astype(BF16)


def _in_proj(x, nw, shift, scale, w_all, layer, col0, n, w_gate2, gate_b, tm, tn):
    b, t, d = x.shape
    tm = min(tm, t)
    jb0 = col0 // tn
    return pl.pallas_call(
        _in_proj_kernel,
        grid=(b, t // tm, n // tn),
        in_specs=[pl.BlockSpec((1, tm, d), lambda bi, i, j: (bi, i, 0)),
                  pl.BlockSpec((1, d), lambda bi, i, j: (0, 0)),
                  pl.BlockSpec((1, 1, d), lambda bi, i, j: (bi, 0, 0)),
                  pl.BlockSpec((1, 1, d), lambda bi, i, j: (bi, 0, 0)),
                  pl.BlockSpec((None, d, tn), lambda bi, i, j: (layer, 0, jb0 + j)),
                  pl.BlockSpec((d, LANES), lambda bi, i, j: (0, 0)),
                  pl.BlockSpec((N_GATE_COLS, 1), lambda bi, i, j: (0, 0))],
        out_specs=[pl.BlockSpec((1, tm, tn), lambda bi, i, j: (bi, i, j)),
                   pl.BlockSpec((1, N_GATE_COLS, tm), lambda bi, i, j: (bi, 0, i))],
        out_shape=[jax.ShapeDtypeStruct((b, t, n), BF16),
                   jax.ShapeDtypeStruct((b, N_GATE_COLS, t), F32)],
        scratch_shapes=[pltpu.VMEM((tm, d), BF16)],
        compiler_params=_params(("arbitrary", "arbitrary", "arbitrary")),
        name="in_proj",
    )(x, nw.reshape(1, d), shift, scale, w_all, w_gate2, gate_b.reshape(N_GATE_COLS, 1))


def _mlstm_chunk(zf_ref, zb_ref, gtf_ref, gtb_ref, hf_ref, hb_ref, c_scr, n_scr, m_scr, r0, dh):
    L = MLSTM_CHUNK
    width = N_HEADS * dh
    scans = [(d, hh) for d in range(N_DIRS) for hh in range(N_HEADS)]
    n_s = len(scans)
    rows = [slice(r, r + L) for r in r0]

    row = lax.broadcasted_iota(I32, (L, L), 0)
    colm = lax.broadcasted_iota(I32, (L, L), 1)
    eye = row == colm
    eye_bf = eye.astype(BF16)
    eye3 = jnp.concatenate([eye_bf, eye_bf, eye_bf], axis=1)
    keep_t = (row <= colm, row >= colm)
    tri = (keep_t[0].astype(BF16), keep_t[1].astype(BF16))

    gts = [(gtf_ref, gtb_ref)[d][0, :, rows[d]] for d in range(N_DIRS)]
    cums = [sum(jnp.dot(part, tri[d], preferred_element_type=F32) for part in _split3_bf16(gts[d]))
            for d in range(N_DIRS)]

    q, k, v, m_prev, i_row, b_row, b_last = [], [], [], [], [], [], []
    for si, (d, hh) in enumerate(scans):
        z_ref = (zf_ref, zb_ref)[d]
        q.append(z_ref[0, rows[d], hh * dh:(hh + 1) * dh])
        k.append(z_ref[0, rows[d], width + hh * dh:width + (hh + 1) * dh] * jnp.asarray(dh ** -0.5, BF16))
        v.append(z_ref[0, rows[d], 2 * width + hh * dh:2 * width + (hh + 1) * dh])
        m_prev.append(m_scr[si][:, 0:1])
        ci = d * 2 * N_HEADS + hh
        cf = ci + N_HEADS
        i_row.append(gts[d][ci:ci + 1, :])
        b_row.append(cums[d][cf:cf + 1, :])
        b_last.append(b_row[si][:, L - 1:L] if d == 0 else b_row[si][:, 0:1])

    nt = (((1,), (1,)), ((), ()))
    s_t = [lax.dot_general(k[i], q[i], nt, preferred_element_type=F32) for i in range(n_s)]
    qc = [jnp.dot(q[i], c_scr[i].astype(BF16), preferred_element_type=F32).astype(BF16) for i in range(n_s)]
    qn = [lax.dot_general(n_scr[i].astype(BF16), q[i], nt, preferred_element_type=F32)[0:1, :] for i in range(n_s)]
    cb = []
    for i in range(n_s):
        parts = [jnp.broadcast_to(p, (LANES, L)) for p in _split3_bf16(b_row[i] - i_row[i])]
        cb.append(lax.dot_general(eye3, jnp.concatenate(parts, axis=1), nt, preferred_element_type=F32))

    for i in range(n_s):
        g_row = b_last[i] - b_row[i] + i_row[i]
        mn = jnp.maximum(b_last[i] + m_prev[i], jnp.max(g_row, axis=-1, keepdims=True))
        decay = jnp.exp(b_last[i] + m_prev[i] - mn)
        w_row = jnp.exp(g_row - mn)
        ktw = (k[i].T.astype(F32) * w_row).astype(BF16)
        kv = jnp.dot(ktw, v[i], preferred_element_type=F32)
        wk = jnp.dot(jnp.broadcast_to(w_row, (8, L)).astype(BF16), k[i], preferred_element_type=F32)
        c_new = decay * c_scr[i] + kv
        n_new = decay * n_scr[i] + wk
        c_scr[i] = c_new
        n_scr[i] = n_new
        m_scr[i] = jnp.broadcast_to(mn, (1, LANES))

    for i, (d, hh) in enumerate(scans):
        h_ref = (hf_ref, hb_ref)[d]
        log_d = jnp.where(keep_t[d], b_row[i] - cb[i], -jnp.inf)
        m_inter = b_row[i] + m_prev[i]
        m_t = jnp.maximum(m_inter, jnp.max(log_d, axis=0, keepdims=True))
        p_t = s_t[i] * jnp.exp(log_d - m_t)
        inter = jnp.exp(m_inter - m_t)
        den = jnp.sum(p_t, axis=0, keepdims=True) + inter * qn[i]
        scale = 1.0 / jnp.maximum(jnp.abs(den), jnp.exp(-m_t))
        lhs_t = jnp.concatenate([(p_t * scale).astype(BF16), jnp.where(eye, scale * inter, 0.0).astype(BF16)], axis=0)
        rhs = jnp.concatenate([v[i], qc[i]], axis=0)
        h = lax.dot_general(lhs_t, rhs, (((0,), (0,)), ((), ())), preferred_element_type=F32)
        h_ref[0, rows[d], hh * dh:(hh + 1) * dh] = h.astype(h_ref.dtype)


def _mlstm_kernel(*refs, dh, has_init, n_steps, ch, n_cast):
    refs = list(refs)
    zf_ref, zb_ref, gtf_ref, gtb_ref = refs[:4]
    n_in = 4 + (3 if has_init else 0)
    if has_init:
        c0_ref, n0_ref, m0_ref = refs[4:7]
    cast_in = refs[n_in:n_in + n_cast]
    hf_ref, hb_ref, cN_ref, nN_ref, mN_ref = refs[n_in + n_cast:n_in + n_cast + 5]
    cast_out = refs[n_in + n_cast + 5:n_in + 2 * n_cast + 5]
    c_scr, n_scr, m_scr = refs[n_in + 2 * n_cast + 5:]
    c = pl.program_id(1)
    for src, dst in zip(cast_in, cast_out):
        dst[...] = src[...].astype(dst.dtype)

    @pl.when(c == 0)
    def _():
        if has_init:
            c_scr[...] = c0_ref[0]
            n_scr[...] = n0_ref[0]
            m_scr[...] = m0_ref[0]
        else:
            c_scr[...] = jnp.zeros_like(c_scr)
            n_scr[...] = jnp.zeros_like(n_scr)
            m_scr[...] = jnp.zeros_like(m_scr)

    for sc in range(ch):
        r0 = (sc * MLSTM_CHUNK, (ch - 1 - sc) * MLSTM_CHUNK)
        _mlstm_chunk(zf_ref, zb_ref, gtf_ref, gtb_ref, hf_ref, hb_ref, c_scr, n_scr, m_scr, r0, dh)

    @pl.when(c == n_steps - 1)
    def _():
        cN_ref[0] = c_scr[...]
        nN_ref[0] = n_scr[...]
        mN_ref[0] = m_scr[...]


def _mlstm(z, zblk, gates_t, init, dh, ch, cast=(), cast_layer=0):
    b, t, _ = z.shape
    L = ch * MLSTM_CHUNK
    nc = t // L
    width = N_HEADS * dh
    ns = N_DIRS * N_HEADS
    fwd = lambda bi, c: (bi, c, 0)
    bwd = lambda bi, c: (bi, nc - 1 - c, 0)
    st4 = lambda bi, c: (bi, 0, 0, 0)
    in_specs = [pl.BlockSpec((1, L, 3 * width), lambda bi, c: (bi, c, zblk)),
                pl.BlockSpec((1, L, 3 * width), lambda bi, c: (bi, nc - 1 - c, zblk)),
                pl.BlockSpec((1, N_GATE_COLS, L), lambda bi, c: (bi, 0, c)),
                pl.BlockSpec((1, N_GATE_COLS, L), lambda bi, c: (bi, 0, nc - 1 - c))]
    args = [z, z, gates_t, gates_t]
    state_specs = [pl.BlockSpec((1, ns, dh, dh), st4), pl.BlockSpec((1, ns, 8, dh), st4),
                   pl.BlockSpec((1, ns, 1, LANES), st4)]
    state_shapes = [jax.ShapeDtypeStruct((b, ns, dh, dh), F32), jax.ShapeDtypeStruct((b, ns, 8, dh), F32),
                    jax.ShapeDtypeStruct((b, ns, 1, LANES), F32)]
    if init is not None:
        in_specs += state_specs
        args += list(init)
    c_in, c_args, cast_specs, cast_shapes = _cast_operands(cast, cast_layer, b, nc)
    in_specs += c_in
    args += c_args
    outs = pl.pallas_call(
        functools.partial(_mlstm_kernel, dh=dh, has_init=init is not None, n_steps=nc, ch=ch, n_cast=len(cast)),
        grid=(b, nc),
        in_specs=in_specs,
        out_specs=[pl.BlockSpec((1, L, width), fwd), pl.BlockSpec((1, L, width), bwd)] + state_specs + cast_specs,
        out_shape=([jax.ShapeDtypeStruct((b, t, width), BF16), jax.ShapeDtypeStruct((b, t, width), BF16)]
                   + state_shapes + cast_shapes),
        scratch_shapes=[pltpu.VMEM((ns, dh, dh), F32), pltpu.VMEM((ns, 8, dh), F32), pltpu.VMEM((ns, 1, LANES), F32)],
        compiler_params=_params(("arbitrary", "arbitrary")),
        name="mlstm",
    )(*args)
    w_bf = tuple(o.reshape(w.shape[1:]) for o, w in zip(outs[5:], cast))
    return outs[0], outs[1], tuple(outs[2:5]), w_bf


def _layer_norm(v, w, b):
    mu = jnp.mean(v, axis=-1, keepdims=True)
    vc = v - mu
    var = jnp.mean(vc * vc, axis=-1, keepdims=True)
    return vc * lax.rsqrt(var + EPS) * w + b


def _gelu_tanh(v):
    return 0.5 * v * (1.0 + jnp.tanh(0.7978845608028654 * (v + 0.044715 * (v * v * v))))


def _mixer_kernel(*refs, d, row_len, n_cast):
    (z_ref, hf_ref, hb_ref, x_ref, g1_ref, mnw_ref, wm_ref, dww_ref, dwb_ref, clw_ref, clb_ref,
     wc_ref, slw_ref, slb_ref, sgw_ref, sgb_ref, ws_ref, wo_ref) = refs[:18]
    cast_in = refs[18:18 + n_cast]
    o_ref = refs[18 + n_cast]
    cast_out = refs[19 + n_cast:19 + 2 * n_cast]
    pad_scr, conv_scr = refs[19 + 2 * n_cast:]
    for src, dst in zip(cast_in, cast_out):
        dst[...] = src[...].astype(dst.dtype)
    tm = z_ref.shape[1]
    dh = d // N_HEADS
    cc = d // 2
    off_conv, off_sg, off_merge = d, 2 * d, 3 * d

    heads = []
    for hh in range(N_HEADS):
        sl = slice(hh * dh, (hh + 1) * dh)
        hm = hf_ref[0, :, sl].astype(F32) + hb_ref[0, :, sl].astype(F32)
        yn = hm * lax.rsqrt(jnp.mean(hm * hm, axis=-1, keepdims=True) + EPS) * mnw_ref[:, sl]
        heads.append(yn.astype(BF16) * _sigmoid(z_ref[0, :, sl]))
    y_m = jnp.dot(jnp.concatenate(heads, axis=1), wm_ref[...], preferred_element_type=F32)

    u = z_ref[0, :, off_conv:off_conv + cc].astype(F32) * _sigmoid(z_ref[0, :, off_conv + cc:off_conv + 2 * cc]).astype(F32)
    n_rows = tm // row_len
    zpad = jnp.zeros((SUBLANES, n_rows, CONV_PAD, cc), F32)
    pad_scr[:, :, 0:CONV_PAD, :] = zpad
    pad_scr[:, :, row_len:row_len + CONV_PAD, :] = zpad
    pad_scr[:, :, row_len + CONV_PAD:row_len + 2 * CONV_PAD, :] = zpad
    for j in range(SUBLANES):
        for r in range(n_rows):
            pad_scr[j, r, CONV_PAD - j:CONV_PAD - j + row_len, :] = u[r * row_len:(r + 1) * row_len, :]
    base = CONV_PAD - DW_CONV_SIZE // 2
    for cb in range(cc // LANES):
        ls = slice(cb * LANES, (cb + 1) * LANES)
        taps = [dww_ref[kk:kk + 1, ls] for kk in range(DW_CONV_SIZE)]
        for r in range(n_rows):
            acc = jnp.zeros((row_len, LANES), F32)
            for kk in range(DW_CONV_SIZE):
                hi, lo = divmod(base + kk, SUBLANES)
                acc += pad_scr[lo, r, hi * SUBLANES:hi * SUBLANES + row_len, ls] * taps[kk]
            conv_scr[r * row_len:(r + 1) * row_len, ls] = acc
    cv = _layer_norm(conv_scr[...] + dwb_ref[...], clw_ref[...], clb_ref[...])
    cv = cv * _sigmoid(cv)
    y_c = jnp.dot(cv.astype(BF16), wc_ref[...], preferred_element_type=F32)

    su = _gelu_tanh(z_ref[0, :, off_sg:off_sg + cc].astype(F32))
    sv = _gelu_tanh(z_ref[0, :, off_sg + cc:off_sg + 2 * cc].astype(F32))
    sv = _layer_norm(sv, slw_ref[...], slb_ref[...]).astype(BF16)
    gd = cc // SG_GROUPS
    for ch in range(tm // SG_CHUNK):
        rs = slice(ch * SG_CHUNK, (ch + 1) * SG_CHUNK)
        parts = []
        for gi in range(SG_GROUPS):
            mixed = jnp.dot(sgw_ref[gi], sv[rs, gi * gd:(gi + 1) * gd], preferred_element_type=F32)
            parts.append(mixed + sgb_ref[:, gi:gi + 1])
        gated = su[rs, :] * jnp.concatenate(parts, axis=1)
        conv_scr[rs, :] = gated
    y_s = jnp.dot(conv_scr[...].astype(BF16), ws_ref[...], preferred_element_type=F32)

    gm0 = _sigmoid(z_ref[0, :, off_merge:off_merge + d])
    gm1 = _sigmoid(z_ref[0, :, off_merge + d:off_merge + 2 * d])
    gm2 = _sigmoid(z_ref[0, :, off_merge + 2 * d:off_merge + 3 * d])
    merged = gm0 * y_m.astype(BF16) + gm1 * y_c.astype(BF16) + gm2 * y_s.astype(BF16)
    y = jnp.dot(merged, wo_ref[...], preferred_element_type=F32)
    o_ref[0] = x_ref[0] + g1_ref[0] * y


def _cast_operands(cast, layer, n_a, n_b):
    in_specs, args, out_specs, out_shapes = [], [], [], []
    for w in cast:
        nl, ne, wr, wc = w.shape
        slab = ne * wr // (n_a * n_b)
        assert slab * n_a * n_b == ne * wr and slab % 16 == 0
        in_specs.append(pl.BlockSpec((None, slab, wc), lambda ai, bi: (layer, ai * n_b + bi, 0)))
        args.append(w.reshape(nl, ne * wr, wc))
        out_specs.append(pl.BlockSpec((slab, wc), lambda ai, bi: (ai * n_b + bi, 0)))
        out_shapes.append(jax.ShapeDtypeStruct((ne * wr, wc), BF16))
    return in_specs, args, out_specs, out_shapes


def _mixer_out(z, zblk, hf, hb, x, g1, p, row_len, tm, cast=(), cast_layer=0):
    b, t, d = x.shape
    cc = d // 2
    tm = min(tm, t)
    nz = 6 * d
    c_in, c_args, c_out, c_shapes = _cast_operands(cast, cast_layer, b, t // tm)
    full = lambda a: pl.BlockSpec(a.shape, lambda bi, i: (0,) * a.ndim)
    row = lambda a: a.reshape(1, -1)
    consts = [row(p["mlstm_norm_w"]), p["w_mlstm_out"], p["conv_dw_w"], row(p["conv_dw_b"]), row(p["conv_ln_w"]),
              row(p["conv_ln_b"]), p["w_conv_out"], row(p["sg_ln_w"]), row(p["sg_ln_b"]), p["sg_w"], p["sg_b"].T,
              p["w_sg_out"], p["w_o"]]
    tok = lambda w: pl.BlockSpec((1, tm, w), lambda bi, i: (bi, i, 0))
    outs = pl.pallas_call(
        functools.partial(_mixer_kernel, d=d, row_len=row_len, n_cast=len(cast)),
        grid=(b, t // tm),
        in_specs=[pl.BlockSpec((1, tm, nz), lambda bi, i: (bi, i, zblk)), tok(d), tok(d), tok(d),
                  pl.BlockSpec((1, 1, d), lambda bi, i: (bi, 0, 0))] + [full(a) for a in consts] + c_in,
        out_specs=[tok(d)] + c_out,
        out_shape=[jax.ShapeDtypeStruct((b, t, d), F32)] + c_shapes,
        scratch_shapes=[pltpu.VMEM((SUBLANES, tm // row_len, row_len + 2 * CONV_PAD, cc), F32),
                        pltpu.VMEM((tm, cc), F32)],
        compiler_params=_params(("arbitrary", "arbitrary")),
        name="mixer_out",
    )(z, hf, hb, x, g1, *consts, *c_args)
    return outs[0], tuple(o.reshape(w.shape[1:]) for o, w in zip(outs[1:], cast))


def _router_kernel(x_ref, nw_ref, sh_ref, sc_ref, wr_ref, rb_ref, hp_ref, at_ref):
    h = _norm_mod(x_ref[0], nw_ref[...], sh_ref[0], sc_ref[0])
    h_hi, h_lo = _split_bf16(h)
    wr = wr_ref[...]
    raw = jnp.dot(h_hi, wr, preferred_element_type=F32) + jnp.dot(h_lo, wr, preferred_element_type=F32)
    logits = raw[:, :N_EXPERTS] + raw[:, N_EXPERTS:2 * N_EXPERTS] + rb_ref[...]
    mx = jnp.max(logits, axis=-1, keepdims=True)
    ex = jnp.exp(logits - mx)
    aff = ex / jnp.sum(ex, axis=-1, keepdims=True)
    apad = jnp.concatenate([aff, jnp.zeros((aff.shape[0], LANES - N_EXPERTS), F32)], axis=1)
    at_ref[0] = apad.T[:N_EXPERTS, :]
    half = h.shape[1] // 2
    bits = lax.bitcast_convert_type(h_hi.astype(F32), U32)
    hp_ref[0, :, :half] = (bits[:, :half] >> 16) | (bits[:, half:] & jnp.uint32(0xFFFF0000))
    hp_ref[0, :, half:] = lax.bitcast_convert_type(apad, U32)


def _router(x, nw, shift, scale, wr2, rb, tm):
    b, t, d = x.shape
    tm = min(tm, t)
    return pl.pallas_call(
        _router_kernel,
        grid=(b, t // tm),
        in_specs=[pl.BlockSpec((1, tm, d), lambda bi, i: (bi, i, 0)),
                  pl.BlockSpec((1, d), lambda bi, i: (0, 0)),
                  pl.BlockSpec((1, 1, d), lambda bi, i: (bi, 0, 0)),
                  pl.BlockSpec((1, 1, d), lambda bi, i: (bi, 0, 0)),
                  pl.BlockSpec((d, LANES), lambda bi, i: (0, 0)),
                  pl.BlockSpec((1, N_EXPERTS), lambda bi, i: (0, 0))],
        out_specs=[pl.BlockSpec((1, tm, d // 2 + LANES), lambda bi, i: (bi, i, 0)),
                   pl.BlockSpec((1, N_EXPERTS, tm), lambda bi, i: (bi, 0, i))],
        out_shape=[jax.ShapeDtypeStruct((b, t, d // 2 + LANES), U32), jax.ShapeDtypeStruct((b, N_EXPERTS, t), F32)],
        compiler_params=_params(("arbitrary", "arbitrary")),
        name="router",
    )(x, nw.reshape(1, d), shift, scale, wr2, rb.reshape(1, N_EXPERTS))


def _prefix_count(mask, tri):
    e, t = mask.shape
    nb = t // LANES
    stacked = jnp.concatenate([mask[:, c * LANES:(c + 1) * LANES] for c in range(nb)], axis=0).astype(BF16)
    local = jnp.dot(stacked, tri, preferred_element_type=F32)
    out = []
    off = jnp.zeros((e, 1), F32)
    for c in range(nb):
        blk = local[c * e:(c + 1) * e, :]
        out.append(blk + off)
        off = off + blk[:, LANES - 1:LANES]
    return jnp.concatenate(out, axis=1)


def _slots_two_level(sel, tri, idx_ref, cap):
    e, t = sel.shape
    nb = t // LANES
    n = nb * e
    stacked = jnp.concatenate([sel[:, c * LANES:(c + 1) * LANES] for c in range(nb)], axis=0).astype(BF16)
    local_bm = jnp.dot(stacked, tri, preferred_element_type=F32)
    ri = lax.broadcasted_iota(I32, (n, n), 0)
    ci = lax.broadcasted_iota(I32, (n, n), 1)
    perm = (ci == (ri % nb) * e + ri // nb).astype(BF16)
    local = jnp.dot(perm, local_bm.astype(BF16), preferred_element_type=F32)
    r128 = lax.broadcasted_iota(I32, (LANES, LANES), 0)
    last = (r128 == LANES - 1).astype(BF16)
    tot = jnp.dot(local.astype(BF16), last, preferred_element_type=F32)
    before = ((ri // nb == ci // nb) & (ci < ri)).astype(BF16)
    offs = jnp.dot(before, tot.astype(BF16), preferred_element_type=F32)
    cnt = local + offs
    c_end = tot + offs
    c_hi = jnp.floor(cnt * (1.0 / 64.0))
    c_lo = cnt - 64.0 * c_hi
    jrow = lax.broadcasted_iota(I32, (1, cap), 1).astype(F32)
    blk_id = lax.broadcasted_iota(I32, (nb, cap), 0).astype(F32)
    ones_b = jnp.ones((SUBLANES, nb), BF16)
    ones_r = jnp.ones((SUBLANES, LANES), BF16)
    tn = (((0,), (0,)), ((), ()))
    for ei in range(e):
        rs = slice(ei * nb, (ei + 1) * nb)
        ce = jnp.concatenate([c_end[rs, :]] * (cap // LANES), axis=1)
        done = jnp.where(ce <= jrow, 1.0, 0.0).astype(BF16)
        bj = jnp.dot(ones_b, done, preferred_element_type=F32)[0:1, :]
        pick = jnp.where(blk_id == bj, 1.0, 0.0).astype(BF16)
        row = (64.0 * lax.dot_general(c_hi[rs, :].astype(BF16), pick, tn, preferred_element_type=F32)
               + lax.dot_general(c_lo[rs, :].astype(BF16), pick, tn, preferred_element_type=F32))
        inside = jnp.where(row <= jrow, 1.0, 0.0).astype(BF16)
        off = jnp.dot(ones_r, inside, preferred_element_type=F32)[0:1, :]
        idx_ref[0, ei, :, :] = (LANES * bj + off).astype(I32)


def _select_kernel(at_ref, idx_ref, cnt_scr, icol_scr, *, cap, jt, two_level):
    a = at_ref[0]
    e, t = a.shape
    bits = lax.bitcast_convert_type(a, I32)

    def search(i, v):
        cand = v | jnp.left_shift(jnp.int32(1), 30 - i)
        cnt = jnp.sum((bits >= cand).astype(I32), axis=1, keepdims=True)
        return jnp.where(cnt >= cap, cand, v)

    thr = lax.fori_loop(0, 31, search, jnp.zeros((e, 1), I32))
    r = lax.broadcasted_iota(I32, (LANES, LANES), 0)
    s = lax.broadcasted_iota(I32, (LANES, LANES), 1)
    tri = (r <= s).astype(BF16)
    gt = bits > thr
    eq = bits == thr
    need = (cap - jnp.sum(gt.astype(I32), axis=1, keepdims=True)).astype(F32)
    sel = gt | (eq & (_prefix_count(eq.astype(F32), tri) <= need))
    if two_level:
        _slots_two_level(sel.astype(F32), tri, idx_ref, cap)
        return
    cnt = _prefix_count(sel.astype(F32), tri)
    for ei in range(e):
        cnt_scr[ei] = cnt[ei:ei + 1, :]
    icol_scr[...] = jnp.zeros_like(icol_scr)
    lane = lax.broadcasted_iota(I32, (jt, LANES), 1)

    def per_expert(ei, carry):
        for jb in range(cap // jt):
            jcol = (lax.broadcasted_iota(I32, (jt, 1), 0) + jb * jt).astype(F32)
            acc_n = jnp.zeros((jt, LANES), F32)
            for tb in range(t // LANES):
                cn = cnt_scr[ei, :, tb * LANES:(tb + 1) * LANES]
                acc_n += jnp.where(cn <= jcol, 1.0, 0.0)
            tok = jnp.sum(acc_n, axis=1, keepdims=True)
            rows = slice(jb * jt, (jb + 1) * jt)
            icol_scr[rows, :] = jnp.where(lane == ei, tok, icol_scr[rows, :])
        return carry

    lax.fori_loop(0, e, per_expert, 0)
    idx_ref[0, :, 0, :] = icol_scr[...].T[:e, :cap].astype(I32)


def _route_select(aff_t, cap):
    b, e, t = aff_t.shape
    jt = min(cap, 64)
    return pl.pallas_call(
        functools.partial(_select_kernel, cap=cap, jt=jt, two_level=(t // LANES) % SUBLANES == 0),
        grid=(b,),
        in_specs=[pl.BlockSpec((1, e, t), lambda bi: (bi, 0, 0))],
        out_specs=pl.BlockSpec((1, e, 1, cap), lambda bi: (bi, 0, 0, 0)),
        out_shape=jax.ShapeDtypeStruct((b, e, 1, cap), I32),
        scratch_shapes=[pltpu.VMEM((e, 1, t), F32), pltpu.VMEM((max(cap, LANES), LANES), F32)],
        compiler_params=_params(("arbitrary",)),
        name="route_select",
    )(aff_t)


def _moe_kernel(idxp_ref, idxn_ref, hp_ref, g2_ref, wg_ref, wu_ref, wd_ref, fnw_ref, x_hbm, o_hbm,
                xe32_scr, xe_scr, gate_scr, y_scr, yg_scr, acc_scr, sem_in, sem_out,
                *, gsz, cap, t, n_groups, n_f, final_norm):
    grp = pl.program_id(0)
    e = pl.program_id(1)
    f = pl.program_id(2)
    n_e = pl.num_programs(1)
    half = hp_ref.shape[2] - LANES
    q_rows = cap // n_f

    def in_copy(gi):
        return pltpu.make_async_copy(x_hbm.at[pl.ds(gi * gsz, gsz)], acc_scr.at[:, pl.ds(0, t)], sem_in)

    def out_copy(gi):
        return pltpu.make_async_copy(acc_scr.at[:, pl.ds(0, t)], o_hbm.at[pl.ds(gi * gsz, gsz)], sem_out)

    def unpack(expert):
        p = xe32_scr[:, :half]
        xe_scr[:, :half] = lax.bitcast_convert_type(p << 16, F32).astype(BF16)
        xe_scr[:, half:] = lax.bitcast_convert_type(p & jnp.uint32(0xFFFF0000), F32).astype(BF16)
        aff = lax.bitcast_convert_type(xe32_scr[:, half:], F32)
        lane = lax.broadcasted_iota(I32, aff.shape, 1)
        gate = jnp.sum(jnp.where(lane == expert, aff, 0.0), axis=1, keepdims=True)
        gate_scr[...] = jnp.broadcast_to(gate, gate_scr.shape)

    def gated():
        for si in range(gsz):
            rs = slice(si * cap, (si + 1) * cap)
            g2 = g2_ref[si]
            for cbk in range(y_scr.shape[1] // LANES):
                ls = slice(cbk * LANES, (cbk + 1) * LANES)
                yg_scr[rs, ls] = y_scr[rs, ls] * gate_scr[rs, :] * g2[:, ls]

    @pl.when((e == 0) & (f == 1))
    def _():
        @pl.when(grp > 0)
        def _():
            out_copy(grp - 1).wait()
        in_copy(grp).start()

    @pl.when((e == 0) & (f == 0))
    def _():
        for si in range(gsz):
            def gather(j, carry):
                tkn = idxp_ref[si, 0, 0, j]
                xe32_scr[pl.ds(si * cap + j, 1), :] = hp_ref[si, pl.ds(tkn, 1), :]
                return carry
            lax.fori_loop(0, cap, gather, 0, unroll=8)
        unpack(e)
        y_scr[...] = jnp.zeros_like(y_scr)
        yg_scr[...] = jnp.zeros_like(yg_scr)
        acc_scr[:, t:t + 8, :] = jnp.zeros((gsz, 8, acc_scr.shape[2]), F32)

    @pl.when((e > 0) & (f == 0))
    def _():
        gated()
        unpack(e)

    @pl.when((e == 1) & (f == 0))
    def _():
        in_copy(grp).wait()

    xe = xe_scr[...]
    a = jnp.dot(xe, wg_ref[0], preferred_element_type=F32)
    u = jnp.dot(xe, wu_ref[0], preferred_element_type=F32)
    hid = (a * _sigmoid(a) * u).astype(BF16)
    part = jnp.dot(hid, wd_ref[0], preferred_element_type=F32)
    spare = jnp.where(e == 0, 1, 0)
    for si in range(gsz):
        for jj in range(q_rows):
            j = f * q_rows + jj
            tkn = idxn_ref[si, 0, 0, j]
            xe32_scr[pl.ds(si * cap + j, 1), :] = hp_ref[si, pl.ds(tkn, 1), :]
        for jj in range(q_rows):
            j = f * q_rows + jj
            tkn = jnp.where(spare == 1, t, idxp_ref[si, 0, 0, j])
            acc_scr[si, pl.ds(tkn, 1), :] += yg_scr[pl.ds(si * cap + j, 1), :]

    y_scr[...] = jnp.where(f > 0, y_scr[...], 0.0) + part

    @pl.when((e == n_e - 1) & (f == n_f - 1))
    def _():
        gated()
        for si in range(gsz):
            def scatter(j, carry):
                tkn = idxn_ref[si, 0, 0, j]
                acc_scr[si, pl.ds(tkn, 1), :] += yg_scr[pl.ds(si * cap + j, 1), :]
                return carry
            lax.fori_loop(0, cap, scatter, 0, unroll=8)
        if final_norm:
            def norm_rows(r, carry):
                rows = pl.ds(pl.multiple_of(r * NORM_ROWS, NORM_ROWS), NORM_ROWS)
                for si in range(gsz):
                    blk = acc_scr[si, rows, :]
                    inv = lax.rsqrt(jnp.mean(blk * blk, axis=-1, keepdims=True) + EPS)
                    acc_scr[si, rows, :] = blk * inv * fnw_ref[...]
                return carry
            lax.fori_loop(0, t // NORM_ROWS, norm_rows, 0)
        out_copy(grp).start()

        @pl.when(grp == n_groups - 1)
        def _():
            out_copy(grp).wait()


def _moe_ffn(idx, hp, g2, x, wg, wu, wd, fnw, gsz, tf, final_norm):
    b, t, d = x.shape
    e, cap = idx.shape[1], idx.shape[3]
    ff = wg.shape[2]
    n_groups, n_f = b // gsz, ff // tf
    hw = hp.shape[2]
    smem_idx = lambda fn: pl.BlockSpec((gsz, 1, 1, cap), fn, memory_space=pltpu.SMEM)
    return pl.pallas_call(
        functools.partial(_moe_kernel, gsz=gsz, cap=cap, t=t, n_groups=n_groups, n_f=n_f, final_norm=final_norm),
        grid=(n_groups, e, n_f),
        in_specs=[smem_idx(lambda gi, ei, fi: (gi, jnp.maximum(ei - 1, 0), 0, 0)),
                  smem_idx(lambda gi, ei, fi: (gi, jnp.minimum(ei + 1, e - 1), 0, 0)),
                  pl.BlockSpec((gsz, t, hw), lambda gi, ei, fi: (gi, 0, 0), pipeline_mode=pl.Buffered(1)),
                  pl.BlockSpec((gsz, 1, d), lambda gi, ei, fi: (gi, 0, 0)),
                  pl.BlockSpec((1, d, tf), lambda gi, ei, fi: (ei, 0, fi)),
                  pl.BlockSpec((1, d, tf), lambda gi, ei, fi: (ei, 0, fi)),
                  pl.BlockSpec((1, tf, d), lambda gi, ei, fi: (ei, fi, 0)),
                  pl.BlockSpec((1, d), lambda gi, ei, fi: (0, 0)),
                  pl.BlockSpec(memory_space=pl.ANY)],
        out_specs=pl.BlockSpec(memory_space=pl.ANY),
        out_shape=jax.ShapeDtypeStruct((b, t, d), F32),
        scratch_shapes=[pltpu.VMEM((gsz * cap, hw), U32), pltpu.VMEM((gsz * cap, d), BF16),
                        pltpu.VMEM((gsz * cap, LANES), F32),
                        pltpu.VMEM((gsz * cap, d), F32), pltpu.VMEM((gsz * cap, d), F32),
                        pltpu.VMEM((gsz, t + 8, d), F32),
                        pltpu.SemaphoreType.DMA(()), pltpu.SemaphoreType.DMA(())],
        compiler_params=_params(("arbitrary", "arbitrary", "arbitrary")),
        name="moe_ffn",
    )(idx, idx, hp, g2, wg, wu, wd, fnw.reshape(1, d), x)


def _stack_hi_lo(w):
    hi, lo = _split_bf16(w)
    pad = jnp.zeros((w.shape[0], LANES - 2 * w.shape[1]), BF16)
    return jnp.concatenate([hi, lo, pad], axis=1)


def _moe(x, nw, shift, scale, g2, wr2, rb, wg, wu, wd, fnw, gsz, tm, final_norm=False):
    t = x.shape[1]
    cap = CAPACITY_FACTOR * t // N_EXPERTS
    hp, aff_t = _router(x, nw, shift, scale, wr2, rb, tm)
    idx = _route_select(aff_t, cap)
    return _moe_ffn(idx, hp, g2, x, wg, wu, wd, fnw, gsz, tf=1024, final_norm=final_norm)


def kernel(x, c, ctx, c_ctx, ada_w, ada_b, norm1_w, norm2_w, w_in, mlstm_gate_b, mlstm_norm_w, w_mlstm_out,
           conv_dw_w, conv_dw_b, conv_ln_w, conv_ln_b, w_conv_out, sg_ln_w, sg_ln_b, sg_w, sg_b, w_sg_out, w_o,
           router_w, router_b, expert_w_gate, expert_w_up, expert_w_down, final_norm_w):
    depth = ada_w.shape[0]
    b, t, d = x.shape
    dh = d // N_HEADS
    n_state = 3 * d + N_GATE_COLS
    cond = jnp.concatenate([c, c_ctx[None, :], jnp.zeros((16 - b - 1, d), F32)], axis=0)
    tc = ctx.shape[1]
    w_all = _w_regroup(w_in, 3 * d, n_state)

    for layer in range(depth):
        need_ctx = layer < depth - 1
        mod = _modulation(cond, ada_w, ada_b, layer)
        lat = [m[:, None, :] for m in jnp.split(mod[:b], 6, axis=-1)]
        cx = [jnp.broadcast_to(m[None], (b, 1, d)) for m in jnp.split(mod[b:b + 1], 6, axis=-1)]

        w_gate2 = _stack_hi_lo(w_in[layer, :, 3 * d:n_state])
        z_lat, gt_lat = _in_proj(x, norm1_w[layer], lat[0], lat[1], w_all, layer, 0, 9 * d, w_gate2,
                                 mlstm_gate_b[layer], tm=2048, tn=1536)
        col0, n_ctx = (0, 9 * d) if need_ctx else (6 * d, 3 * d)
        z_ctx, gt_ctx = _in_proj(ctx.reshape(1, b * tc, d), norm1_w[layer], cx[0][:1], cx[1][:1], w_all, layer,
                                 col0, n_ctx, w_gate2, mlstm_gate_b[layer], tm=2048, tn=1536)
        z_ctx = z_ctx.reshape(b, tc, z_ctx.shape[2])
        gt_ctx = gt_ctx.reshape(N_GATE_COLS, b, tc).transpose(1, 0, 2)

        hcf, hcb, state, _ = _mlstm(z_ctx, 2 if need_ctx else 0, gt_ctx, None, dh, ch=2)
        hlf, hlb, _, (wg, wu) = _mlstm(z_lat, 2, gt_lat, state, dh, ch=4,
                                       cast=(expert_w_gate, expert_w_up), cast_layer=layer)

        bf = lambda a: a[layer].astype(BF16)
        p = dict(mlstm_norm_w=mlstm_norm_w[layer], w_mlstm_out=bf(w_mlstm_out), conv_dw_w=conv_dw_w[layer],
                 conv_dw_b=conv_dw_b[layer], conv_ln_w=conv_ln_w[layer], conv_ln_b=conv_ln_b[layer],
                 w_conv_out=bf(w_conv_out), sg_ln_w=sg_ln_w[layer], sg_ln_b=sg_ln_b[layer], sg_w=bf(sg_w),
                 sg_b=sg_b[layer], w_sg_out=bf(w_sg_out), w_o=bf(w_o))
        x, (wd,) = _mixer_out(z_lat, 0, hlf, hlb, x, lat[2], p, row_len=GRID_W, tm=512,
                              cast=(expert_w_down,), cast_layer=layer)
        if need_ctx:
            ctx, _ = _mixer_out(z_ctx, 0, hcf, hcb, ctx, cx[2], p, row_len=ctx.shape[1], tm=256)

        wr2 = _stack_hi_lo(router_w[layer])
        x = _moe(x, norm2_w[layer], lat[3], lat[4], lat[5], wr2, router_b[layer], wg, wu, wd, final_norm_w,
                 gsz=1, tm=1024, final_norm=layer == depth - 1)
        if need_ctx:
            ctx = _moe(ctx, norm2_w[layer], cx[3], cx[4], cx[5], wr2, router_b[layer], wg, wu, wd, final_norm_w,
                       gsz=b, tm=256)

    return x
```

```python
import functools

import jax
import jax.numpy as jnp
from jax import lax
from jax.experimental import pallas as pl
from jax.experimental.pallas import tpu as pltpu

F32 = jnp.float32
BF16 = jnp.bfloat16
I32 = jnp.int32
U32 = jnp.uint32

EPS = 1e-6
LANES = 128
SUBLANES = 8
VMEM_LIMIT = 56 * 1024 * 1024
IN_PROJ_VMEM_LIMIT = 60 * 1024 * 1024

N_HEADS = 4
MLSTM_CHUNK = 128
N_DIRS = 2
N_GATE_COLS = N_DIRS * 2 * N_HEADS
DW_CONV_SIZE = 31
CONV_PAD = 16
SG_GROUPS = 4
SG_CHUNK = 128
N_EXPERTS = 16
CAPACITY_FACTOR = 2
GRID_W = 64
NORM_ROWS = 128


def _params(sem, vmem=VMEM_LIMIT):
    return pltpu.CompilerParams(dimension_semantics=sem, vmem_limit_bytes=vmem)


def _sigmoid(v):
    return 0.5 * jnp.tanh(0.5 * v) + 0.5


def _split_bf16(a):
    hi = a.astype(BF16)
    lo = (a - hi.astype(F32)).astype(BF16)
    return hi, lo


def _split3_bf16(a):
    hi = a.astype(BF16)
    r1 = a - hi.astype(F32)
    mid = r1.astype(BF16)
    lo = (r1 - mid.astype(F32)).astype(BF16)
    return hi, mid, lo


def _norm_mod(x, nw, shift, scale):
    ms = jnp.mean(x * x, axis=-1, keepdims=True)
    y = x * lax.rsqrt(ms + EPS) * nw
    return y * (1.0 + scale) + shift


def _modulation_kernel(a_ref, w_ref, b_ref, o_ref):
    a = a_ref[...]
    a = a * _sigmoid(a)
    a_hi, a_lo = _split_bf16(a)
    w_hi, w_lo = _split_bf16(w_ref[...])
    acc = jnp.dot(a_hi, w_hi, preferred_element_type=F32)
    acc += jnp.dot(a_hi, w_lo, preferred_element_type=F32)
    acc += jnp.dot(a_lo, w_hi, preferred_element_type=F32)
    o_ref[...] = acc + b_ref[...]


def _modulation(cond, w, b, layer, tn=1536):
    m, d = cond.shape
    n = w.shape[2]
    return pl.pallas_call(
        _modulation_kernel,
        grid=(n // tn,),
        in_specs=[pl.BlockSpec((m, d), lambda j: (0, 0)),
                  pl.BlockSpec((None, d, tn), lambda j: (layer, 0, j)),
                  pl.BlockSpec((None, 1, tn), lambda j: (layer, 0, j))],
        out_specs=pl.BlockSpec((m, tn), lambda j: (0, j)),
        out_shape=jax.ShapeDtypeStruct((m, n), F32),
        compiler_params=_params(("arbitrary",)),
        name="modulation",
    )(cond, w, b.reshape(b.shape[0], 1, n))


def _w_regroup_kernel(w_ref, o_ref, *, n_qkv, n_state):
    n_rest = w_ref.shape[1] - n_state
    o_ref[:, :n_rest] = w_ref[:, n_state:].astype(BF16)
    o_ref[:, n_rest:] = w_ref[:, :n_qkv].astype(BF16)


def _w_regroup(w_in, n_qkv, n_state, tr=256):
    nl, d, n_in = w_in.shape
    n_out = n_in - (n_state - n_qkv)
    return pl.pallas_call(
        functools.partial(_w_regroup_kernel, n_qkv=n_qkv, n_state=n_state),
        grid=(nl, d // tr),
        in_specs=[pl.BlockSpec((None, tr, n_in), lambda li, i: (li, i, 0))],
        out_specs=pl.BlockSpec((None, tr, n_out), lambda li, i: (li, i, 0)),
        out_shape=jax.ShapeDtypeStruct((nl, d, n_out), BF16),
        compiler_params=_params(("arbitrary", "arbitrary")),
        name="w_regroup",
    )(w_in)


def _in_proj_kernel(x_ref, nw_ref, sh_ref, sc_ref, w_ref, wg_ref, gb_ref, z_ref, gt_ref, h_scr):
    j = pl.program_id(2)

    @pl.when(j == 0)
    def _():
        h = _norm_mod(x_ref[0], nw_ref[...], sh_ref[0], sc_ref[0])
        h_hi, h_lo = _split_bf16(h)
        h_scr[...] = h_hi
        wg = wg_ref[...]
        raw = jnp.dot(h_hi, wg, preferred_element_type=F32) + jnp.dot(h_lo, wg, preferred_element_type=F32)
        raw_t = raw.T
        raw_t = raw_t[:N_GATE_COLS, :] + raw_t[N_GATE_COLS:2 * N_GATE_COLS, :] + gb_ref[...]
        gate_id = lax.broadcasted_iota(I32, raw_t.shape, 0)
        is_forget = ((gate_id >> 2) & 1) == 1
        logsig = jnp.minimum(raw_t, 0.0) - jnp.log(1.0 + jnp.exp(-jnp.abs(raw_t)))
        gt_ref[0] = jnp.where(is_forget, logsig, raw_t)

    z_ref[0] = jnp.dot(h_scr[...], w_ref[...], preferred_element_type=F32).astype(BF16)


def _in_proj(x, nw, shift, scale, w_all, layer, col0, n, w_gate2, gate_b, tm, tn):
    b, t, d = x.shape
    tm = min(tm, t)
    jb0 = col0 // tn
    return pl.pallas_call(
        _in_proj_kernel,
        grid=(b, t // tm, n // tn),
        in_specs=[pl.BlockSpec((1, tm, d), lambda bi, i, j: (bi, i, 0)),
                  pl.BlockSpec((1, d), lambda bi, i, j: (0, 0)),
                  pl.BlockSpec((1, 1, d), lambda bi, i, j: (bi, 0, 0)),
                  pl.BlockSpec((1, 1, d), lambda bi, i, j: (bi, 0, 0)),
                  pl.BlockSpec((None, d, tn), lambda bi, i, j: (layer, 0, jb0 + j)),
                  pl.BlockSpec((d, LANES), lambda bi, i, j: (0, 0)),
                  pl.BlockSpec((N_GATE_COLS, 1), lambda bi, i, j: (0, 0))],
        out_specs=[pl.BlockSpec((1, tm, tn), lambda bi, i, j: (bi, i, j)),
                   pl.BlockSpec((1, N_GATE_COLS, tm), lambda bi, i, j: (bi, 0, i))],
        out_shape=[jax.ShapeDtypeStruct((b, t, n), BF16),
                   jax.ShapeDtypeStruct((b, N_GATE_COLS, t), F32)],
        scratch_shapes=[pltpu.VMEM((tm, d), BF16)],
        compiler_params=_params(("arbitrary", "arbitrary", "arbitrary"), vmem=IN_PROJ_VMEM_LIMIT),
        name="in_proj",
    )(x, nw.reshape(1, d), shift, scale, w_all, w_gate2, gate_b.reshape(N_GATE_COLS, 1))


def _mlstm_chunk(zf_ref, zb_ref, gtf_ref, gtb_ref, hf_ref, hb_ref, c_scr, n_scr, m_scr, r0, dh):
    L = MLSTM_CHUNK
    width = N_HEADS * dh
    scans = [(d, hh) for d in range(N_DIRS) for hh in range(N_HEADS)]
    n_s = len(scans)
    rows = [slice(r, r + L) for r in r0]

    row = lax.broadcasted_iota(I32, (L, L), 0)
    colm = lax.broadcasted_iota(I32, (L, L), 1)
    eye = row == colm
    eye_bf = eye.astype(BF16)
    eye3 = jnp.concatenate([eye_bf, eye_bf, eye_bf], axis=1)
    keep_t = (row <= colm, row >= colm)
    tri = (keep_t[0].astype(BF16), keep_t[1].astype(BF16))

    gts = [(gtf_ref, gtb_ref)[d][0, :, rows[d]] for d in range(N_DIRS)]
    cums = [sum(jnp.dot(part, tri[d], preferred_element_type=F32) for part in _split3_bf16(gts[d]))
            for d in range(N_DIRS)]

    q, k, v, m_prev, i_row, b_row, b_last = [], [], [], [], [], [], []
    for si, (d, hh) in enumerate(scans):
        z_ref = (zf_ref, zb_ref)[d]
        q.append(z_ref[0, rows[d], hh * dh:(hh + 1) * dh])
        k.append(z_ref[0, rows[d], width + hh * dh:width + (hh + 1) * dh] * jnp.asarray(dh ** -0.5, BF16))
        v.append(z_ref[0, rows[d], 2 * width + hh * dh:2 * width + (hh + 1) * dh])
        m_prev.append(m_scr[si][:, 0:1])
        ci = d * 2 * N_HEADS + hh
        cf = ci + N_HEADS
        i_row.append(gts[d][ci:ci + 1, :])
        b_row.append(cums[d][cf:cf + 1, :])
        b_last.append(b_row[si][:, L - 1:L] if d == 0 else b_row[si][:, 0:1])

    nt = (((1,), (1,)), ((), ()))
    s_t = [lax.dot_general(k[i], q[i], nt, preferred_element_type=F32) for i in range(n_s)]
    qc = [jnp.dot(q[i], c_scr[i].astype(BF16), preferred_element_type=F32).astype(BF16) for i in range(n_s)]
    qn = [lax.dot_general(n_scr[i].astype(BF16), q[i], nt, preferred_element_type=F32)[0:1, :] for i in range(n_s)]
    cb = []
    for i in range(n_s):
        parts = [jnp.broadcast_to(p, (LANES, L)) for p in _split3_bf16(b_row[i] - i_row[i])]
        cb.append(lax.dot_general(eye3, jnp.concatenate(parts, axis=1), nt, preferred_element_type=F32))

    for i in range(n_s):
        g_row = b_last[i] - b_row[i] + i_row[i]
        mn = jnp.maximum(b_last[i] + m_prev[i], jnp.max(g_row, axis=-1, keepdims=True))
        decay = jnp.exp(b_last[i] + m_prev[i] - mn)
        w_row = jnp.exp(g_row - mn)
        ktw = (k[i].T.astype(F32) * w_row).astype(BF16)
        kv = jnp.dot(ktw, v[i], preferred_element_type=F32)
        wk = jnp.dot(jnp.broadcast_to(w_row, (8, L)).astype(BF16), k[i], preferred_element_type=F32)
        c_new = decay * c_scr[i] + kv
        n_new = decay * n_scr[i] + wk
        c_scr[i] = c_new
        n_scr[i] = n_new
        m_scr[i] = jnp.broadcast_to(mn, (1, LANES))

    for i, (d, hh) in enumerate(scans):
        h_ref = (hf_ref, hb_ref)[d]
        log_d = jnp.where(keep_t[d], b_row[i] - cb[i], -jnp.inf)
        m_inter = b_row[i] + m_prev[i]
        m_t = jnp.maximum(m_inter, jnp.max(log_d, axis=0, keepdims=True))
        p_t = s_t[i] * jnp.exp(log_d - m_t)
        inter = jnp.exp(m_inter - m_t)
        den = jnp.sum(p_t, axis=0, keepdims=True) + inter * qn[i]
        scale = 1.0 / jnp.maximum(jnp.abs(den), jnp.exp(-m_t))
        lhs_t = jnp.concatenate([(p_t * scale).astype(BF16), jnp.where(eye, scale * inter, 0.0).astype(BF16)], axis=0)
        rhs = jnp.concatenate([v[i], qc[i]], axis=0)
        h = lax.dot_general(lhs_t, rhs, (((0,), (0,)), ((), ())), preferred_element_type=F32)
        h_ref[0, rows[d], hh * dh:(hh + 1) * dh] = h.astype(h_ref.dtype)


def _mlstm_kernel(*refs, dh, has_init, n_steps, ch, n_cast):
    refs = list(refs)
    zf_ref, zb_ref, gtf_ref, gtb_ref = refs[:4]
    n_in = 4 + (3 if has_init else 0)
    if has_init:
        c0_ref, n0_ref, m0_ref = refs[4:7]
    cast_in = refs[n_in:n_in + n_cast]
    hf_ref, hb_ref, cN_ref, nN_ref, mN_ref = refs[n_in + n_cast:n_in + n_cast + 5]
    cast_out = refs[n_in + n_cast + 5:n_in + 2 * n_cast + 5]
    c_scr, n_scr, m_scr = refs[n_in + 2 * n_cast + 5:]
    c = pl.program_id(1)
    for src, dst in zip(cast_in, cast_out):
        dst[...] = src[...].astype(dst.dtype)

    @pl.when(c == 0)
    def _():
        if has_init:
            c_scr[...] = c0_ref[0]
            n_scr[...] = n0_ref[0]
            m_scr[...] = m0_ref[0]
        else:
            c_scr[...] = jnp.zeros_like(c_scr)
            n_scr[...] = jnp.zeros_like(n_scr)
            m_scr[...] = jnp.zeros_like(m_scr)

    for sc in range(ch):
        r0 = (sc * MLSTM_CHUNK, (ch - 1 - sc) * MLSTM_CHUNK)
        _mlstm_chunk(zf_ref, zb_ref, gtf_ref, gtb_ref, hf_ref, hb_ref, c_scr, n_scr, m_scr, r0, dh)

    @pl.when(c == n_steps - 1)
    def _():
        cN_ref[0] = c_scr[...]
        nN_ref[0] = n_scr[...]
        mN_ref[0] = m_scr[...]


def _mlstm(z, zblk, gates_t, init, dh, ch, cast=(), cast_layer=0):
    b, t, _ = z.shape
    L = ch * MLSTM_CHUNK
    nc = t // L
    width = N_HEADS * dh
    ns = N_DIRS * N_HEADS
    fwd = lambda bi, c: (bi, c, 0)
    bwd = lambda bi, c: (bi, nc - 1 - c, 0)
    st4 = lambda bi, c: (bi, 0, 0, 0)
    in_specs = [pl.BlockSpec((1, L, 3 * width), lambda bi, c: (bi, c, zblk)),
                pl.BlockSpec((1, L, 3 * width), lambda bi, c: (bi, nc - 1 - c, zblk)),
                pl.BlockSpec((1, N_GATE_COLS, L), lambda bi, c: (bi, 0, c)),
                pl.BlockSpec((1, N_GATE_COLS, L), lambda bi, c: (bi, 0, nc - 1 - c))]
    args = [z, z, gates_t, gates_t]
    state_specs = [pl.BlockSpec((1, ns, dh, dh), st4), pl.BlockSpec((1, ns, 8, dh), st4),
                   pl.BlockSpec((1, ns, 1, LANES), st4)]
    state_shapes = [jax.ShapeDtypeStruct((b, ns, dh, dh), F32), jax.ShapeDtypeStruct((b, ns, 8, dh), F32),
                    jax.ShapeDtypeStruct((b, ns, 1, LANES), F32)]
    if init is not None:
        in_specs += state_specs
        args += list(init)
    cast_specs, cast_shapes = [], []
    for w in cast:
        nl, ne, wr, wc = w.shape
        slab = ne * wr // (b * nc)
        assert slab * b * nc == ne * wr and slab % 16 == 0
        in_specs.append(pl.BlockSpec((None, slab, wc), lambda bi, c: (cast_layer, bi * nc + c, 0)))
        args.append(w.reshape(nl, ne * wr, wc))
        cast_specs.append(pl.BlockSpec((slab, wc), lambda bi, c: (bi * nc + c, 0)))
        cast_shapes.append(jax.ShapeDtypeStruct((ne * wr, wc), BF16))
    outs = pl.pallas_call(
        functools.partial(_mlstm_kernel, dh=dh, has_init=init is not None, n_steps=nc, ch=ch, n_cast=len(cast)),
        grid=(b, nc),
        in_specs=in_specs,
        out_specs=[pl.BlockSpec((1, L, width), fwd), pl.BlockSpec((1, L, width), bwd)] + state_specs + cast_specs,
        out_shape=([jax.ShapeDtypeStruct((b, t, width), BF16), jax.ShapeDtypeStruct((b, t, width), BF16)]
                   + state_shapes + cast_shapes),
        scratch_shapes=[pltpu.VMEM((ns, dh, dh), F32), pltpu.VMEM((ns, 8, dh), F32), pltpu.VMEM((ns, 1, LANES), F32)],
        compiler_params=_params(("arbitrary", "arbitrary")),
        name="mlstm",
    )(*args)
    w_bf = tuple(o.reshape(w.shape[1:]) for o, w in zip(outs[5:], cast))
    return outs[0], outs[1], tuple(outs[2:5]), w_bf


def _layer_norm(v, w, b):
    mu = jnp.mean(v, axis=-1, keepdims=True)
    vc = v - mu
    var = jnp.mean(vc * vc, axis=-1, keepdims=True)
    return vc * lax.rsqrt(var + EPS) * w + b


def _gelu_tanh(v):
    return 0.5 * v * (1.0 + jnp.tanh(0.7978845608028654 * (v + 0.044715 * (v * v * v))))


def _mixer_kernel(z_ref, hf_ref, hb_ref, x_ref, g1_ref, mnw_ref, wm_ref, dww_ref, dwb_ref, clw_ref, clb_ref,
                  wc_ref, slw_ref, slb_ref, sgw_ref, sgb_ref, ws_ref, wo_ref, o_ref, pad_scr, conv_scr,
                  *, d, row_len):
    tm = z_ref.shape[1]
    dh = d // N_HEADS
    cc = d // 2
    off_conv, off_sg, off_merge = d, 2 * d, 3 * d

    heads = []
    for hh in range(N_HEADS):
        sl = slice(hh * dh, (hh + 1) * dh)
        hm = hf_ref[0, :, sl].astype(F32) + hb_ref[0, :, sl].astype(F32)
        yn = hm * lax.rsqrt(jnp.mean(hm * hm, axis=-1, keepdims=True) + EPS) * mnw_ref[:, sl]
        heads.append(yn.astype(BF16) * _sigmoid(z_ref[0, :, sl]))
    y_m = jnp.dot(jnp.concatenate(heads, axis=1), wm_ref[...], preferred_element_type=F32)

    u = z_ref[0, :, off_conv:off_conv + cc].astype(F32) * _sigmoid(z_ref[0, :, off_conv + cc:off_conv + 2 * cc]).astype(F32)
    n_rows = tm // row_len
    zpad = jnp.zeros((SUBLANES, n_rows, CONV_PAD, cc), F32)
    pad_scr[:, :, 0:CONV_PAD, :] = zpad
    pad_scr[:, :, row_len:row_len + CONV_PAD, :] = zpad
    pad_scr[:, :, row_len + CONV_PAD:row_len + 2 * CONV_PAD, :] = zpad
    for j in range(SUBLANES):
        for r in range(n_rows):
            pad_scr[j, r, CONV_PAD - j:CONV_PAD - j + row_len, :] = u[r * row_len:(r + 1) * row_len, :]
    base = CONV_PAD - DW_CONV_SIZE // 2
    for cb in range(cc // LANES):
        ls = slice(cb * LANES, (cb + 1) * LANES)
        taps = [dww_ref[kk:kk + 1, ls] for kk in range(DW_CONV_SIZE)]
        for r in range(n_rows):
            acc = jnp.zeros((row_len, LANES), F32)
            for kk in range(DW_CONV_SIZE):
                hi, lo = divmod(base + kk, SUBLANES)
                acc += pad_scr[lo, r, hi * SUBLANES:hi * SUBLANES + row_len, ls] * taps[kk]
            conv_scr[r * row_len:(r + 1) * row_len, ls] = acc
    cv = _layer_norm(conv_scr[...] + dwb_ref[...], clw_ref[...], clb_ref[...])
    cv = cv * _sigmoid(cv)
    y_c = jnp.dot(cv.astype(BF16), wc_ref[...], preferred_element_type=F32)

    su = _gelu_tanh(z_ref[0, :, off_sg:off_sg + cc].astype(F32))
    sv = _gelu_tanh(z_ref[0, :, off_sg + cc:off_sg + 2 * cc].astype(F32))
    sv = _layer_norm(sv, slw_ref[...], slb_ref[...]).astype(BF16)
    gd = cc // SG_GROUPS
    for ch in range(tm // SG_CHUNK):
        rs = slice(ch * SG_CHUNK, (ch + 1) * SG_CHUNK)
        parts = []
        for gi in range(SG_GROUPS):
            mixed = jnp.dot(sgw_ref[gi], sv[rs, gi * gd:(gi + 1) * gd], preferred_element_type=F32)
            parts.append(mixed + sgb_ref[:, gi:gi + 1])
        gated = su[rs, :] * jnp.concatenate(parts, axis=1)
        conv_scr[rs, :] = gated
    y_s = jnp.dot(conv_scr[...].astype(BF16), ws_ref[...], preferred_element_type=F32)

    gm0 = _sigmoid(z_ref[0, :, off_merge:off_merge + d])
    gm1 = _sigmoid(z_ref[0, :, off_merge + d:off_merge + 2 * d])
    gm2 = _sigmoid(z_ref[0, :, off_merge + 2 * d:off_merge + 3 * d])
    merged = gm0 * y_m.astype(BF16) + gm1 * y_c.astype(BF16) + gm2 * y_s.astype(BF16)
    y = jnp.dot(merged, wo_ref[...], preferred_element_type=F32)
    o_ref[0] = x_ref[0] + g1_ref[0] * y


def _mixer_out(z, zblk, hf, hb, x, g1, p, row_len, tm):
    b, t, d = x.shape
    cc = d // 2
    tm = min(tm, t)
    nz = 6 * d
    full = lambda a: pl.BlockSpec(a.shape, lambda bi, i: (0,) * a.ndim)
    row = lambda a: a.reshape(1, -1)
    consts = [row(p["mlstm_norm_w"]), p["w_mlstm_out"], p["conv_dw_w"], row(p["conv_dw_b"]), row(p["conv_ln_w"]),
              row(p["conv_ln_b"]), p["w_conv_out"], row(p["sg_ln_w"]), row(p["sg_ln_b"]), p["sg_w"], p["sg_b"].T,
              p["w_sg_out"], p["w_o"]]
    tok = lambda w: pl.BlockSpec((1, tm, w), lambda bi, i: (bi, i, 0))
    return pl.pallas_call(
        functools.partial(_mixer_kernel, d=d, row_len=row_len),
        grid=(b, t // tm),
        in_specs=[pl.BlockSpec((1, tm, nz), lambda bi, i: (bi, i, zblk)), tok(d), tok(d), tok(d),
                  pl.BlockSpec((1, 1, d), lambda bi, i: (bi, 0, 0))] + [full(a) for a in consts],
        out_specs=tok(d),
        out_shape=jax.ShapeDtypeStruct((b, t, d), F32),
        scratch_shapes=[pltpu.VMEM((SUBLANES, tm // row_len, row_len + 2 * CONV_PAD, cc), F32),
                        pltpu.VMEM((tm, cc), F32)],
        compiler_params=_params(("arbitrary", "arbitrary")),
        name="mixer_out",
    )(z, hf, hb, x, g1, *consts)


def _router_kernel(x_ref, nw_ref, sh_ref, sc_ref, wr_ref, rb_ref, hp_ref, at_ref):
    h = _norm_mod(x_ref[0], nw_ref[...], sh_ref[0], sc_ref[0])
    h_hi, h_lo = _split_bf16(h)
    wr = wr_ref[...]
    raw = jnp.dot(h_hi, wr, preferred_element_type=F32) + jnp.dot(h_lo, wr, preferred_element_type=F32)
    logits = raw[:, :N_EXPERTS] + raw[:, N_EXPERTS:2 * N_EXPERTS] + rb_ref[...]
    mx = jnp.max(logits, axis=-1, keepdims=True)
    ex = jnp.exp(logits - mx)
    aff = ex / jnp.sum(ex, axis=-1, keepdims=True)
    apad = jnp.concatenate([aff, jnp.zeros((aff.shape[0], LANES - N_EXPERTS), F32)], axis=1)
    at_ref[0] = apad.T[:N_EXPERTS, :]
    half = h.shape[1] // 2
    bits = lax.bitcast_convert_type(h_hi.astype(F32), U32)
    hp_ref[0, :, :half] = (bits[:, :half] >> 16) | (bits[:, half:] & jnp.uint32(0xFFFF0000))
    hp_ref[0, :, half:] = lax.bitcast_convert_type(apad, U32)


def _router(x, nw, shift, scale, wr2, rb, tm):
    b, t, d = x.shape
    tm = min(tm, t)
    return pl.pallas_call(
        _router_kernel,
        grid=(b, t // tm),
        in_specs=[pl.BlockSpec((1, tm, d), lambda bi, i: (bi, i, 0)),
                  pl.BlockSpec((1, d), lambda bi, i: (0, 0)),
                  pl.BlockSpec((1, 1, d), lambda bi, i: (bi, 0, 0)),
                  pl.BlockSpec((1, 1, d), lambda bi, i: (bi, 0, 0)),
                  pl.BlockSpec((d, LANES), lambda bi, i: (0, 0)),
                  pl.BlockSpec((1, N_EXPERTS), lambda bi, i: (0, 0))],
        out_specs=[pl.BlockSpec((1, tm, d // 2 + LANES), lambda bi, i: (bi, i, 0)),
                   pl.BlockSpec((1, N_EXPERTS, tm), lambda bi, i: (bi, 0, i))],
        out_shape=[jax.ShapeDtypeStruct((b, t, d // 2 + LANES), U32), jax.ShapeDtypeStruct((b, N_EXPERTS, t), F32)],
        compiler_params=_params(("arbitrary", "arbitrary")),
        name="router",
    )(x, nw.reshape(1, d), shift, scale, wr2, rb.reshape(1, N_EXPERTS))


def _prefix_count(mask, tri):
    e, t = mask.shape
    nb = t // LANES
    stacked = jnp.concatenate([mask[:, c * LANES:(c + 1) * LANES] for c in range(nb)], axis=0).astype(BF16)
    local = jnp.dot(stacked, tri, preferred_element_type=F32)
    out = []
    off = jnp.zeros((e, 1), F32)
    for c in range(nb):
        blk = local[c * e:(c + 1) * e, :]
        out.append(blk + off)
        off = off + blk[:, LANES - 1:LANES]
    return jnp.concatenate(out, axis=1)


def _slots_two_level(sel, tri, idx_ref, cap):
    e, t = sel.shape
    nb = t // LANES
    n = nb * e
    stacked = jnp.concatenate([sel[:, c * LANES:(c + 1) * LANES] for c in range(nb)], axis=0).astype(BF16)
    local_bm = jnp.dot(stacked, tri, preferred_element_type=F32)
    ri = lax.broadcasted_iota(I32, (n, n), 0)
    ci = lax.broadcasted_iota(I32, (n, n), 1)
    perm = (ci == (ri % nb) * e + ri // nb).astype(BF16)
    local = jnp.dot(perm, local_bm.astype(BF16), preferred_element_type=F32)
    r128 = lax.broadcasted_iota(I32, (LANES, LANES), 0)
    last = (r128 == LANES - 1).astype(BF16)
    tot = jnp.dot(local.astype(BF16), last, preferred_element_type=F32)
    before = ((ri // nb == ci // nb) & (ci < ri)).astype(BF16)
    offs = jnp.dot(before, tot.astype(BF16), preferred_element_type=F32)
    cnt = local + offs
    c_end = tot + offs
    c_hi = jnp.floor(cnt * (1.0 / 64.0))
    c_lo = cnt - 64.0 * c_hi
    jrow = lax.broadcasted_iota(I32, (1, cap), 1).astype(F32)
    blk_id = lax.broadcasted_iota(I32, (nb, cap), 0).astype(F32)
    ones_b = jnp.ones((SUBLANES, nb), BF16)
    ones_r = jnp.ones((SUBLANES, LANES), BF16)
    tn = (((0,), (0,)), ((), ()))
    for ei in range(e):
        rs = slice(ei * nb, (ei + 1) * nb)
        ce = jnp.concatenate([c_end[rs, :]] * (cap // LANES), axis=1)
        done = jnp.where(ce <= jrow, 1.0, 0.0).astype(BF16)
        bj = jnp.dot(ones_b, done, preferred_element_type=F32)[0:1, :]
        pick = jnp.where(blk_id == bj, 1.0, 0.0).astype(BF16)
        row = (64.0 * lax.dot_general(c_hi[rs, :].astype(BF16), pick, tn, preferred_element_type=F32)
               + lax.dot_general(c_lo[rs, :].astype(BF16), pick, tn, preferred_element_type=F32))
        inside = jnp.where(row <= jrow, 1.0, 0.0).astype(BF16)
        off = jnp.dot(ones_r, inside, preferred_element_type=F32)[0:1, :]
        idx_ref[0, ei, :, :] = (LANES * bj + off).astype(I32)


def _select_kernel(at_ref, idx_ref, cnt_scr, icol_scr, *, cap, jt, two_level):
    a = at_ref[0]
    e, t = a.shape
    bits = lax.bitcast_convert_type(a, I32)

    def search(i, v):
        cand = v | jnp.left_shift(jnp.int32(1), 30 - i)
        cnt = jnp.sum((bits >= cand).astype(I32), axis=1, keepdims=True)
        return jnp.where(cnt >= cap, cand, v)

    thr = lax.fori_loop(0, 31, search, jnp.zeros((e, 1), I32))
    r = lax.broadcasted_iota(I32, (LANES, LANES), 0)
    s = lax.broadcasted_iota(I32, (LANES, LANES), 1)
    tri = (r <= s).astype(BF16)
    gt = bits > thr
    eq = bits == thr
    need = (cap - jnp.sum(gt.astype(I32), axis=1, keepdims=True)).astype(F32)
    sel = gt | (eq & (_prefix_count(eq.astype(F32), tri) <= need))
    if two_level:
        _slots_two_level(sel.astype(F32), tri, idx_ref, cap)
        return
    cnt = _prefix_count(sel.astype(F32), tri)
    for ei in range(e):
        cnt_scr[ei] = cnt[ei:ei + 1, :]
    icol_scr[...] = jnp.zeros_like(icol_scr)
    lane = lax.broadcasted_iota(I32, (jt, LANES), 1)

    def per_expert(ei, carry):
        for jb in range(cap // jt):
            jcol = (lax.broadcasted_iota(I32, (jt, 1), 0) + jb * jt).astype(F32)
            acc_n = jnp.zeros((jt, LANES), F32)
            for tb in range(t // LANES):
                cn = cnt_scr[ei, :, tb * LANES:(tb + 1) * LANES]
                acc_n += jnp.where(cn <= jcol, 1.0, 0.0)
            tok = jnp.sum(acc_n, axis=1, keepdims=True)
            rows = slice(jb * jt, (jb + 1) * jt)
            icol_scr[rows, :] = jnp.where(lane == ei, tok, icol_scr[rows, :])
        return carry

    lax.fori_loop(0, e, per_expert, 0)
    idx_ref[0, :, 0, :] = icol_scr[...].T[:e, :cap].astype(I32)


def _route_select(aff_t, cap):
    b, e, t = aff_t.shape
    jt = min(cap, 64)
    return pl.pallas_call(
        functools.partial(_select_kernel, cap=cap, jt=jt, two_level=(t // LANES) % SUBLANES == 0),
        grid=(b,),
        in_specs=[pl.BlockSpec((1, e, t), lambda bi: (bi, 0, 0))],
        out_specs=pl.BlockSpec((1, e, 1, cap), lambda bi: (bi, 0, 0, 0)),
        out_shape=jax.ShapeDtypeStruct((b, e, 1, cap), I32),
        scratch_shapes=[pltpu.VMEM((e, 1, t), F32), pltpu.VMEM((max(cap, LANES), LANES), F32)],
        compiler_params=_params(("arbitrary",)),
        name="route_select",
    )(aff_t)


def _moe_kernel(idxp_ref, idxn_ref, hp_ref, g2_ref, wg_ref, wu_ref, wd_ref, fnw_ref, x_hbm, o_hbm,
                xe32_scr, xe_scr, gate_scr, y_scr, yg_scr, acc_scr, sem_in, sem_out,
                *, gsz, cap, t, n_groups, n_f, final_norm):
    grp = pl.program_id(0)
    e = pl.program_id(1)
    f = pl.program_id(2)
    n_e = pl.num_programs(1)
    half = hp_ref.shape[2] - LANES
    q_rows = cap // n_f

    def in_copy(gi):
        return pltpu.make_async_copy(x_hbm.at[pl.ds(gi * gsz, gsz)], acc_scr.at[:, pl.ds(0, t)], sem_in)

    def out_copy(gi):
        return pltpu.make_async_copy(acc_scr.at[:, pl.ds(0, t)], o_hbm.at[pl.ds(gi * gsz, gsz)], sem_out)

    def unpack(expert):
        p = xe32_scr[:, :half]
        xe_scr[:, :half] = lax.bitcast_convert_type(p << 16, F32).astype(BF16)
        xe_scr[:, half:] = lax.bitcast_convert_type(p & jnp.uint32(0xFFFF0000), F32).astype(BF16)
        aff = lax.bitcast_convert_type(xe32_scr[:, half:], F32)
        lane = lax.broadcasted_iota(I32, aff.shape, 1)
        gate = jnp.sum(jnp.where(lane == expert, aff, 0.0), axis=1, keepdims=True)
        gate_scr[...] = jnp.broadcast_to(gate, gate_scr.shape)

    def gated():
        for si in range(gsz):
            rs = slice(si * cap, (si + 1) * cap)
            g2 = g2_ref[si]
            for cbk in range(y_scr.shape[1] // LANES):
                ls = slice(cbk * LANES, (cbk + 1) * LANES)
                yg_scr[rs, ls] = y_scr[rs, ls] * gate_scr[rs, :] * g2[:, ls]

    @pl.when((e == 0) & (f == 0))
    def _():
        @pl.when(grp > 0)
        def _():
            out_copy(grp - 1).wait()
        in_copy(grp).start()
        for si in range(gsz):
            def gather(j, carry):
                tkn = idxp_ref[si, 0, 0, j]
                xe32_scr[pl.ds(si * cap + j, 1), :] = hp_ref[si, pl.ds(tkn, 1), :]
                return carry
            lax.fori_loop(0, cap, gather, 0, unroll=8)
        unpack(e)
        y_scr[...] = jnp.zeros_like(y_scr)
        yg_scr[...] = jnp.zeros_like(yg_scr)
        acc_scr[:, t:t + 8, :] = jnp.zeros((gsz, 8, acc_scr.shape[2]), F32)

    @pl.when((e > 0) & (f == 0))
    def _():
        gated()
        unpack(e)

    @pl.when((e == 1) & (f == 0))
    def _():
        in_copy(grp).wait()

    xe = xe_scr[...]
    a = jnp.dot(xe, wg_ref[0], preferred_element_type=F32)
    u = jnp.dot(xe, wu_ref[0], preferred_element_type=F32)
    hid = (a * _sigmoid(a) * u).astype(BF16)
    part = jnp.dot(hid, wd_ref[0].astype(BF16), preferred_element_type=F32)
    spare = jnp.where(e == 0, 1, 0)
    for si in range(gsz):
        for jj in range(q_rows):
            j = f * q_rows + jj
            tkn = idxn_ref[si, 0, 0, j]
            xe32_scr[pl.ds(si * cap + j, 1), :] = hp_ref[si, pl.ds(tkn, 1), :]
        for jj in range(q_rows):
            j = f * q_rows + jj
            tkn = jnp.where(spare == 1, t, idxp_ref[si, 0, 0, j])
            acc_scr[si, pl.ds(tkn, 1), :] += yg_scr[pl.ds(si * cap + j, 1), :]

    y_scr[...] = jnp.where(f > 0, y_scr[...], 0.0) + part

    @pl.when((e == n_e - 1) & (f == n_f - 1))
    def _():
        gated()
        for si in range(gsz):
            def scatter(j, carry):
                tkn = idxn_ref[si, 0, 0, j]
                acc_scr[si, pl.ds(tkn, 1), :] += yg_scr[pl.ds(si * cap + j, 1), :]
                return carry
            lax.fori_loop(0, cap, scatter, 0, unroll=8)
        if final_norm:
            def norm_rows(r, carry):
                rows = pl.ds(pl.multiple_of(r * NORM_ROWS, NORM_ROWS), NORM_ROWS)
                for si in range(gsz):
                    blk = acc_scr[si, rows, :]
                    inv = lax.rsqrt(jnp.mean(blk * blk, axis=-1, keepdims=True) + EPS)
                    acc_scr[si, rows, :] = blk * inv * fnw_ref[...]
                return carry
            lax.fori_loop(0, t // NORM_ROWS, norm_rows, 0)
        out_copy(grp).start()

        @pl.when(grp == n_groups - 1)
        def _():
            out_copy(grp).wait()


def _moe_ffn(idx, hp, g2, x, wg, wu, wd, fnw, layer, gsz, tf, final_norm):
    b, t, d = x.shape
    e, cap = idx.shape[1], idx.shape[3]
    ff = wg.shape[2]
    n_groups, n_f = b // gsz, ff // tf
    hw = hp.shape[2]
    smem_idx = lambda fn: pl.BlockSpec((gsz, 1, 1, cap), fn, memory_space=pltpu.SMEM)
    return pl.pallas_call(
        functools.partial(_moe_kernel, gsz=gsz, cap=cap, t=t, n_groups=n_groups, n_f=n_f, final_norm=final_norm),
        grid=(n_groups, e, n_f),
        in_specs=[smem_idx(lambda gi, ei, fi: (gi, jnp.maximum(ei - 1, 0), 0, 0)),
                  smem_idx(lambda gi, ei, fi: (gi, jnp.minimum(ei + 1, e - 1), 0, 0)),
                  pl.BlockSpec((gsz, t, hw), lambda gi, ei, fi: (gi, 0, 0), pipeline_mode=pl.Buffered(1)),
                  pl.BlockSpec((gsz, 1, d), lambda gi, ei, fi: (gi, 0, 0)),
                  pl.BlockSpec((1, d, tf), lambda gi, ei, fi: (ei, 0, fi)),
                  pl.BlockSpec((1, d, tf), lambda gi, ei, fi: (ei, 0, fi)),
                  pl.BlockSpec((None, 1, tf, d), lambda gi, ei, fi: (layer, ei, fi, 0)),
                  pl.BlockSpec((1, d), lambda gi, ei, fi: (0, 0)),
                  pl.BlockSpec(memory_space=pl.ANY)],
        out_specs=pl.BlockSpec(memory_space=pl.ANY),
        out_shape=jax.ShapeDtypeStruct((b, t, d), F32),
        scratch_shapes=[pltpu.VMEM((gsz * cap, hw), U32), pltpu.VMEM((gsz * cap, d), BF16),
                        pltpu.VMEM((gsz * cap, LANES), F32),
                        pltpu.VMEM((gsz * cap, d), F32), pltpu.VMEM((gsz * cap, d), F32),
                        pltpu.VMEM((gsz, t + 8, d), F32),
                        pltpu.SemaphoreType.DMA(()), pltpu.SemaphoreType.DMA(())],
        compiler_params=_params(("arbitrary", "arbitrary", "arbitrary")),
        name="moe_ffn",
    )(idx, idx, hp, g2, wg, wu, wd, fnw.reshape(1, d), x)


def _stack_hi_lo(w):
    hi, lo = _split_bf16(w)
    pad = jnp.zeros((w.shape[0], LANES - 2 * w.shape[1]), BF16)
    return jnp.concatenate([hi, lo, pad], axis=1)


def _moe(x, nw, shift, scale, g2, wr2, rb, wg, wu, wd, fnw, layer, gsz, tm, final_norm=False):
    t = x.shape[1]
    cap = CAPACITY_FACTOR * t // N_EXPERTS
    hp, aff_t = _router(x, nw, shift, scale, wr2, rb, tm)
    idx = _route_select(aff_t, cap)
    return _moe_ffn(idx, hp, g2, x, wg, wu, wd, fnw, layer, gsz, tf=1024, final_norm=final_norm)


def kernel(x, c, ctx, c_ctx, ada_w, ada_b, norm1_w, norm2_w, w_in, mlstm_gate_b, mlstm_norm_w, w_mlstm_out,
           conv_dw_w, conv_dw_b, conv_ln_w, conv_ln_b, w_conv_out, sg_ln_w, sg_ln_b, sg_w, sg_b, w_sg_out, w_o,
           router_w, router_b, expert_w_gate, expert_w_up, expert_w_down, final_norm_w):
    depth = ada_w.shape[0]
    b, t, d = x.shape
    dh = d // N_HEADS
    n_state = 3 * d + N_GATE_COLS
    cond = jnp.concatenate([c, c_ctx[None, :], jnp.zeros((16 - b - 1, d), F32)], axis=0)
    wd = expert_w_down
    tc = ctx.shape[1]
    w_all = _w_regroup(w_in, 3 * d, n_state)

    for layer in range(depth):
        need_ctx = layer < depth - 1
        mod = _modulation(cond, ada_w, ada_b, layer)
        lat = [m[:, None, :] for m in jnp.split(mod[:b], 6, axis=-1)]
        cx = [jnp.broadcast_to(m[None], (b, 1, d)) for m in jnp.split(mod[b:b + 1], 6, axis=-1)]

        w_gate2 = _stack_hi_lo(w_in[layer, :, 3 * d:n_state])
        z_lat, gt_lat = _in_proj(x, norm1_w[layer], lat[0], lat[1], w_all, layer, 0, 9 * d, w_gate2,
                                 mlstm_gate_b[layer], tm=2048, tn=2304)
        col0, n_ctx = (0, 9 * d) if need_ctx else (6 * d, 3 * d)
        z_ctx, gt_ctx = _in_proj(ctx.reshape(1, b * tc, d), norm1_w[layer], cx[0][:1], cx[1][:1], w_all, layer,
                                 col0, n_ctx, w_gate2, mlstm_gate_b[layer], tm=2048, tn=1536)
        z_ctx = z_ctx.reshape(b, tc, z_ctx.shape[2])
        gt_ctx = gt_ctx.reshape(N_GATE_COLS, b, tc).transpose(1, 0, 2)

        hcf, hcb, state, _ = _mlstm(z_ctx, 2 if need_ctx else 0, gt_ctx, None, dh, ch=2)
        hlf, hlb, _, (wg, wu) = _mlstm(z_lat, 2, gt_lat, state, dh, ch=4,
                                       cast=(expert_w_gate, expert_w_up), cast_layer=layer)

        bf = lambda a: a[layer].astype(BF16)
        p = dict(mlstm_norm_w=mlstm_norm_w[layer], w_mlstm_out=bf(w_mlstm_out), conv_dw_w=conv_dw_w[layer],
                 conv_dw_b=conv_dw_b[layer], conv_ln_w=conv_ln_w[layer], conv_ln_b=conv_ln_b[layer],
                 w_conv_out=bf(w_conv_out), sg_ln_w=sg_ln_w[layer], sg_ln_b=sg_ln_b[layer], sg_w=bf(sg_w),
                 sg_b=sg_b[layer], w_sg_out=bf(w_sg_out), w_o=bf(w_o))
        x = _mixer_out(z_lat, 0, hlf, hlb, x, lat[2], p, row_len=GRID_W, tm=512)
        if need_ctx:
            ctx = _mixer_out(z_ctx, 0, hcf, hcb, ctx, cx[2], p, row_len=ctx.shape[1], tm=256)

        wr2 = _stack_hi_lo(router_w[layer])
        x = _moe(x, norm2_w[layer], lat[3], lat[4], lat[5], wr2, router_b[layer], wg, wu, wd, final_norm_w, layer,
                 gsz=1, tm=1024, final_norm=layer == depth - 1)
        if need_ctx:
            ctx = _moe(ctx, norm2_w[layer], cx[3], cx[4], cx[5], wr2, router_b[layer], wg, wu, wd, final_norm_w,
                       layer, gsz=b, tm=256)

    return x
```
